```python
import jax, jax.numpy as jnp
from jax import lax
import numpy as np

D_MODEL = 1024
BATCH = 4
SEQ = 8192
DEPTH = 1
DEC_BATCH = 32
DEC_SEQ = 1
PAST_LEN = 16384
PAGE_SIZE = 128

POOL_WIDTH = D_MODEL // 2
POOL_WINDOWS = (2, 4, 8, 16)
N_POOL_GROUPS = len(POOL_WINDOWS)
POOL_GROUP = POOL_WIDTH // N_POOL_GROUPS
POOL_HIST = max(POOL_WINDOWS) - 1
ATTN_WIDTH = D_MODEL - POOL_WIDTH
HEAD_DIM = 64
N_HEADS = ATTN_WIDTH // HEAD_DIM
ROT_DIM = HEAD_DIM // 4
ROPE_THETA = 500000.0
MOBA_BLOCK = 256
MOBA_TOPK = 3
Q_CHUNK = 64
N_EXPERT_GROUPS = 4
EXPERTS_PER_GROUP = 4
N_EXPERTS = N_EXPERT_GROUPS * EXPERTS_PER_GROUP
EXPERT_TOPK = 2
D_EXPERT = D_MODEL // 2
RMS_EPS = 1e-6

kernel_name = "hymba_pool_moba_hiermoe_step"


def rmsnorm(x, g):
    xf = x.astype(jnp.float32)
    r = lax.rsqrt(jnp.mean(xf * xf, axis=-1, keepdims=True) + RMS_EPS)
    return (xf * r * g.astype(jnp.float32)).astype(x.dtype)


def rotary(x, pos):
    inv = ROPE_THETA ** (-(jnp.arange(0, ROT_DIM, 2, dtype=jnp.float32) / ROT_DIM))
    ang = pos.astype(jnp.float32)[:, None] * inv[None, :]
    cos = jnp.cos(ang)[:, None, :]
    sin = jnp.sin(ang)[:, None, :]
    xr = x[..., :ROT_DIM].astype(jnp.float32)
    x1, x2 = xr[..., :ROT_DIM // 2], xr[..., ROT_DIM // 2:]
    rot = jnp.concatenate([x1 * cos - x2 * sin, x2 * cos + x1 * sin], axis=-1)
    return jnp.concatenate([rot.astype(x.dtype), x[..., ROT_DIM:]], axis=-1)


def in_project(xn, w_in, pos):
    B, S, _ = xn.shape
    proj = xn @ w_in
    u = proj[..., :POOL_WIDTH]
    q, k, v = jnp.split(proj[..., POOL_WIDTH:], 3, axis=-1)
    q = rotary(q.reshape(B, S, N_HEADS, HEAD_DIM), pos)
    k = rotary(k.reshape(B, S, N_HEADS, HEAD_DIM), pos)
    v = v.reshape(B, S, N_HEADS, HEAD_DIM)
    return u, q, k, v


def pool_mix(ext, n_prev, w_pool, pool_scale):
    B, n_tot, C = ext.shape
    ef = ext.astype(jnp.float32)
    c = jnp.concatenate([jnp.zeros((B, 1, C), jnp.float32), jnp.cumsum(ef, axis=1)], axis=1)
    j = np.arange(n_prev, n_tot)
    hi = c[:, n_prev + 1:]
    u = ef[:, n_prev:]
    outs = []
    for g, w in enumerate(POOL_WINDOWS):
        sl = slice(g * POOL_GROUP, (g + 1) * POOL_GROUP)
        lo_idx = np.maximum(j + 1 - w, 0)
        cnt = (j + 1 - lo_idx).astype(np.float32)
        mean = (hi[..., sl] - c[:, lo_idx, sl]) / cnt[None, :, None]
        outs.append(mean - u[..., sl])
    d = jnp.stack(outs, axis=2)
    y = jnp.einsum('bngc,gcd->bngd', d, w_pool.astype(jnp.float32)).reshape(B, -1, C)
    return (y * pool_scale.astype(jnp.float32)).astype(ext.dtype)


def moba_block(q, pos, kb, vb, kmean):
    B, Q, H, Dh = q.shape
    nb = kb.shape[1]
    cur = pos // MOBA_BLOCK
    gate = jnp.einsum('bqhd,bnhd->bhqn', q, kmean)
    fully_past = jnp.arange(nb)[None, :] < cur[:, None]
    gate = jnp.where(fully_past[None, None], gate, -jnp.inf)
    ksel = min(MOBA_TOPK, nb)
    _, top_i = lax.top_k(gate, ksel)
    own = jnp.broadcast_to(cur[None, None, :, None], (B, H, Q, 1)).astype(top_i.dtype)
    blk = jnp.concatenate([top_i, own], axis=-1)
    slot_ok = jnp.concatenate([jnp.arange(ksel)[None, :] < cur[:, None],
                               jnp.ones((Q, 1), bool)], axis=1)
    kpos = blk[..., None] * MOBA_BLOCK + jnp.arange(MOBA_BLOCK)
    allowed = slot_ok[None, None, :, :, None] & (kpos <= pos[None, None, :, None, None])
    bi = jnp.arange(B)[:, None, None, None]
    hi = jnp.arange(H)[None, :, None, None]
    kg = kb[bi, blk, :, hi]
    vg = vb[bi, blk, :, hi]
    s = jnp.einsum('bqhd,bhqskd->bhqsk', q, kg) * (HEAD_DIM ** -0.5)
    s = jnp.where(allowed, s, -jnp.inf).reshape(B, H, Q, -1)
    p = jax.nn.softmax(s, axis=-1).reshape(blk.shape + (MOBA_BLOCK,))
    return jnp.einsum('bhqsk,bhqskd->bqhd', p, vg)


def moba_attend(q, q_pos, k_all, v_all):
    B, L, H, Dh = k_all.shape
    nb = -(-L // MOBA_BLOCK)
    pad = nb * MOBA_BLOCK - L
    kf, vf = k_all.astype(jnp.float32), v_all.astype(jnp.float32)
    if pad:
        kf = jnp.pad(kf, ((0, 0), (0, pad), (0, 0), (0, 0)))
        vf = jnp.pad(vf, ((0, 0), (0, pad), (0, 0), (0, 0)))
    kb = kf.reshape(B, nb, MOBA_BLOCK, H, Dh)
    vb = vf.reshape(B, nb, MOBA_BLOCK, H, Dh)
    kmean = jnp.mean(kb, axis=2)
    Q = q.shape[1]
    qf = q.astype(jnp.float32)
    if Q > Q_CHUNK and Q % Q_CHUNK == 0:
        nc = Q // Q_CHUNK
        qc = jnp.swapaxes(qf.reshape(B, nc, Q_CHUNK, H, Dh), 0, 1)
        pc = q_pos.reshape(nc, Q_CHUNK)
        o = lax.map(lambda a: moba_block(a[0], a[1], kb, vb, kmean), (qc, pc))
        o = jnp.swapaxes(o, 0, 1).reshape(B, Q, H, Dh)
    else:
        o = moba_block(qf, q_pos, kb, vb, kmean)
    return o.astype(q.dtype)


def hier_moe(x, w_rg, w_re, w_gate, w_up, w_down):
    B, S, D = x.shape
    xt = x.reshape(B * S, D)
    pg = jax.nn.softmax((xt @ w_rg).astype(jnp.float32), axis=-1)
    pg_sel, g_sel = lax.top_k(pg, 1)
    le = (xt @ w_re).astype(jnp.float32).reshape(-1, N_EXPERT_GROUPS, EXPERTS_PER_GROUP)
    le_g = jnp.take_along_axis(le, g_sel[:, :, None], axis=1)[:, 0]
    w_top, i_top = lax.top_k(jax.nn.softmax(le_g, axis=-1), EXPERT_TOPK)
    w_top = w_top / jnp.sum(w_top, axis=-1, keepdims=True) * pg_sel
    e_idx = g_sel * EXPERTS_PER_GROUP + i_top
    gate = jnp.sum(jax.nn.one_hot(e_idx, N_EXPERTS, dtype=jnp.float32) * w_top[..., None], axis=1)
    y = jnp.zeros((B * S, D), jnp.float32)
    for e in range(N_EXPERTS):
        h = jax.nn.silu(xt @ w_gate[e]) * (xt @ w_up[e])
        y = y + gate[:, e:e + 1] * (h @ w_down[e]).astype(jnp.float32)
    return y.astype(x.dtype).reshape(B, S, D)


def setup_inputs(seed: int = 0) -> dict:
    key = jax.random.key(seed)
    ks = jax.random.split(key, 20)
    f32 = jnp.float32
    n_pages = PAST_LEN // PAGE_SIZE
    n_used = DEC_BATCH * n_pages
    n_phys = n_used + max(n_used // 4, 1)

    def nrm(k, shape, scale):
        return jax.random.normal(k, shape, f32) * scale

    x_prompt = nrm(ks[0], (BATCH, SEQ, D_MODEL), 1.0)
    x_sample = nrm(ks[1], (DEC_BATCH, DEC_SEQ, D_MODEL), 1.0)
    cache_k = nrm(ks[2], (DEPTH, n_phys, PAGE_SIZE, N_HEADS, HEAD_DIM), 1.0)
    cache_v = nrm(ks[3], (DEPTH, n_phys, PAGE_SIZE, N_HEADS, HEAD_DIM), 1.0)
    state_pool = nrm(ks[4], (DEPTH, DEC_BATCH, POOL_HIST, POOL_WIDTH), 1.0)
    page_table = jax.random.permutation(ks[5], n_phys)[:n_used].reshape(DEC_BATCH, n_pages).astype(jnp.int32)
    norm_mix = 1.0 + nrm(ks[6], (DEPTH, D_MODEL), 0.02)
    w_in = nrm(ks[7], (DEPTH, D_MODEL, POOL_WIDTH + 3 * ATTN_WIDTH), D_MODEL ** -0.5)
    w_pool = nrm(ks[8], (DEPTH, N_POOL_GROUPS, POOL_GROUP, POOL_GROUP), POOL_GROUP ** -0.5)
    pool_scale = 1.0 + nrm(ks[9], (DEPTH, POOL_WIDTH), 0.02)
    w_out = nrm(ks[10], (DEPTH, D_MODEL, D_MODEL), D_MODEL ** -0.5)
    norm_ffn = 1.0 + nrm(ks[11], (DEPTH, D_MODEL), 0.02)
    w_router_group = nrm(ks[12], (DEPTH, D_MODEL, N_EXPERT_GROUPS), D_MODEL ** -0.5)
    w_router_expert = nrm(ks[13], (DEPTH, D_MODEL, N_EXPERTS), D_MODEL ** -0.5)
    w_gate = nrm(ks[14], (DEPTH, N_EXPERTS, D_MODEL, D_EXPERT), D_MODEL ** -0.5)
    w_up = nrm(ks[15], (DEPTH, N_EXPERTS, D_MODEL, D_EXPERT), D_MODEL ** -0.5)
    w_down = nrm(ks[16], (DEPTH, N_EXPERTS, D_EXPERT, D_MODEL), D_EXPERT ** -0.5)
    norm_final = 1.0 + nrm(ks[17], (D_MODEL,), 0.02)
    return {"x_prompt": x_prompt, "x_sample": x_sample, "cache_k": cache_k, "cache_v": cache_v,
            "state_pool": state_pool, "page_table": page_table, "norm_mix": norm_mix, "w_in": w_in,
            "w_pool": w_pool, "pool_scale": pool_scale, "w_out": w_out, "norm_ffn": norm_ffn,
            "w_router_group": w_router_group, "w_router_expert": w_router_expert,
            "w_gate": w_gate, "w_up": w_up, "w_down": w_down, "norm_final": norm_final}


def reference(x_prompt, x_sample, cache_k, cache_v, state_pool, page_table, norm_mix, w_in, w_pool,
              pool_scale, w_out, norm_ffn, w_router_group, w_router_expert, w_gate, w_up, w_down,
              norm_final):
    B, S, _ = x_prompt.shape
    DB, DS, _ = x_sample.shape
    past_len = page_table.shape[1] * cache_k.shape[2]
    pos_p = jnp.arange(S, dtype=jnp.int32)
    pos_s = past_len + jnp.arange(DS, dtype=jnp.int32)
    hp, hs = x_prompt, x_sample
    kp_l, vp_l, pp_l, ks_l, vs_l, ps_l = [], [], [], [], [], []
    for l in range(DEPTH):
        xn = rmsnorm(hp, norm_mix[l])
        u, q, k, v = in_project(xn, w_in[l], pos_p)
        pool_o = pool_mix(u, 0, w_pool[l], pool_scale[l])
        attn_o = moba_attend(q, pos_p, k, v).reshape(B, S, ATTN_WIDTH)
        hp = hp + jnp.concatenate([pool_o, attn_o], axis=-1) @ w_out[l]
        hp = hp + hier_moe(rmsnorm(hp, norm_ffn[l]), w_router_group[l], w_router_expert[l],
                           w_gate[l], w_up[l], w_down[l])
        kp_l.append(k)
        vp_l.append(v)
        pp_l.append(u[:, S - POOL_HIST:])

        xn = rmsnorm(hs, norm_mix[l])
        u, q, k, v = in_project(xn, w_in[l], pos_s)
        ext = jnp.concatenate([state_pool[l].astype(u.dtype), u], axis=1)
        pool_o = pool_mix(ext, POOL_HIST, w_pool[l], pool_scale[l])
        k_past = cache_k[l, page_table].reshape(DB, past_len, N_HEADS, HEAD_DIM)
        v_past = cache_v[l, page_table].reshape(DB, past_len, N_HEADS, HEAD_DIM)
        k_all = jnp.concatenate([k_past.astype(k.dtype), k], axis=1)
        v_all = jnp.concatenate([v_past.astype(v.dtype), v], axis=1)
        attn_o = moba_attend(q, pos_s, k_all, v_all).reshape(DB, DS, ATTN_WIDTH)
        hs = hs + jnp.concatenate([pool_o, attn_o], axis=-1) @ w_out[l]
        hs = hs + hier_moe(rmsnorm(hs, norm_ffn[l]), w_router_group[l], w_router_expert[l],
                           w_gate[l], w_up[l], w_down[l])
        ks_l.append(k)
        vs_l.append(v)
        ps_l.append(ext[:, ext.shape[1] - POOL_HIST:])
    y_prompt = rmsnorm(hp, norm_final)
    y_sample = rmsnorm(hs, norm_final)
    return (y_prompt, y_sample, jnp.stack(kp_l), jnp.stack(vp_l), jnp.stack(pp_l),
            jnp.stack(ks_l), jnp.stack(vs_l), jnp.stack(ps_l))
```

```python
import functools

import jax
import jax.numpy as jnp
import numpy as np
from jax import lax
from jax.experimental import pallas as pl
from jax.experimental.pallas import tpu as pltpu

F32 = jnp.float32
BF16 = jnp.bfloat16
HIGHEST = lax.Precision.HIGHEST

POOL_WINDOWS = (2, 4, 8, 16)
POOL_GROUP = 128
POOL_WIDTH = POOL_GROUP * len(POOL_WINDOWS)
POOL_HIST = max(POOL_WINDOWS) - 1
HALO = POOL_HIST + 1
HEAD_DIM = 64
N_HEADS = 8
ATTN_WIDTH = N_HEADS * HEAD_DIM
ROT_DIM = HEAD_DIM // 4
ROPE_THETA = 500000.0
MOBA_BLOCK = 256
MOBA_TOPK = 3
N_EXPERT_GROUPS = 4
EXPERTS_PER_GROUP = 4
N_EXPERTS = N_EXPERT_GROUPS * EXPERTS_PER_GROUP
RMS_EPS = 1e-6

LANES = 128
HEADS_PER_TILE = LANES // HEAD_DIM
VMEM_LIMIT = 52 * 1024 * 1024

NEG_INF = float("-inf")
BIG_IDX = 1 << 20


def _cparams(sem):
    return pltpu.CompilerParams(dimension_semantics=sem, vmem_limit_bytes=VMEM_LIMIT)


def _rms(x, g):
    r = lax.rsqrt(jnp.mean(x * x, axis=-1, keepdims=True) + RMS_EPS)
    return x * r * g


def _rope_tables(pos):
    inv = ROPE_THETA ** (-(jnp.arange(0, ROT_DIM, 2, dtype=F32) / ROT_DIM))
    ang = pos.astype(F32)[:, None] * inv[None, :]
    cos, sin = jnp.cos(ang), jnp.sin(ang)
    half = ROT_DIM // 2
    d = np.arange(LANES) % HEAD_DIM
    fi = d % half
    cos_l, sin_l = cos[:, fi], sin[:, fi]
    c = jnp.where(d < ROT_DIM, cos_l, 1.0)
    s_lo = jnp.where(d < half, -sin_l, 0.0)
    s_hi = jnp.where((d >= half) & (d < ROT_DIM), sin_l, 0.0)
    return c.astype(F32), s_lo.astype(F32), s_hi.astype(F32)


def _rotate(x, c, s_lo, s_hi):
    half = ROT_DIM // 2
    outs = []
    for j in range(x.shape[1] // LANES):
        t = x[:, j * LANES:(j + 1) * LANES]
        up = pltpu.roll(t, LANES - half, axis=1)
        dn = pltpu.roll(t, half, axis=1)
        outs.append(t * c + up * s_lo + dn * s_hi)
    return jnp.concatenate(outs, axis=1)


def _inproj_kernel(x_ref, g_ref, w_ref, c_ref, slo_ref, shi_ref, wp_ref, ps_ref,
                   kf_ref, vf_ref, qb_ref, kb_ref, vb_ref, po_ref, ksum_ref, ulast_ref, ext_ref):
    s = pl.program_id(1)
    ts = x_ref.shape[0]
    xn = _rms(x_ref[...], g_ref[...]).astype(BF16)

    def proj(i):
        return jnp.dot(xn, w_ref[:, i * POOL_WIDTH:(i + 1) * POOL_WIDTH], preferred_element_type=F32)

    u, q, k, v = proj(0), proj(1), proj(2), proj(3)
    c, s_lo, s_hi = c_ref[...], slo_ref[...], shi_ref[...]
    q = _rotate(q, c, s_lo, s_hi)
    k = _rotate(k, c, s_lo, s_hi)
    kf_ref[...] = k
    vf_ref[...] = v
    qb_ref[...] = (q * (HEAD_DIM ** -0.5)).astype(BF16)
    kb_ref[...] = k.astype(BF16)
    vb_ref[...] = v.astype(BF16)
    nblk = ts // MOBA_BLOCK
    ksum_ref[...] = jnp.sum(k.reshape(nblk, MOBA_BLOCK, ATTN_WIDTH), axis=1)

    @pl.when(s == 0)
    def _():
        ext_ref[0:HALO, :] = jnp.zeros((HALO, POOL_WIDTH), F32)

    ext_ref[HALO:HALO + ts, :] = u
    pos1 = s * ts + lax.broadcasted_iota(jnp.int32, (ts, POOL_GROUP), 0) + 1
    outs = []
    for g, w in enumerate(POOL_WINDOWS):
        sl = slice(g * POOL_GROUP, (g + 1) * POOL_GROUP)
        ug = u[:, sl]
        acc = ug
        for j in range(1, w):
            acc = acc + ext_ref[HALO - j:HALO - j + ts, sl]
        cnt = jnp.minimum(pos1, w).astype(F32)
        d = (acc / cnt - ug).astype(BF16)
        outs.append(jnp.dot(d, wp_ref[g], preferred_element_type=F32))
    y = jnp.concatenate(outs, axis=1) * ps_ref[...]
    po_ref[...] = y.astype(BF16)
    tail = u[ts - HALO:ts, :]
    ext_ref[0:HALO, :] = tail

    @pl.when(s == pl.num_programs(1) - 1)
    def _():
        ulast_ref[...] = tail


def _inproj(x, g_mix, w_in_bf, w_pool_bf, pool_scale, ts):
    B, S, D = x.shape
    ns = S // ts
    nblk = ts // MOBA_BLOCK
    c, s_lo, s_hi = _rope_tables(jnp.arange(S, dtype=jnp.int32))
    row = lambda b, s: (b, s, 0)
    tab = pl.BlockSpec((ts, LANES), lambda b, s: (s, 0))
    full2 = lambda shape: pl.BlockSpec(shape, lambda b, s: (0, 0))
    act = lambda: pl.BlockSpec((None, ts, ATTN_WIDTH), row)
    out_shapes = (
        jax.ShapeDtypeStruct((B, S, ATTN_WIDTH), F32),
        jax.ShapeDtypeStruct((B, S, ATTN_WIDTH), F32),
        jax.ShapeDtypeStruct((B, S, ATTN_WIDTH), BF16),
        jax.ShapeDtypeStruct((B, S, ATTN_WIDTH), BF16),
        jax.ShapeDtypeStruct((B, S, ATTN_WIDTH), BF16),
        jax.ShapeDtypeStruct((B, S, POOL_WIDTH), BF16),
        jax.ShapeDtypeStruct((B, ns, nblk, ATTN_WIDTH), F32),
        jax.ShapeDtypeStruct((B, HALO, POOL_WIDTH), F32),
    )
    return pl.pallas_call(
        _inproj_kernel,
        grid=(B, ns),
        in_specs=[
            pl.BlockSpec((None, ts, D), row),
            full2((1, D)),
            full2(w_in_bf.shape),
            tab, tab, tab,
            pl.BlockSpec(w_pool_bf.shape, lambda b, s: (0, 0, 0)),
            full2((1, POOL_WIDTH)),
        ],
        out_specs=(act(), act(), act(), act(), act(), act(),
                   pl.BlockSpec((None, None, nblk, ATTN_WIDTH), lambda b, s: (b, s, 0, 0)),
                   pl.BlockSpec((None, HALO, POOL_WIDTH), lambda b, s: (b, 0, 0))),
        out_shape=out_shapes,
        scratch_shapes=[pltpu.VMEM((HALO + ts, POOL_WIDTH), F32)],
        compiler_params=_cparams(("arbitrary", "arbitrary")),
        name="prompt_inproj",
    )(x, g_mix, w_in_bf, c, s_lo, s_hi, w_pool_bf, pool_scale)


def _top_blocks(gate, n_iota, n_valid):
    picks = []
    g = gate
    for t in range(MOBA_TOPK):
        m = jnp.max(g, axis=1, keepdims=True)
        idx = jnp.min(jnp.where(g == m, n_iota, BIG_IDX), axis=1, keepdims=True)
        picks.append(jnp.where(t < n_valid, idx, -1))
        g = jnp.where(n_iota == idx, NEG_INF, g)
    return picks


def _moba_kernel(q_ref, k_ref, v_ref, ksum_ref, o_ref):
    c = pl.program_id(2)
    tq = q_ref.shape[0]
    nb = ksum_ref.shape[0]
    q = q_ref[...]
    kmean = (ksum_ref[...] * (1.0 / MOBA_BLOCK)).astype(BF16)
    lane = lax.broadcasted_iota(jnp.int32, (tq, LANES), 1)
    n_iota = lax.broadcasted_iota(jnp.int32, (tq, nb), 1)
    dims = (((1,), (1,)), ((), ()))

    qh, picks = [], []
    for h in range(HEADS_PER_TILE):
        in_head = (lane >= h * HEAD_DIM) & (lane < (h + 1) * HEAD_DIM)
        qm = jnp.where(in_head, q, jnp.zeros_like(q))
        gate = lax.dot_general(qm, kmean, dims, preferred_element_type=F32)
        gate = jnp.where(n_iota < c, gate, NEG_INF)
        qh.append(qm)
        picks.append([jnp.broadcast_to(p, (tq, LANES)) for p in _top_blocks(gate, n_iota, c)])

    start = pl.multiple_of(c * MOBA_BLOCK, MOBA_BLOCK)
    k_own = k_ref[pl.ds(start, MOBA_BLOCK), :]
    v_own = v_ref[pl.ds(start, MOBA_BLOCK), :]
    causal = (lax.broadcasted_iota(jnp.int32, (tq, MOBA_BLOCK), 1)
              <= lax.broadcasted_iota(jnp.int32, (tq, MOBA_BLOCK), 0))
    state = []
    for h in range(HEADS_PER_TILE):
        s = lax.dot_general(qh[h], k_own, dims, preferred_element_type=F32)
        s = jnp.where(causal, s, NEG_INF)
        m = jnp.max(s, axis=1, keepdims=True)
        p = jnp.exp(s - m)
        l = jnp.sum(p, axis=1, keepdims=True)
        acc = jnp.dot(p.astype(BF16), v_own, preferred_element_type=F32)
        state += [m, l, acc]

    def body(n, state):
        off = pl.multiple_of(n * MOBA_BLOCK, MOBA_BLOCK)
        k_n = k_ref[pl.ds(off, MOBA_BLOCK), :]
        v_n = v_ref[pl.ds(off, MOBA_BLOCK), :]
        new = []
        for h in range(HEADS_PER_TILE):
            m, l, acc = state[3 * h:3 * h + 3]
            p1, p2, p3 = picks[h]
            bias = jnp.where((p1 == n) | (p2 == n) | (p3 == n), 0.0, NEG_INF)
            bias = jnp.concatenate([bias] * (MOBA_BLOCK // LANES), axis=1)
            s = lax.dot_general(qh[h], k_n, dims, preferred_element_type=F32) + bias
            m_new = jnp.maximum(m, jnp.max(s, axis=1, keepdims=True))
            alpha = jnp.exp(m - m_new)
            p = jnp.exp(s - m_new)
            l = alpha * l + jnp.sum(p, axis=1, keepdims=True)
            acc = alpha * acc + jnp.dot(p.astype(BF16), v_n, preferred_element_type=F32)
            new += [m_new, l, acc]
        return tuple(new)

    state = lax.fori_loop(0, c, body, tuple(state))
    out = jnp.zeros((tq, LANES), F32)
    for h in range(HEADS_PER_TILE):
        m, l, acc = state[3 * h:3 * h + 3]
        in_head = (lane >= h * HEAD_DIM) & (lane < (h + 1) * HEAD_DIM)
        out = jnp.where(in_head, acc / l, out)
    o_ref[...] = out.astype(o_ref.dtype)


def _moba_prompt(q_bf, k_bf, v_bf, ksum):
    B, S, W = q_bf.shape
    nb = S // MOBA_BLOCK
    ntile = W // LANES
    blk = pl.BlockSpec((None, MOBA_BLOCK, LANES), lambda b, t, c: (b, c, t))
    seq = pl.BlockSpec((None, S, LANES), lambda b, t, c: (b, 0, t))
    return pl.pallas_call(
        _moba_kernel,
        grid=(B, ntile, nb),
        in_specs=[blk, seq, seq, pl.BlockSpec((None, nb, LANES), lambda b, t, c: (b, 0, t))],
        out_specs=blk,
        out_shape=jax.ShapeDtypeStruct((B, S, W), BF16),
        compiler_params=_cparams(("arbitrary", "arbitrary", "arbitrary")),
        name="prompt_moba",
    )(q_bf, k_bf, v_bf, ksum)


def _route(logits):
    lane = lax.broadcasted_iota(jnp.int32, logits.shape, 1)
    is_grp = (lane >= N_EXPERTS) & (lane < N_EXPERTS + N_EXPERT_GROUPS)
    gl = jnp.where(is_grp, logits, NEG_INF)
    ge = jnp.exp(gl - jnp.max(gl, axis=1, keepdims=True))
    pg = ge / jnp.sum(ge, axis=1, keepdims=True)
    pg_sel = jnp.max(pg, axis=1, keepdims=True)
    g_sel = jnp.min(jnp.where(is_grp & (pg == pg_sel), lane - N_EXPERTS, BIG_IDX), axis=1, keepdims=True)
    in_grp = (lane < N_EXPERTS) & (jnp.right_shift(lane, 2) == g_sel)
    el = jnp.where(in_grp, logits, NEG_INF)
    ee = jnp.exp(el - jnp.max(el, axis=1, keepdims=True))
    pe = jnp.where(in_grp, ee / jnp.sum(ee, axis=1, keepdims=True), -1.0)
    p1 = jnp.max(pe, axis=1, keepdims=True)
    i1 = jnp.min(jnp.where(pe == p1, lane, BIG_IDX), axis=1, keepdims=True)
    pe2 = jnp.where(lane == i1, -1.0, pe)
    p2 = jnp.max(pe2, axis=1, keepdims=True)
    i2 = jnp.min(jnp.where(pe2 == p2, lane, BIG_IDX), axis=1, keepdims=True)
    tot = p1 + p2
    return (jnp.where(lane == i1, p1 / tot * pg_sel, 0.0)
            + jnp.where(lane == i2, p2 / tot * pg_sel, 0.0))


def _gate_column(gate, e):
    lane = lax.broadcasted_iota(jnp.int32, gate.shape, 1)
    return jnp.sum(jnp.where(lane == e, gate, 0.0), axis=1, keepdims=True)


def _swiglu(x_bf, wg, wu, wd):
    hg = jnp.dot(x_bf, wg, preferred_element_type=F32)
    hu = jnp.dot(x_bf, wu, preferred_element_type=F32)
    h = hg * (1.0 / (1.0 + jnp.exp(-hg))) * hu
    return jnp.dot(h.astype(BF16), wd, preferred_element_type=F32)


def _outproj_kernel(x_ref, po_ref, ao_ref, w_ref, g_ref, wr_ref, h_ref, xn_ref, gate_ref):
    mix = (jnp.dot(po_ref[...], w_ref[0:POOL_WIDTH, :], preferred_element_type=F32)
           + jnp.dot(ao_ref[...], w_ref[POOL_WIDTH:, :], preferred_element_type=F32))
    h = x_ref[...] + mix
    h_ref[...] = h
    xn = _rms(h, g_ref[...]).astype(BF16)
    xn_ref[...] = xn
    gate_ref[...] = _route(jnp.dot(xn, wr_ref[...], preferred_element_type=F32))


def _outproj(x2d, po2d, ao2d, w_out_bf, g_ffn, w_router_bf, tm):
    T, D = x2d.shape
    row = lambda i: (i, 0)
    full = lambda shape: pl.BlockSpec(shape, lambda i: (0, 0))
    return pl.pallas_call(
        _outproj_kernel,
        grid=(T // tm,),
        in_specs=[pl.BlockSpec((tm, D), row), pl.BlockSpec((tm, POOL_WIDTH), row),
                  pl.BlockSpec((tm, ATTN_WIDTH), row), full(w_out_bf.shape), full((1, D)),
                  full(w_router_bf.shape)],
        out_specs=(pl.BlockSpec((tm, D), row), pl.BlockSpec((tm, D), row), pl.BlockSpec((tm, LANES), row)),
        out_shape=(jax.ShapeDtypeStruct((T, D), F32), jax.ShapeDtypeStruct((T, D), BF16),
                   jax.ShapeDtypeStruct((T, LANES), F32)),
        compiler_params=_cparams(("arbitrary",)),
        name="prompt_outproj",
    )(x2d, po2d, ao2d, w_out_bf, g_ffn, w_router_bf)


def _moe_kernel(xn_ref, gate_ref, h_ref, wg_ref, wu_ref, wd_ref, gf_ref, y_ref, acc_ref):
    e = pl.program_id(1)
    o = _swiglu(xn_ref[...], wg_ref[...], wu_ref[...], wd_ref[...])
    contrib = _gate_column(gate_ref[...], e) * o

    @pl.when(e == 0)
    def _():
        acc_ref[...] = contrib

    @pl.when(e > 0)
    def _():
        acc_ref[...] += contrib

    @pl.when(e == pl.num_programs(1) - 1)
    def _():
        y_ref[...] = _rms(h_ref[...] + acc_ref[...], gf_ref[...])


def _moe(xn2d, gate, h2d, wg_bf, wu_bf, wd_bf, g_final, tm):
    T, D = h2d.shape
    E, _, DE = wg_bf.shape
    row = lambda i, e: (i, 0)
    return pl.pallas_call(
        _moe_kernel,
        grid=(T // tm, E),
        in_specs=[pl.BlockSpec((tm, D), row), pl.BlockSpec((tm, LANES), row), pl.BlockSpec((tm, D), row),
                  pl.BlockSpec((None, D, DE), lambda i, e: (e, 0, 0)),
                  pl.BlockSpec((None, D, DE), lambda i, e: (e, 0, 0)),
                  pl.BlockSpec((None, DE, D), lambda i, e: (e, 0, 0)),
                  pl.BlockSpec((1, D), lambda i, e: (0, 0))],
        out_specs=pl.BlockSpec((tm, D), row),
        out_shape=jax.ShapeDtypeStruct((T, D), F32),
        scratch_shapes=[pltpu.VMEM((tm, D), F32)],
        compiler_params=_cparams(("arbitrary", "arbitrary")),
        name="prompt_moe",
    )(xn2d, gate, h2d, wg_bf, wu_bf, wd_bf, g_final)


def _sample_inproj_kernel(x_ref, g_ref, w_ref, c_ref, slo_ref, shi_ref, sp_ref, wp_ref, ps_ref,
                          u_ref, q_ref, k_ref, v_ref, po_ref):
    xn = _rms(x_ref[...], g_ref[...])

    def proj(i):
        return jnp.dot(xn, w_ref[:, i * POOL_WIDTH:(i + 1) * POOL_WIDTH],
                       preferred_element_type=F32, precision=HIGHEST)

    u, q, k, v = proj(0), proj(1), proj(2), proj(3)
    c, s_lo, s_hi = c_ref[...], slo_ref[...], shi_ref[...]
    u_ref[...] = u
    q_ref[...] = _rotate(q, c, s_lo, s_hi)
    k_ref[...] = _rotate(k, c, s_lo, s_hi)
    v_ref[...] = v
    outs = []
    for g, w in enumerate(POOL_WINDOWS):
        sl = slice(g * POOL_GROUP, (g + 1) * POOL_GROUP)
        ug = u[:, sl]
        acc = ug
        for j in range(1, w):
            acc = acc + sp_ref[POOL_HIST - j, :, sl]
        d = acc / float(w) - ug
        outs.append(jnp.dot(d, wp_ref[g], preferred_element_type=F32, precision=HIGHEST))
    po_ref[...] = jnp.concatenate(outs, axis=1) * ps_ref[...]


def _sample_inproj(x, g_mix, w_in, pos, sp_t, w_pool, pool_scale):
    DB, D = x.shape
    c, s_lo, s_hi = _rope_tables(pos)
    sds = jax.ShapeDtypeStruct((DB, POOL_WIDTH), F32)
    return pl.pallas_call(
        _sample_inproj_kernel,
        out_shape=(sds,) * 5,
        compiler_params=pltpu.CompilerParams(vmem_limit_bytes=VMEM_LIMIT),
        name="sample_inproj",
    )(x, g_mix, w_in, c, s_lo, s_hi, sp_t, w_pool, pool_scale)


PAGES_PER_STEP = 16


def _ksum_copies(pt_ref, cache_ref, buf_ref, sem_ref, step, slot):
    return [pltpu.make_async_copy(cache_ref.at[pt_ref[step * PAGES_PER_STEP + i]],
                                  buf_ref.at[slot, i], sem_ref.at[slot])
            for i in range(PAGES_PER_STEP)]


def _sample_ksum_kernel(pt_ref, cache_ref, o_ref, buf_ref, sem_ref):
    step = pl.program_id(0) * pl.num_programs(1) + pl.program_id(1)
    nsteps = pl.num_programs(0) * pl.num_programs(1)
    slot = step % 2

    @pl.when(step == 0)
    def _():
        for cp in _ksum_copies(pt_ref, cache_ref, buf_ref, sem_ref, step, slot):
            cp.start()

    @pl.when(step + 1 < nsteps)
    def _():
        for cp in _ksum_copies(pt_ref, cache_ref, buf_ref, sem_ref, step + 1, 1 - slot):
            cp.start()

    for cp in _ksum_copies(pt_ref, cache_ref, buf_ref, sem_ref, step, slot):
        cp.wait()
    pages_per_blk = MOBA_BLOCK // buf_ref.shape[2]
    for j in range(PAGES_PER_STEP // pages_per_blk):
        tot = buf_ref[slot, j * pages_per_blk]
        for i in range(1, pages_per_blk):
            tot = tot + buf_ref[slot, j * pages_per_blk + i]
        o_ref[pl.ds(j, 1), :] = jnp.sum(tot, axis=0, keepdims=True)


def _sample_ksum(page_table_flat, cache_k3, DB, n_pages):
    _, page, W = cache_k3.shape
    blks_per_step = PAGES_PER_STEP * page // MOBA_BLOCK
    nchunk = n_pages // PAGES_PER_STEP
    nblk = n_pages * page // MOBA_BLOCK
    return pl.pallas_call(
        _sample_ksum_kernel,
        grid_spec=pltpu.PrefetchScalarGridSpec(
            num_scalar_prefetch=1,
            grid=(DB, nchunk),
            in_specs=[pl.BlockSpec(memory_space=pl.ANY)],
            out_specs=pl.BlockSpec((None, blks_per_step, W), lambda b, ch, pt: (b, ch, 0)),
            scratch_shapes=[pltpu.VMEM((2, PAGES_PER_STEP, page, W), F32), pltpu.SemaphoreType.DMA((2,))],
        ),
        out_shape=jax.ShapeDtypeStruct((DB, nblk, W), F32),
        compiler_params=_cparams(("arbitrary", "arbitrary")),
        name="sample_ksum",
    )(page_table_flat, cache_k3)


def _sample_gate_kernel(q_ref, ksum_ref, hsel_ref, o_ref):
    DB, nblk, W = ksum_ref.shape
    n_iota = lax.broadcasted_iota(jnp.int32, (nblk, LANES), 0)
    for b in range(DB):
        prod = ksum_ref[b] * q_ref[pl.ds(b, 1), :]
        g = jnp.dot(prod, hsel_ref[...], preferred_element_type=F32, precision=HIGHEST)
        rows = []
        for _ in range(MOBA_TOPK):
            m = jnp.max(g, axis=0, keepdims=True)
            idx = jnp.min(jnp.where(g == m, n_iota, BIG_IDX), axis=0, keepdims=True)
            rows.append(idx)
            g = jnp.where(n_iota == idx, NEG_INF, g)
        o_ref[b] = jnp.concatenate(rows + [jnp.zeros((8 - MOBA_TOPK, LANES), jnp.int32)], axis=0)


def _sample_gate(q, ksum):
    DB, nblk, W = ksum.shape
    hsel = (np.arange(W)[:, None] // HEAD_DIM == np.arange(LANES)[None, :]).astype(np.float32)
    return pl.pallas_call(
        _sample_gate_kernel,
        out_shape=jax.ShapeDtypeStruct((DB, 8, LANES), jnp.int32),
        compiler_params=pltpu.CompilerParams(vmem_limit_bytes=VMEM_LIMIT),
        name="sample_gate",
    )(q, ksum, jnp.asarray(hsel))


def _attn_copies(pt_ref, top_ref, ck_ref, cv_ref, kbuf, vbuf, sem_ref, b, slot, n_pages, page):
    pages_per_blk = MOBA_BLOCK // page
    cps = []
    for h in range(N_HEADS):
        lanes = pl.ds((h // HEADS_PER_TILE) * LANES, LANES)
        for t in range(MOBA_TOPK):
            blk = top_ref[(b * N_HEADS + h) * MOBA_TOPK + t]
            for i in range(pages_per_blk):
                phys = pt_ref[b * n_pages + blk * pages_per_blk + i]
                rows = pl.ds((t * pages_per_blk + i) * page, page)
                cps.append(pltpu.make_async_copy(ck_ref.at[phys, :, lanes], kbuf.at[slot, h, rows, :],
                                                 sem_ref.at[slot]))
                cps.append(pltpu.make_async_copy(cv_ref.at[phys, :, lanes], vbuf.at[slot, h, rows, :],
                                                 sem_ref.at[slot]))
    return cps


def _sample_attn_kernel(pt_ref, top_ref, q_ref, kn_ref, vn_ref, ck_ref, cv_ref, o_ref, kbuf, vbuf, sem_ref,
                        *, n_pages, page):
    b = pl.program_id(0)
    nb = pl.num_programs(0)
    slot = b % 2
    copies = functools.partial(_attn_copies, pt_ref, top_ref, ck_ref, cv_ref, kbuf, vbuf, sem_ref,
                               n_pages=n_pages, page=page)

    @pl.when(b == 0)
    def _():
        for cp in copies(b=b, slot=slot):
            cp.start()

    @pl.when(b + 1 < nb)
    def _():
        for cp in copies(b=b + 1, slot=1 - slot):
            cp.start()

    for cp in copies(b=b, slot=slot):
        cp.wait()

    scale = HEAD_DIM ** -0.5
    lane = lax.broadcasted_iota(jnp.int32, (1, LANES), 1)
    outs = []
    for tile in range(N_HEADS // HEADS_PER_TILE):
        cols = slice(tile * LANES, (tile + 1) * LANES)
        q_t = q_ref[:, cols]
        k_new = kn_ref[:, cols]
        v_new = vn_ref[:, cols]
        out_t = jnp.zeros((1, LANES), F32)
        for hh in range(HEADS_PER_TILE):
            h = tile * HEADS_PER_TILE + hh
            in_head = (lane >= hh * HEAD_DIM) & (lane < (hh + 1) * HEAD_DIM)
            qm = jnp.where(in_head, q_t, 0.0)
            s = jnp.sum(kbuf[slot, h] * qm, axis=1, keepdims=True) * scale
            s_own = jnp.sum(k_new * qm, axis=1, keepdims=True) * scale
            m = jnp.maximum(jnp.max(s, axis=0, keepdims=True), s_own)
            p = jnp.exp(s - m)
            p_own = jnp.exp(s_own - m)
            l = jnp.sum(p, axis=0, keepdims=True) + p_own
            acc = jnp.sum(p * vbuf[slot, h], axis=0, keepdims=True) + p_own * v_new
            out_t = jnp.where(in_head, acc / l, out_t)
        outs.append(out_t)
    o_ref[...] = jnp.concatenate(outs, axis=1)


def _sample_attn(page_table_flat, top_flat, q, k_new, v_new, cache_k3, cache_v3, n_pages):
    DB, W = q.shape
    _, page, _ = cache_k3.shape
    nkeys = MOBA_TOPK * MOBA_BLOCK
    vm = lambda: pl.BlockSpec((None, 1, W), lambda b, pt, tp: (b, 0, 0))
    hbm = lambda: pl.BlockSpec(memory_space=pl.ANY)
    return pl.pallas_call(
        functools.partial(_sample_attn_kernel, n_pages=n_pages, page=page),
        grid_spec=pltpu.PrefetchScalarGridSpec(
            num_scalar_prefetch=2,
            grid=(DB,),
            in_specs=[vm(), vm(), vm(), hbm(), hbm()],
            out_specs=vm(),
            scratch_shapes=[pltpu.VMEM((2, N_HEADS, nkeys, LANES), F32),
                            pltpu.VMEM((2, N_HEADS, nkeys, LANES), F32),
                            pltpu.SemaphoreType.DMA((2,))],
        ),
        out_shape=jax.ShapeDtypeStruct((DB, 1, W), F32),
        compiler_params=_cparams(("arbitrary",)),
        name="sample_attn",
    )(page_table_flat, top_flat, q[:, None], k_new[:, None], v_new[:, None], cache_k3, cache_v3).reshape(DB, W)


def _sample_tail_kernel(x_ref, po_ref, ao_ref, w_ref, g_ref, wr_ref, wg_ref, wu_ref, wd_ref, gf_ref,
                        y_ref, h_sc, xn_sc, gate_sc, acc_sc):
    e = pl.program_id(0)

    @pl.when(e == 0)
    def _():
        mix = (jnp.dot(po_ref[...], w_ref[0:POOL_WIDTH, :], preferred_element_type=F32, precision=HIGHEST)
               + jnp.dot(ao_ref[...], w_ref[POOL_WIDTH:, :], preferred_element_type=F32, precision=HIGHEST))
        h = x_ref[...] + mix
        h_sc[...] = h
        xn = _rms(h, g_ref[...])
        xn_sc[...] = xn
        gate_sc[...] = _route(jnp.dot(xn, wr_ref[...], preferred_element_type=F32, precision=HIGHEST))
        acc_sc[...] = jnp.zeros_like(acc_sc)

    o = _swiglu(xn_sc[...].astype(BF16), wg_ref[...], wu_ref[...], wd_ref[...])
    acc_sc[...] += _gate_column(gate_sc[...], e) * o

    @pl.when(e == pl.num_programs(0) - 1)
    def _():
        y_ref[...] = _rms(h_sc[...] + acc_sc[...], gf_ref[...])


def _sample_tail(x, po, ao, w_out, g_ffn, w_router, wg_bf, wu_bf, wd_bf, g_final):
    DB, D = x.shape
    E, _, DE = wg_bf.shape
    full = lambda shape: pl.BlockSpec(shape, lambda e: (0,) * len(shape))
    return pl.pallas_call(
        _sample_tail_kernel,
        grid=(E,),
        in_specs=[full((DB, D)), full((DB, POOL_WIDTH)), full((DB, ATTN_WIDTH)), full(w_out.shape),
                  full((1, D)), full(w_router.shape),
                  pl.BlockSpec((None, D, DE), lambda e: (e, 0, 0)),
                  pl.BlockSpec((None, D, DE), lambda e: (e, 0, 0)),
                  pl.BlockSpec((None, DE, D), lambda e: (e, 0, 0)),
                  full((1, D))],
        out_specs=full((DB, D)),
        out_shape=jax.ShapeDtypeStruct((DB, D), F32),
        scratch_shapes=[pltpu.VMEM((DB, D), F32), pltpu.VMEM((DB, D), F32), pltpu.VMEM((DB, LANES), F32),
                        pltpu.VMEM((DB, D), F32)],
        compiler_params=_cparams(("arbitrary",)),
        name="sample_tail",
    )(x, po, ao, w_out, g_ffn, w_router, wg_bf, wu_bf, wd_bf, g_final)


def _tile(n, pref):
    while n % pref:
        pref //= 2
    return pref


def kernel(x_prompt, x_sample, cache_k, cache_v, state_pool, page_table, norm_mix, w_in, w_pool, pool_scale,
           w_out, norm_ffn, w_router_group, w_router_expert, w_gate, w_up, w_down, norm_final):
    B, S, D = x_prompt.shape
    DB, DS, _ = x_sample.shape
    depth, n_phys, page = cache_k.shape[:3]
    assert depth == 1 and DS == 1, "single layer, one new token per sequence"
    n_pages = page_table.shape[1]
    past_len = n_pages * page
    assert S % MOBA_BLOCK == 0 and past_len % MOBA_BLOCK == 0 and MOBA_BLOCK % page == 0
    assert n_pages % PAGES_PER_STEP == 0 and past_len // MOBA_BLOCK >= MOBA_TOPK
    T = B * S

    g_mix = norm_mix[0][None, :]
    g_ffn = norm_ffn[0][None, :]
    g_final = norm_final[None, :]
    ps = pool_scale[0][None, :]
    w_router = jnp.concatenate(
        [w_router_expert[0], w_router_group[0],
         jnp.zeros((D, LANES - N_EXPERTS - N_EXPERT_GROUPS), F32)], axis=1)
    wg_bf, wu_bf, wd_bf = w_gate[0].astype(BF16), w_up[0].astype(BF16), w_down[0].astype(BF16)

    ts = _tile(S, 512)
    k_f, v_f, q_bf, k_bf, v_bf, po, ksum, u_last = _inproj(
        x_prompt, g_mix, w_in[0].astype(BF16), w_pool[0].astype(BF16), ps, ts)
    ao = _moba_prompt(q_bf, k_bf, v_bf, ksum.reshape(B, S // MOBA_BLOCK, ATTN_WIDTH))
    h2, xn2, gate = _outproj(x_prompt.reshape(T, D), po.reshape(T, POOL_WIDTH), ao.reshape(T, ATTN_WIDTH),
                             w_out[0].astype(BF16), g_ffn, w_router.astype(BF16), _tile(T, 512))
    y_prompt = _moe(xn2, gate, h2, wg_bf, wu_bf, wd_bf, g_final, _tile(T, 1024)).reshape(B, S, D)
    k_prompt = k_f.reshape(1, B, S, N_HEADS, HEAD_DIM)
    v_prompt = v_f.reshape(1, B, S, N_HEADS, HEAD_DIM)
    pool_prompt = u_last[None, :, HALO - POOL_HIST:, :]

    xs = x_sample.reshape(DB, D)
    pos_s = jnp.full((1,), past_len, jnp.int32)
    sp_t = jnp.swapaxes(state_pool[0], 0, 1)
    u_s, q_s, k_s, v_s, po_s = _sample_inproj(xs, g_mix, w_in[0], pos_s, sp_t, w_pool[0], ps)
    cache_k3 = cache_k[0].reshape(n_phys, page, ATTN_WIDTH)
    cache_v3 = cache_v[0].reshape(n_phys, page, ATTN_WIDTH)
    pt_flat = page_table.reshape(-1).astype(jnp.int32)
    ksum_s = _sample_ksum(pt_flat, cache_k3, DB, n_pages)
    top = _sample_gate(q_s, ksum_s)
    top_flat = jnp.swapaxes(top[:, :MOBA_TOPK, :N_HEADS], 1, 2).reshape(-1)
    ao_s = _sample_attn(pt_flat, top_flat, q_s, k_s, v_s, cache_k3, cache_v3, n_pages)
    y_sample = _sample_tail(xs, po_s, ao_s, w_out[0], g_ffn, w_router, wg_bf, wu_bf, wd_bf,
                            g_final).reshape(DB, 1, D)
    k_sample = k_s.reshape(1, DB, 1, N_HEADS, HEAD_DIM)
    v_sample = v_s.reshape(1, DB, 1, N_HEADS, HEAD_DIM)
    pool_sample = jnp.concatenate([state_pool[0][:, 1:], u_s[:, None, :]], axis=1)[None]
    return (y_prompt, y_sample, k_prompt, v_prompt, pool_prompt, k_sample, v_sample, pool_sample)
```

```python
import functools

import jax
import jax.numpy as jnp
import numpy as np
from jax import lax
from jax.experimental import pallas as pl
from jax.experimental.pallas import tpu as pltpu

F32 = jnp.float32
BF16 = jnp.bfloat16
HIGHEST = lax.Precision.HIGHEST

POOL_WINDOWS = (2, 4, 8, 16)
POOL_GROUP = 128
POOL_WIDTH = POOL_GROUP * len(POOL_WINDOWS)
POOL_HIST = max(POOL_WINDOWS) - 1
HALO = POOL_HIST + 1
HEAD_DIM = 64
N_HEADS = 8
ATTN_WIDTH = N_HEADS * HEAD_DIM
ROT_DIM = HEAD_DIM // 4
ROT_HALF = ROT_DIM // 2
ROPE_THETA = 500000.0
MOBA_BLOCK = 256
MOBA_TOPK = 3
N_EXPERT_GROUPS = 4
EXPERTS_PER_GROUP = 4
N_EXPERTS = N_EXPERT_GROUPS * EXPERTS_PER_GROUP
RMS_EPS = 1e-6

LANES = 128
SUBLANES = 8
HEADS_PER_TILE = LANES // HEAD_DIM
VMEM_LIMIT = 52 * 1024 * 1024

NEG_INF = float("-inf")
NEG_BIG = -1e30
BIG_IDX = 1 << 20
NT = (((1,), (1,)), ((), ()))


def _cparams(sem):
    return pltpu.CompilerParams(dimension_semantics=sem, vmem_limit_bytes=VMEM_LIMIT)


def _rms(x, g):
    r = lax.rsqrt(jnp.mean(x * x, axis=-1, keepdims=True) + RMS_EPS)
    return x * r * g


def _rope_angles(pos):
    inv = ROPE_THETA ** (-(jnp.arange(0, ROT_DIM, 2, dtype=F32) / ROT_DIM))
    ang = pos.astype(F32)[:, None] * inv[None, :]
    return jnp.cos(ang), jnp.sin(ang)


def _rope_tables(pos):
    cos, sin = _rope_angles(pos)
    d = np.arange(LANES) % HEAD_DIM
    fi = d % ROT_HALF
    cos_l, sin_l = cos[:, fi], sin[:, fi]
    c = jnp.where(d < ROT_DIM, cos_l, 1.0)
    s_lo = jnp.where(d < ROT_HALF, -sin_l, 0.0)
    s_hi = jnp.where((d >= ROT_HALF) & (d < ROT_DIM), sin_l, 0.0)
    return c.astype(F32), s_lo.astype(F32), s_hi.astype(F32)


def _rotate(x, c, s_lo, s_hi):
    outs = []
    for j in range(x.shape[1] // LANES):
        t = x[:, j * LANES:(j + 1) * LANES]
        up = pltpu.roll(t, LANES - ROT_HALF, axis=1)
        dn = pltpu.roll(t, ROT_HALF, axis=1)
        outs.append(t * c + up * s_lo + dn * s_hi)
    return jnp.concatenate(outs, axis=1)


def _rotate_t(xt, cos_t, sin_t):
    pieces = []
    for h in range(N_HEADS):
        r0 = h * HEAD_DIM
        x1 = xt[r0:r0 + ROT_HALF]
        x2 = xt[r0 + ROT_HALF:r0 + ROT_DIM]
        pieces += [x1 * cos_t - x2 * sin_t, x2 * cos_t + x1 * sin_t, xt[r0 + ROT_DIM:r0 + HEAD_DIM]]
    return jnp.concatenate(pieces, axis=0)


def _inproj_kernel(x_ref, g_ref, wuq_ref, wkv_ref, c_ref, slo_ref, shi_ref, ct_ref, st_ref, ind_ref,
                   wp_ref, ps_ref,
                   kf_ref, vf_ref, qb_ref, kb_ref, vb_ref, po_ref, ksum_ref, ulast_ref, ext_ref):
    s = pl.program_id(1)
    ts = x_ref.shape[0]
    xn = _rms(x_ref[...], g_ref[...]).astype(BF16)
    u = jnp.dot(xn, wuq_ref[:, 0:POOL_WIDTH], preferred_element_type=F32)
    q = jnp.dot(xn, wuq_ref[:, POOL_WIDTH:], preferred_element_type=F32)
    kt = lax.dot_general(wkv_ref[0:ATTN_WIDTH, :], xn, NT, preferred_element_type=F32)
    vt = lax.dot_general(wkv_ref[ATTN_WIDTH:, :], xn, NT, preferred_element_type=F32)
    q = _rotate(q, c_ref[...], slo_ref[...], shi_ref[...])
    kt = _rotate_t(kt, ct_ref[...], st_ref[...])
    kf_ref[...] = kt
    vf_ref[...] = vt
    qb_ref[...] = (q * (HEAD_DIM ** -0.5)).astype(BF16)
    kb_ref[...] = kt.astype(BF16)
    vb_ref[...] = vt.astype(BF16)
    ksum_ref[...] = lax.dot_general(ind_ref[...], kt, NT, preferred_element_type=F32, precision=HIGHEST)

    @pl.when(s == 0)
    def _():
        ext_ref[0:HALO, :] = jnp.zeros((HALO, POOL_WIDTH), F32)

    ext_ref[HALO:HALO + ts, :] = u
    pos1 = s * ts + lax.broadcasted_iota(jnp.int32, (ts, POOL_GROUP), 0) + 1
    outs = []
    for g, w in enumerate(POOL_WINDOWS):
        sl = slice(g * POOL_GROUP, (g + 1) * POOL_GROUP)
        ug = u[:, sl]
        acc = ug
        for j in range(1, w):
            acc = acc + ext_ref[HALO - j:HALO - j + ts, sl]
        cnt = jnp.minimum(pos1, w).astype(F32)
        d = (acc / cnt - ug).astype(BF16)
        outs.append(jnp.dot(d, wp_ref[g], preferred_element_type=F32))
    y = jnp.concatenate(outs, axis=1) * ps_ref[...]
    po_ref[...] = y.astype(BF16)
    tail = u[ts - HALO:ts, :]
    ext_ref[0:HALO, :] = tail

    @pl.when(s == pl.num_programs(1) - 1)
    def _():
        ulast_ref[...] = tail


def _inproj(x, g_mix, w_in, w_pool_bf, pool_scale, ts):
    B, S, D = x.shape
    ns = S // ts
    pos = jnp.arange(S, dtype=jnp.int32)
    c, s_lo, s_hi = _rope_tables(pos)
    cos, sin = _rope_angles(pos)
    cos_t, sin_t = cos.T, sin.T
    w_uq = w_in[:, :POOL_WIDTH + ATTN_WIDTH].astype(BF16)
    w_kv_t = w_in[:, POOL_WIDTH + ATTN_WIDTH:].T.astype(BF16)
    ind = (np.arange(ts)[None, :] // MOBA_BLOCK == np.arange(SUBLANES)[:, None]).astype(np.float32)
    row = lambda b, s: (b, s, 0)
    col = lambda b, s: (b, 0, s)
    tab = pl.BlockSpec((ts, LANES), lambda b, s: (s, 0))
    tab_t = pl.BlockSpec((ROT_HALF, ts), lambda b, s: (0, s))
    full2 = lambda shape: pl.BlockSpec(shape, lambda b, s: (0, 0))
    act = lambda: pl.BlockSpec((None, ts, ATTN_WIDTH), row)
    act_t = lambda: pl.BlockSpec((None, ATTN_WIDTH, ts), col)
    out_shapes = (
        jax.ShapeDtypeStruct((B, ATTN_WIDTH, S), F32),
        jax.ShapeDtypeStruct((B, ATTN_WIDTH, S), F32),
        jax.ShapeDtypeStruct((B, S, ATTN_WIDTH), BF16),
        jax.ShapeDtypeStruct((B, ATTN_WIDTH, S), BF16),
        jax.ShapeDtypeStruct((B, ATTN_WIDTH, S), BF16),
        jax.ShapeDtypeStruct((B, S, POOL_WIDTH), BF16),
        jax.ShapeDtypeStruct((B, ns, SUBLANES, ATTN_WIDTH), F32),
        jax.ShapeDtypeStruct((B, HALO, POOL_WIDTH), F32),
    )
    return pl.pallas_call(
        _inproj_kernel,
        grid=(B, ns),
        in_specs=[
            pl.BlockSpec((None, ts, D), row),
            full2((1, D)),
            full2(w_uq.shape),
            full2(w_kv_t.shape),
            tab, tab, tab, tab_t, tab_t,
            full2(ind.shape),
            pl.BlockSpec(w_pool_bf.shape, lambda b, s: (0, 0, 0)),
            full2((1, POOL_WIDTH)),
        ],
        out_specs=(act_t(), act_t(), act(), act_t(), act_t(), act(),
                   pl.BlockSpec((None, None, SUBLANES, ATTN_WIDTH), lambda b, s: (b, s, 0, 0)),
                   pl.BlockSpec((None, HALO, POOL_WIDTH), lambda b, s: (b, 0, 0))),
        out_shape=out_shapes,
        scratch_shapes=[pltpu.VMEM((HALO + ts, POOL_WIDTH), F32)],
        compiler_params=_cparams(("arbitrary", "arbitrary")),
        name="prompt_inproj",
    )(x, g_mix, w_uq, w_kv_t, c, s_lo, s_hi, cos_t, sin_t, jnp.asarray(ind), w_pool_bf, pool_scale)


def _moba_kernel(q_ref, k_ref, v_ref, e_ref, ksum_ref, o_ref, s_sc, *, chunk):
    c = pl.program_id(2)
    tq = q_ref.shape[0]
    nb = ksum_ref.shape[0]
    blocks_per_chunk = chunk // MOBA_BLOCK
    n_tiles = chunk // LANES
    q = q_ref[...]
    kmean = (ksum_ref[...] * (1.0 / MOBA_BLOCK)).astype(BF16)
    lane = lax.broadcasted_iota(jnp.int32, (tq, LANES), 1)
    blk_iota = lax.broadcasted_iota(jnp.int32, (nb, tq), 0)

    qp = []
    for h in range(HEADS_PER_TILE):
        in_head = (lane >= h * HEAD_DIM) & (lane < (h + 1) * HEAD_DIM)
        qm = jnp.where(in_head, q, jnp.zeros_like(q))
        g = lax.dot_general(kmean, qm, NT, preferred_element_type=F32)
        g = jnp.where(blk_iota < c, g, NEG_INF)
        sel = blk_iota == c
        for t in range(MOBA_TOPK):
            m = jnp.max(g, axis=0, keepdims=True)
            idx = jnp.min(jnp.where(g == m, blk_iota, BIG_IDX), axis=0, keepdims=True)
            hit = blk_iota == idx
            sel = sel | (hit & (t < c))
            g = jnp.where(hit, NEG_INF, g)
        bias_t = jnp.where(sel, 0.0, NEG_BIG)
        bias_t = jnp.concatenate([bias_t, jnp.zeros((LANES - nb, tq), F32)], axis=0)
        bias = jnp.transpose(bias_t).astype(BF16)
        qp.append(jnp.concatenate([qm, bias], axis=1))

    def keys(off):
        return jnp.concatenate([k_ref[:, pl.ds(off, chunk)], e_ref[:, pl.ds(off, chunk)]], axis=0)

    def fold(fn, acc, x):
        for j in range(n_tiles):
            acc = fn(acc, x[:, j * LANES:(j + 1) * LANES])
        return acc

    n_past = lax.shift_right_logical(c, int(np.log2(blocks_per_chunk)))

    def pass1(i, mrun):
        off = pl.multiple_of(i * chunk, chunk)
        kk = keys(off)
        new = []
        for h in range(HEADS_PER_TILE):
            s = jnp.dot(qp[h], kk, preferred_element_type=F32)
            s_sc[h, :, pl.ds(off, chunk)] = s
            new.append(fold(jnp.maximum, mrun[h], s))
        return tuple(new)

    mrun = lax.fori_loop(0, n_past, pass1, tuple(jnp.full((tq, LANES), NEG_BIG, F32)
                                                 for _ in range(HEADS_PER_TILE)))
    off = pl.multiple_of(n_past * chunk, chunk)
    kk = keys(off)
    kpos = off + lax.broadcasted_iota(jnp.int32, (tq, chunk), 1)
    qpos = c * MOBA_BLOCK + lax.broadcasted_iota(jnp.int32, (tq, chunk), 0)
    m_b = []
    for h in range(HEADS_PER_TILE):
        s = jnp.dot(qp[h], kk, preferred_element_type=F32)
        s = jnp.where(kpos <= qpos, s, NEG_BIG)
        s_sc[h, :, pl.ds(off, chunk)] = s
        m = jnp.max(fold(jnp.maximum, mrun[h], s), axis=1, keepdims=True)
        m_b.append(jnp.broadcast_to(m, (tq, LANES)))

    def pass2(i, carry):
        off = pl.multiple_of(i * chunk, chunk)
        vv = v_ref[:, pl.ds(off, chunk)]
        new = []
        for h in range(HEADS_PER_TILE):
            l, acc = carry[2 * h:2 * h + 2]
            s = s_sc[h, :, pl.ds(off, chunk)]
            p = [jnp.exp(s[:, j * LANES:(j + 1) * LANES] - m_b[h]) for j in range(n_tiles)]
            for pj in p:
                l = l + pj
            pb = jnp.concatenate([pj.astype(BF16) for pj in p], axis=1)
            acc = acc + lax.dot_general(pb, vv, NT, preferred_element_type=F32)
            new += [l, acc]
        return tuple(new)

    zeros = jnp.zeros((tq, LANES), F32)
    carry = lax.fori_loop(0, n_past + 1, pass2, (zeros,) * (2 * HEADS_PER_TILE))
    out = zeros
    for h in range(HEADS_PER_TILE):
        l, acc = carry[2 * h:2 * h + 2]
        in_head = (lane >= h * HEAD_DIM) & (lane < (h + 1) * HEAD_DIM)
        out = jnp.where(in_head, acc / jnp.sum(l, axis=1, keepdims=True), out)
    o_ref[...] = out.astype(o_ref.dtype)


def _moba_prompt(q_bf, kt_bf, vt_bf, ksum):
    B, S, W = q_bf.shape
    nb = S // MOBA_BLOCK
    assert nb <= LANES
    ntile = W // LANES
    chunk = min(S, 4 * MOBA_BLOCK)
    ind = (np.arange(S)[None, :] // MOBA_BLOCK == np.arange(LANES)[:, None]).astype(np.float32)
    blk = pl.BlockSpec((None, MOBA_BLOCK, LANES), lambda b, t, c: (b, c, t))
    seq = pl.BlockSpec((None, LANES, S), lambda b, t, c: (b, t, 0))
    return pl.pallas_call(
        functools.partial(_moba_kernel, chunk=chunk),
        grid=(B, ntile, nb),
        in_specs=[blk, seq, seq, pl.BlockSpec((LANES, S), lambda b, t, c: (0, 0)),
                  pl.BlockSpec((None, nb, LANES), lambda b, t, c: (b, 0, t))],
        out_specs=blk,
        out_shape=jax.ShapeDtypeStruct((B, S, W), BF16),
        scratch_shapes=[pltpu.VMEM((HEADS_PER_TILE, MOBA_BLOCK, S), F32)],
        compiler_params=_cparams(("arbitrary", "arbitrary", "arbitrary")),
        name="prompt_moba",
    )(q_bf, kt_bf, vt_bf, jnp.asarray(ind, BF16), ksum)


def _route(logits):
    lane = lax.broadcasted_iota(jnp.int32, logits.shape, 1)
    is_grp = (lane >= N_EXPERTS) & (lane < N_EXPERTS + N_EXPERT_GROUPS)
    gl = jnp.where(is_grp, logits, NEG_INF)
    ge = jnp.exp(gl - jnp.max(gl, axis=1, keepdims=True))
    pg = ge / jnp.sum(ge, axis=1, keepdims=True)
    pg_sel = jnp.max(pg, axis=1, keepdims=True)
    g_sel = jnp.min(jnp.where(is_grp & (pg == pg_sel), lane - N_EXPERTS, BIG_IDX), axis=1, keepdims=True)
    in_grp = (lane < N_EXPERTS) & (jnp.right_shift(lane, 2) == g_sel)
    el = jnp.where(in_grp, logits, NEG_INF)
    ee = jnp.exp(el - jnp.max(el, axis=1, keepdims=True))
    pe = jnp.where(in_grp, ee / jnp.sum(ee, axis=1, keepdims=True), -1.0)
    p1 = jnp.max(pe, axis=1, keepdims=True)
    i1 = jnp.min(jnp.where(pe == p1, lane, BIG_IDX), axis=1, keepdims=True)
    pe2 = jnp.where(lane == i1, -1.0, pe)
    p2 = jnp.max(pe2, axis=1, keepdims=True)
    i2 = jnp.min(jnp.where(pe2 == p2, lane, BIG_IDX), axis=1, keepdims=True)
    tot = p1 + p2
    return (jnp.where(lane == i1, p1 / tot * pg_sel, 0.0)
            + jnp.where(lane == i2, p2 / tot * pg_sel, 0.0))


def _gate_column(gate, e):
    lane = lax.broadcasted_iota(jnp.int32, gate.shape, 1)
    return jnp.sum(jnp.where(lane == e, gate, 0.0), axis=1, keepdims=True)


def _swiglu(x_bf, wg, wu, wd):
    hg = jnp.dot(x_bf, wg, preferred_element_type=F32)
    hu = jnp.dot(x_bf, wu, preferred_element_type=F32)
    h = hg * (1.0 / (1.0 + jnp.exp(-hg))) * hu
    return jnp.dot(h.astype(BF16), wd, preferred_element_type=F32)


def _outproj_kernel(x_ref, po_ref, ao_ref, w_ref, g_ref, wr_ref, h_ref, xn_ref, gate_ref):
    mix = (jnp.dot(po_ref[...], w_ref[0:POOL_WIDTH, :], preferred_element_type=F32)
           + jnp.dot(ao_ref[...], w_ref[POOL_WIDTH:, :], preferred_element_type=F32))
    h = x_ref[...] + mix
    h_ref[...] = h
    xn = _rms(h, g_ref[...]).astype(BF16)
    xn_ref[...] = xn
    gate_ref[...] = _route(jnp.dot(xn, wr_ref[...], preferred_element_type=F32))


def _outproj(x2d, po2d, ao2d, w_out_bf, g_ffn, w_router_bf, tm):
    T, D = x2d.shape
    row = lambda i: (i, 0)
    full = lambda shape: pl.BlockSpec(shape, lambda i: (0, 0))
    return pl.pallas_call(
        _outproj_kernel,
        grid=(T // tm,),
        in_specs=[pl.BlockSpec((tm, D), row), pl.BlockSpec((tm, POOL_WIDTH), row),
                  pl.BlockSpec((tm, ATTN_WIDTH), row), full(w_out_bf.shape), full((1, D)),
                  full(w_router_bf.shape)],
        out_specs=(pl.BlockSpec((tm, D), row), pl.BlockSpec((tm, D), row), pl.BlockSpec((tm, LANES), row)),
        out_shape=(jax.ShapeDtypeStruct((T, D), F32), jax.ShapeDtypeStruct((T, D), BF16),
                   jax.ShapeDtypeStruct((T, LANES), F32)),
        compiler_params=_cparams(("arbitrary",)),
        name="prompt_outproj",
    )(x2d, po2d, ao2d, w_out_bf, g_ffn, w_router_bf)


def _moe_kernel(xn_ref, gate_ref, h_ref, wg_ref, wu_ref, wd_ref, gf_ref, y_ref, acc_ref):
    e = pl.program_id(1)
    o = _swiglu(xn_ref[...], wg_ref[...], wu_ref[...], wd_ref[...])
    contrib = _gate_column(gate_ref[...], e) * o

    @pl.when(e == 0)
    def _():
        acc_ref[...] = contrib

    @pl.when(e > 0)
    def _():
        acc_ref[...] += contrib

    @pl.when(e == pl.num_programs(1) - 1)
    def _():
        y_ref[...] = _rms(h_ref[...] + acc_ref[...], gf_ref[...])


def _moe(xn2d, gate, h2d, wg_bf, wu_bf, wd_bf, g_final, tm):
    T, D = h2d.shape
    E, _, DE = wg_bf.shape
    row = lambda i, e: (i, 0)
    return pl.pallas_call(
        _moe_kernel,
        grid=(T // tm, E),
        in_specs=[pl.BlockSpec((tm, D), row), pl.BlockSpec((tm, LANES), row), pl.BlockSpec((tm, D), row),
                  pl.BlockSpec((None, D, DE), lambda i, e: (e, 0, 0)),
                  pl.BlockSpec((None, D, DE), lambda i, e: (e, 0, 0)),
                  pl.BlockSpec((None, DE, D), lambda i, e: (e, 0, 0)),
                  pl.BlockSpec((1, D), lambda i, e: (0, 0))],
        out_specs=pl.BlockSpec((tm, D), row),
        out_shape=jax.ShapeDtypeStruct((T, D), F32),
        scratch_shapes=[pltpu.VMEM((tm, D), F32)],
        compiler_params=_cparams(("arbitrary", "arbitrary")),
        name="prompt_moe",
    )(xn2d, gate, h2d, wg_bf, wu_bf, wd_bf, g_final)


def _sample_inproj_kernel(x_ref, g_ref, w_ref, c_ref, slo_ref, shi_ref, sp_ref, wp_ref, ps_ref,
                          u_ref, k_ref, v_ref, po_ref, qt_ref, kt_ref, vt_ref):
    xn = _rms(x_ref[...], g_ref[...])

    def proj(i):
        return jnp.dot(xn, w_ref[:, i * POOL_WIDTH:(i + 1) * POOL_WIDTH],
                       preferred_element_type=F32, precision=HIGHEST)

    u, q, k, v = proj(0), proj(1), proj(2), proj(3)
    c, s_lo, s_hi = c_ref[...], slo_ref[...], shi_ref[...]
    q = _rotate(q, c, s_lo, s_hi)
    k = _rotate(k, c, s_lo, s_hi)
    u_ref[...] = u
    k_ref[...] = k
    v_ref[...] = v
    qt_ref[...] = jnp.transpose(q)
    kt_ref[...] = jnp.transpose(k)
    vt_ref[...] = jnp.transpose(v)
    db = sp_ref.shape[1]
    outs = []
    for g, w in enumerate(POOL_WINDOWS):
        sl = slice(g * POOL_GROUP, (g + 1) * POOL_GROUP)
        ug = u[0:db, sl]
        acc = ug
        for j in range(1, w):
            acc = acc + sp_ref[POOL_HIST - j, :, sl]
        d = acc / float(w) - ug
        outs.append(jnp.dot(d, wp_ref[g], preferred_element_type=F32, precision=HIGHEST))
    po_ref[...] = jnp.zeros_like(po_ref)
    po_ref[0:db, :] = jnp.concatenate(outs, axis=1) * ps_ref[...]


def _sample_inproj(x_pad, g_mix, w_in, pos, sp_t, w_pool, pool_scale):
    R, D = x_pad.shape
    c, s_lo, s_hi = _rope_tables(pos)
    row = jax.ShapeDtypeStruct((R, POOL_WIDTH), F32)
    col = jax.ShapeDtypeStruct((POOL_WIDTH, R), F32)
    return pl.pallas_call(
        _sample_inproj_kernel,
        out_shape=(row, row, row, row, col, col, col),
        compiler_params=pltpu.CompilerParams(vmem_limit_bytes=VMEM_LIMIT),
        name="sample_inproj",
    )(x_pad, g_mix, w_in, c, s_lo, s_hi, sp_t, w_pool, pool_scale)


def _column(mat, b):
    lane = lax.broadcasted_iota(jnp.int32, mat.shape, 1)
    col = jnp.sum(jnp.where(lane == b, mat, 0.0), axis=1, keepdims=True)
    return jnp.broadcast_to(col, mat.shape)


PAGES_PER_STEP = 16


def _scan_copies(pt_ref, cache_ref, buf_ref, sem_ref, step, slot):
    return [pltpu.make_async_copy(cache_ref.at[pt_ref[step * PAGES_PER_STEP + i]],
                                  buf_ref.at[slot, i], sem_ref.at[slot])
            for i in range(PAGES_PER_STEP)]


def _sample_scan_kernel(pt_ref, qt_ref, cache_ref, top_ref, buf_ref, sem_ref, qb_sc, g_sc, *, nblk):
    b, ch = pl.program_id(0), pl.program_id(1)
    nch = pl.num_programs(1)
    step = b * nch + ch
    nsteps = pl.num_programs(0) * nch
    slot = step % 2

    @pl.when(step == 0)
    def _():
        for cp in _scan_copies(pt_ref, cache_ref, buf_ref, sem_ref, step, slot):
            cp.start()

    @pl.when(step + 1 < nsteps)
    def _():
        for cp in _scan_copies(pt_ref, cache_ref, buf_ref, sem_ref, step + 1, 1 - slot):
            cp.start()

    @pl.when(ch == 0)
    def _():
        qb_sc[...] = _column(qt_ref[...], b)
        g_sc[...] = jnp.zeros_like(g_sc)

    for cp in _scan_copies(pt_ref, cache_ref, buf_ref, sem_ref, step, slot):
        cp.wait()
    qb = qb_sc[...]
    lane = lax.broadcasted_iota(jnp.int32, (N_HEADS, LANES), 1)
    pages_per_blk = MOBA_BLOCK // buf_ref.shape[3]
    blks_per_step = PAGES_PER_STEP // pages_per_blk
    g = g_sc[...]
    for j in range(blks_per_step):
        tot = buf_ref[slot, j * pages_per_blk]
        for i in range(1, pages_per_blk):
            tot = tot + buf_ref[slot, j * pages_per_blk + i]
        pr = tot * qb
        per_head = jnp.concatenate(
            [jnp.sum(pr[h * HEAD_DIM:(h + 1) * HEAD_DIM], axis=0, keepdims=True) for h in range(N_HEADS)],
            axis=0)
        gcol = jnp.sum(per_head, axis=1, keepdims=True)
        g = jnp.where(lane == ch * blks_per_step + j, gcol, g)
    g_sc[...] = g

    @pl.when(ch == nch - 1)
    def _():
        gg = jnp.where(lane < nblk, g, NEG_INF)
        top = jnp.zeros((N_HEADS, LANES), jnp.int32)
        for t in range(MOBA_TOPK):
            m = jnp.max(gg, axis=1, keepdims=True)
            idx = jnp.min(jnp.where(gg == m, lane, BIG_IDX), axis=1, keepdims=True)
            top = jnp.where(lane == t, idx, top)
            gg = jnp.where(lane == idx, NEG_INF, gg)
        top_ref[...] = top


def _sample_scan(page_table_flat, q_t, cache_kt, DB, n_pages):
    _, W, page = cache_kt.shape
    nchunk = n_pages // PAGES_PER_STEP
    nblk = n_pages * page // MOBA_BLOCK
    assert nblk <= LANES
    return pl.pallas_call(
        functools.partial(_sample_scan_kernel, nblk=nblk),
        grid_spec=pltpu.PrefetchScalarGridSpec(
            num_scalar_prefetch=1,
            grid=(DB, nchunk),
            in_specs=[pl.BlockSpec(q_t.shape, lambda b, ch, pt: (0, 0)), pl.BlockSpec(memory_space=pl.ANY)],
            out_specs=pl.BlockSpec((None, N_HEADS, LANES), lambda b, ch, pt: (b, 0, 0)),
            scratch_shapes=[pltpu.VMEM((2, PAGES_PER_STEP, W, page), F32), pltpu.SemaphoreType.DMA((2,)),
                            pltpu.VMEM(q_t.shape, F32), pltpu.VMEM((N_HEADS, LANES), F32)],
        ),
        out_shape=jax.ShapeDtypeStruct((DB, N_HEADS, LANES), jnp.int32),
        compiler_params=_cparams(("arbitrary", "arbitrary")),
        name="sample_scan",
    )(page_table_flat, q_t, cache_kt)


def _attn_copies(pt_ref, top_ref, ck_ref, cv_ref, kbuf, vbuf, sem_ref, b, slot, n_pages, pages_per_blk):
    cps = []
    for h in range(N_HEADS):
        rows = pl.ds(h * HEAD_DIM, HEAD_DIM)
        for t in range(MOBA_TOPK):
            blk = top_ref[(b * N_HEADS + h) * MOBA_TOPK + t]
            for i in range(pages_per_blk):
                phys = pt_ref[b * n_pages + blk * pages_per_blk + i]
                j = t * pages_per_blk + i
                cps.append(pltpu.make_async_copy(ck_ref.at[phys, rows, :], kbuf.at[slot, h, j], sem_ref.at[slot]))
                cps.append(pltpu.make_async_copy(cv_ref.at[phys, rows, :], vbuf.at[slot, h, j], sem_ref.at[slot]))
    return cps


def _sample_attn_kernel(pt_ref, top_ref, qt_ref, kt_ref, vt_ref, ck_ref, cv_ref, o_ref, kbuf, vbuf, sem_ref,
                        *, n_pages):
    b = pl.program_id(0)
    nb = pl.num_programs(0)
    slot = b % 2
    n_sel = kbuf.shape[2]
    copies = functools.partial(_attn_copies, pt_ref, top_ref, ck_ref, cv_ref, kbuf, vbuf, sem_ref,
                               n_pages=n_pages, pages_per_blk=n_sel // MOBA_TOPK)

    @pl.when(b == 0)
    def _():
        o_ref[...] = jnp.zeros_like(o_ref)
        for cp in copies(b=b, slot=slot):
            cp.start()

    @pl.when(b + 1 < nb)
    def _():
        for cp in copies(b=b + 1, slot=1 - slot):
            cp.start()

    for cp in copies(b=b, slot=slot):
        cp.wait()

    scale = HEAD_DIM ** -0.5
    lane = lax.broadcasted_iota(jnp.int32, (HEAD_DIM, LANES), 1)
    for h in range(N_HEADS):
        rows = slice(h * HEAD_DIM, (h + 1) * HEAD_DIM)
        qb = _column(qt_ref[rows, :], b)
        kb = _column(kt_ref[rows, :], b)
        vb = _column(vt_ref[rows, :], b)
        s = [jnp.sum(kbuf[slot, h, j] * qb, axis=0, keepdims=True) * scale for j in range(n_sel)]
        s_own = jnp.sum(kb * qb, axis=0, keepdims=True) * scale
        m = s_own
        for sj in s:
            m = jnp.maximum(m, jnp.max(sj, axis=1, keepdims=True))
        p = [jnp.exp(sj - m) for sj in s]
        p_own = jnp.exp(s_own - m)
        l = p_own
        acc = jnp.zeros((HEAD_DIM, LANES), F32)
        for j, pj in enumerate(p):
            l = l + jnp.sum(pj, axis=1, keepdims=True)
            acc = acc + vbuf[slot, h, j] * pj
        col = (jnp.sum(acc, axis=1, keepdims=True) + p_own * vb) / l
        o_ref[rows, :] = jnp.where(lane == b, col, o_ref[rows, :])


def _sample_attn(page_table_flat, top_flat, q_t, k_t, v_t, cache_kt, cache_vt, DB, n_pages):
    _, W, page = cache_kt.shape
    n_sel = MOBA_TOPK * MOBA_BLOCK // page
    vm = lambda: pl.BlockSpec(q_t.shape, lambda b, pt, tp: (0, 0))
    hbm = lambda: pl.BlockSpec(memory_space=pl.ANY)
    return pl.pallas_call(
        functools.partial(_sample_attn_kernel, n_pages=n_pages),
        grid_spec=pltpu.PrefetchScalarGridSpec(
            num_scalar_prefetch=2,
            grid=(DB,),
            in_specs=[vm(), vm(), vm(), hbm(), hbm()],
            out_specs=vm(),
            scratch_shapes=[pltpu.VMEM((2, N_HEADS, n_sel, HEAD_DIM, page), F32),
                            pltpu.VMEM((2, N_HEADS, n_sel, HEAD_DIM, page), F32),
                            pltpu.SemaphoreType.DMA((2,))],
        ),
        out_shape=jax.ShapeDtypeStruct(q_t.shape, F32),
        compiler_params=_cparams(("arbitrary",)),
        name="sample_attn",
    )(page_table_flat, top_flat, q_t, k_t, v_t, cache_kt, cache_vt)


def _sample_tail_kernel(x_ref, po_ref, aot_ref, w_ref, g_ref, wr_ref, wg_ref, wu_ref, wd_ref, gf_ref,
                        y_ref, h_sc, xn_sc, gate_sc, acc_sc):
    e = pl.program_id(0)

    @pl.when(e == 0)
    def _():
        ao = jnp.transpose(aot_ref[...])
        mix = (jnp.dot(po_ref[...], w_ref[0:POOL_WIDTH, :], preferred_element_type=F32, precision=HIGHEST)
               + jnp.dot(ao, w_ref[POOL_WIDTH:, :], preferred_element_type=F32, precision=HIGHEST))
        h = x_ref[...] + mix
        h_sc[...] = h
        xn = _rms(h, g_ref[...])
        xn_sc[...] = xn
        gate_sc[...] = _route(jnp.dot(xn, wr_ref[...], preferred_element_type=F32, precision=HIGHEST))
        acc_sc[...] = jnp.zeros_like(acc_sc)

    o = _swiglu(xn_sc[...].astype(BF16), wg_ref[...], wu_ref[...], wd_ref[...])
    acc_sc[...] += _gate_column(gate_sc[...], e) * o

    @pl.when(e == pl.num_programs(0) - 1)
    def _():
        y_ref[...] = _rms(h_sc[...] + acc_sc[...], gf_ref[...])


def _sample_tail(x_pad, po, ao_t, w_out, g_ffn, w_router, wg_bf, wu_bf, wd_bf, g_final):
    R, D = x_pad.shape
    E, _, DE = wg_bf.shape
    full = lambda shape: pl.BlockSpec(shape, lambda e: (0,) * len(shape))
    return pl.pallas_call(
        _sample_tail_kernel,
        grid=(E,),
        in_specs=[full((R, D)), full((R, POOL_WIDTH)), full((ATTN_WIDTH, R)), full(w_out.shape),
                  full((1, D)), full(w_router.shape),
                  pl.BlockSpec((None, D, DE), lambda e: (e, 0, 0)),
                  pl.BlockSpec((None, D, DE), lambda e: (e, 0, 0)),
                  pl.BlockSpec((None, DE, D), lambda e: (e, 0, 0)),
                  full((1, D))],
        out_specs=full((R, D)),
        out_shape=jax.ShapeDtypeStruct((R, D), F32),
        scratch_shapes=[pltpu.VMEM((R, D), F32), pltpu.VMEM((R, D), F32), pltpu.VMEM((R, LANES), F32),
                        pltpu.VMEM((R, D), F32)],
        compiler_params=_cparams(("arbitrary",)),
        name="sample_tail",
    )(x_pad, po, ao_t, w_out, g_ffn, w_router, wg_bf, wu_bf, wd_bf, g_final)


def _tile(n, pref):
    while n % pref:
        pref //= 2
    return pref


def kernel(x_prompt, x_sample, cache_k, cache_v, state_pool, page_table, norm_mix, w_in, w_pool, pool_scale,
           w_out, norm_ffn, w_router_group, w_router_expert, w_gate, w_up, w_down, norm_final):
    B, S, D = x_prompt.shape
    DB, DS, _ = x_sample.shape
    depth, n_phys, page = cache_k.shape[:3]
    assert depth == 1 and DS == 1, "single layer, one new token per sequence"
    n_pages = page_table.shape[1]
    past_len = n_pages * page
    assert S % MOBA_BLOCK == 0 and past_len % MOBA_BLOCK == 0 and MOBA_BLOCK % page == 0
    assert n_pages % PAGES_PER_STEP == 0 and past_len // MOBA_BLOCK >= MOBA_TOPK and DB <= LANES
    T = B * S

    g_mix = norm_mix[0][None, :]
    g_ffn = norm_ffn[0][None, :]
    g_final = norm_final[None, :]
    ps = pool_scale[0][None, :]
    w_router = jnp.concatenate(
        [w_router_expert[0], w_router_group[0],
         jnp.zeros((D, LANES - N_EXPERTS - N_EXPERT_GROUPS), F32)], axis=1)
    wg_bf, wu_bf, wd_bf = w_gate[0].astype(BF16), w_up[0].astype(BF16), w_down[0].astype(BF16)

    ts = _tile(S, 512)
    kt_f, vt_f, q_bf, kt_bf, vt_bf, po, ksum, u_last = _inproj(
        x_prompt, g_mix, w_in[0], w_pool[0].astype(BF16), ps, ts)
    ksum = ksum[:, :, :ts // MOBA_BLOCK].reshape(B, S // MOBA_BLOCK, ATTN_WIDTH)
    ao = _moba_prompt(q_bf, kt_bf, vt_bf, ksum)
    h2, xn2, gate = _outproj(x_prompt.reshape(T, D), po.reshape(T, POOL_WIDTH), ao.reshape(T, ATTN_WIDTH),
                             w_out[0].astype(BF16), g_ffn, w_router.astype(BF16), _tile(T, 512))
    y_prompt = _moe(xn2, gate, h2, wg_bf, wu_bf, wd_bf, g_final, _tile(T, 1024)).reshape(B, S, D)
    k_prompt = jnp.transpose(kt_f.reshape(B, N_HEADS, HEAD_DIM, S), (0, 3, 1, 2))[None]
    v_prompt = jnp.transpose(vt_f.reshape(B, N_HEADS, HEAD_DIM, S), (0, 3, 1, 2))[None]
    pool_prompt = u_last[None, :, HALO - POOL_HIST:, :]

    xs = x_sample.reshape(DB, D)
    x_pad = jnp.pad(xs, ((0, LANES - DB), (0, 0)))
    pos_s = jnp.full((1,), past_len, jnp.int32)
    sp_t = jnp.swapaxes(state_pool[0], 0, 1)
    u_s, k_s, v_s, po_s, q_t, k_t, v_t = _sample_inproj(x_pad, g_mix, w_in[0], pos_s, sp_t, w_pool[0], ps)
    cache_kt = jnp.transpose(cache_k[0], (0, 2, 3, 1)).reshape(n_phys, ATTN_WIDTH, page)
    cache_vt = jnp.transpose(cache_v[0], (0, 2, 3, 1)).reshape(n_phys, ATTN_WIDTH, page)
    pt_flat = page_table.reshape(-1).astype(jnp.int32)
    top = _sample_scan(pt_flat, q_t, cache_kt, DB, n_pages)
    top_flat = top[:, :, :MOBA_TOPK].reshape(-1)
    ao_t = _sample_attn(pt_flat, top_flat, q_t, k_t, v_t, cache_kt, cache_vt, DB, n_pages)
    y_pad = _sample_tail(x_pad, po_s, ao_t, w_out[0], g_ffn, w_router, wg_bf, wu_bf, wd_bf, g_final)
    y_sample = y_pad[:DB].reshape(DB, 1, D)
    k_sample = k_s[:DB].reshape(1, DB, 1, N_HEADS, HEAD_DIM)
    v_sample = v_s[:DB].reshape(1, DB, 1, N_HEADS, HEAD_DIM)
    pool_sample = jnp.concatenate([state_pool[0][:, 1:], u_s[:DB, None, :]], axis=1)[None]
    return (y_prompt, y_sample, k_prompt, v_prompt, pool_prompt, k_sample, v_sample, pool_sample)
```

```python
import functools

import jax
import jax.numpy as jnp
import numpy as np
from jax import lax
from jax.experimental import pallas as pl
from jax.experimental.pallas import tpu as pltpu

F32 = jnp.float32
BF16 = jnp.bfloat16
HIGHEST = lax.Precision.HIGHEST

POOL_WINDOWS = (2, 4, 8, 16)
POOL_GROUP = 128
POOL_WIDTH = POOL_GROUP * len(POOL_WINDOWS)
POOL_HIST = max(POOL_WINDOWS) - 1
HALO = POOL_HIST + 1
HEAD_DIM = 64
N_HEADS = 8
ATTN_WIDTH = N_HEADS * HEAD_DIM
ROT_DIM = HEAD_DIM // 4
ROT_HALF = ROT_DIM // 2
ROPE_THETA = 500000.0
MOBA_BLOCK = 256
MOBA_TOPK = 3
N_EXPERT_GROUPS = 4
EXPERTS_PER_GROUP = 4
N_EXPERTS = N_EXPERT_GROUPS * EXPERTS_PER_GROUP
RMS_EPS = 1e-6

LANES = 128
SUBLANES = 8
HEADS_PER_TILE = LANES // HEAD_DIM
VMEM_LIMIT = 52 * 1024 * 1024

NEG_INF = float("-inf")
NEG_BIG = -1e30
BIG_IDX = 1 << 20
NT = (((1,), (1,)), ((), ()))


def _cparams(sem):
    return pltpu.CompilerParams(dimension_semantics=sem, vmem_limit_bytes=VMEM_LIMIT)


def _rms(x, g):
    r = lax.rsqrt(jnp.mean(x * x, axis=-1, keepdims=True) + RMS_EPS)
    return x * r * g


def _rope_angles(pos):
    inv = ROPE_THETA ** (-(jnp.arange(0, ROT_DIM, 2, dtype=F32) / ROT_DIM))
    ang = pos.astype(F32)[:, None] * inv[None, :]
    return jnp.cos(ang), jnp.sin(ang)


def _rope_tables(pos):
    cos, sin = _rope_angles(pos)
    d = np.arange(LANES) % HEAD_DIM
    fi = d % ROT_HALF
    cos_l, sin_l = cos[:, fi], sin[:, fi]
    c = jnp.where(d < ROT_DIM, cos_l, 1.0)
    s_lo = jnp.where(d < ROT_HALF, -sin_l, 0.0)
    s_hi = jnp.where((d >= ROT_HALF) & (d < ROT_DIM), sin_l, 0.0)
    return c.astype(F32), s_lo.astype(F32), s_hi.astype(F32)


def _rotate(x, c, s_lo, s_hi):
    outs = []
    for j in range(x.shape[1] // LANES):
        t = x[:, j * LANES:(j + 1) * LANES]
        up = pltpu.roll(t, LANES - ROT_HALF, axis=1)
        dn = pltpu.roll(t, ROT_HALF, axis=1)
        outs.append(t * c + up * s_lo + dn * s_hi)
    return jnp.concatenate(outs, axis=1)


def _rotate_t(xt, cos_t, sin_t):
    pieces = []
    for h in range(N_HEADS):
        r0 = h * HEAD_DIM
        x1 = xt[r0:r0 + ROT_HALF]
        x2 = xt[r0 + ROT_HALF:r0 + ROT_DIM]
        pieces += [x1 * cos_t - x2 * sin_t, x2 * cos_t + x1 * sin_t, xt[r0 + ROT_DIM:r0 + HEAD_DIM]]
    return jnp.concatenate(pieces, axis=0)


def _inproj_kernel(x_ref, g_ref, wuq_ref, wkv_ref, c_ref, slo_ref, shi_ref, ct_ref, st_ref, ind_ref,
                   wp_ref, ps_ref,
                   kf_ref, vf_ref, qb_ref, kb_ref, vb_ref, po_ref, ksum_ref, ulast_ref, ext_ref):
    s = pl.program_id(1)
    ts = x_ref.shape[0]
    xn = _rms(x_ref[...], g_ref[...]).astype(BF16)
    u = jnp.dot(xn, wuq_ref[:, 0:POOL_WIDTH], preferred_element_type=F32)
    q = jnp.dot(xn, wuq_ref[:, POOL_WIDTH:], preferred_element_type=F32)
    kt = lax.dot_general(wkv_ref[0:ATTN_WIDTH, :], xn, NT, preferred_element_type=F32)
    vt = lax.dot_general(wkv_ref[ATTN_WIDTH:, :], xn, NT, preferred_element_type=F32)
    q = _rotate(q, c_ref[...], slo_ref[...], shi_ref[...])
    kt = _rotate_t(kt, ct_ref[...], st_ref[...])
    kf_ref[...] = kt
    vf_ref[...] = vt
    qb_ref[...] = (q * (HEAD_DIM ** -0.5)).astype(BF16)
    kb_ref[...] = kt.astype(BF16)
    vb_ref[...] = vt.astype(BF16)
    ksum_ref[...] = lax.dot_general(ind_ref[...], kt, NT, preferred_element_type=F32, precision=HIGHEST)

    @pl.when(s == 0)
    def _():
        ext_ref[0:HALO, :] = jnp.zeros((HALO, POOL_WIDTH), F32)

    ext_ref[HALO:HALO + ts, :] = u
    pos1 = s * ts + lax.broadcasted_iota(jnp.int32, (ts, POOL_GROUP), 0) + 1
    outs = []
    for g, w in enumerate(POOL_WINDOWS):
        sl = slice(g * POOL_GROUP, (g + 1) * POOL_GROUP)
        ug = u[:, sl]
        acc = ug
        for j in range(1, w):
            acc = acc + ext_ref[HALO - j:HALO - j + ts, sl]
        cnt = jnp.minimum(pos1, w).astype(F32)
        d = (acc / cnt - ug).astype(BF16)
        outs.append(jnp.dot(d, wp_ref[g], preferred_element_type=F32))
    y = jnp.concatenate(outs, axis=1) * ps_ref[...]
    po_ref[...] = y.astype(BF16)
    tail = u[ts - HALO:ts, :]
    ext_ref[0:HALO, :] = tail

    @pl.when(s == pl.num_programs(1) - 1)
    def _():
        ulast_ref[...] = tail


def _inproj(x, g_mix, w_in, w_pool_bf, pool_scale, ts):
    B, S, D = x.shape
    ns = S // ts
    pos = jnp.arange(S, dtype=jnp.int32)
    c, s_lo, s_hi = _rope_tables(pos)
    cos, sin = _rope_angles(pos)
    cos_t, sin_t = cos.T, sin.T
    w_uq = w_in[:, :POOL_WIDTH + ATTN_WIDTH].astype(BF16)
    w_kv_t = w_in[:, POOL_WIDTH + ATTN_WIDTH:].T.astype(BF16)
    ind = (np.arange(ts)[None, :] // MOBA_BLOCK == np.arange(SUBLANES)[:, None]).astype(np.float32)
    row = lambda b, s: (b, s, 0)
    col = lambda b, s: (b, 0, s)
    tab = pl.BlockSpec((ts, LANES), lambda b, s: (s, 0))
    tab_t = pl.BlockSpec((ROT_HALF, ts), lambda b, s: (0, s))
    full2 = lambda shape: pl.BlockSpec(shape, lambda b, s: (0, 0))
    act = lambda: pl.BlockSpec((None, ts, ATTN_WIDTH), row)
    act_t = lambda: pl.BlockSpec((None, ATTN_WIDTH, ts), col)
    out_shapes = (
        jax.ShapeDtypeStruct((B, ATTN_WIDTH, S), F32),
        jax.ShapeDtypeStruct((B, ATTN_WIDTH, S), F32),
        jax.ShapeDtypeStruct((B, S, ATTN_WIDTH), BF16),
        jax.ShapeDtypeStruct((B, ATTN_WIDTH, S), BF16),
        jax.ShapeDtypeStruct((B, ATTN_WIDTH, S), BF16),
        jax.ShapeDtypeStruct((B, S, POOL_WIDTH), BF16),
        jax.ShapeDtypeStruct((B, ns, SUBLANES, ATTN_WIDTH), F32),
        jax.ShapeDtypeStruct((B, HALO, POOL_WIDTH), F32),
    )
    return pl.pallas_call(
        _inproj_kernel,
        grid=(B, ns),
        in_specs=[
            pl.BlockSpec((None, ts, D), row),
            full2((1, D)),
            full2(w_uq.shape),
            full2(w_kv_t.shape),
            tab, tab, tab, tab_t, tab_t,
            full2(ind.shape),
            pl.BlockSpec(w_pool_bf.shape, lambda b, s: (0, 0, 0)),
            full2((1, POOL_WIDTH)),
        ],
        out_specs=(act_t(), act_t(), act(), act_t(), act_t(), act(),
                   pl.BlockSpec((None, None, SUBLANES, ATTN_WIDTH), lambda b, s: (b, s, 0, 0)),
                   pl.BlockSpec((None, HALO, POOL_WIDTH), lambda b, s: (b, 0, 0))),
        out_shape=out_shapes,
        scratch_shapes=[pltpu.VMEM((HALO + ts, POOL_WIDTH), F32)],
        compiler_params=_cparams(("arbitrary", "arbitrary")),
        name="prompt_inproj",
    )(x, g_mix, w_uq, w_kv_t, c, s_lo, s_hi, cos_t, sin_t, jnp.asarray(ind), w_pool_bf, pool_scale)


def _moba_kernel(q_ref, k_ref, v_ref, e_ref, ksum_ref, o_ref, s_sc, *, chunk):
    c = pl.program_id(2)
    tq = q_ref.shape[0]
    nb = ksum_ref.shape[0]
    blocks_per_chunk = chunk // MOBA_BLOCK
    n_tiles = chunk // LANES
    q = q_ref[...]
    kmean = (ksum_ref[...] * (1.0 / MOBA_BLOCK)).astype(BF16)
    lane = lax.broadcasted_iota(jnp.int32, (tq, LANES), 1)
    blk_iota = lax.broadcasted_iota(jnp.int32, (nb, tq), 0)

    qp = []
    for h in range(HEADS_PER_TILE):
        in_head = (lane >= h * HEAD_DIM) & (lane < (h + 1) * HEAD_DIM)
        qm = jnp.where(in_head, q, jnp.zeros_like(q))
        g = lax.dot_general(kmean, qm, NT, preferred_element_type=F32)
        g = jnp.where(blk_iota < c, g, NEG_INF)
        sel = blk_iota == c
        for t in range(MOBA_TOPK):
            m = jnp.max(g, axis=0, keepdims=True)
            idx = jnp.min(jnp.where(g == m, blk_iota, BIG_IDX), axis=0, keepdims=True)
            hit = blk_iota == idx
            sel = sel | (hit & (t < c))
            g = jnp.where(hit, NEG_INF, g)
        bias_t = jnp.where(sel, 0.0, NEG_BIG)
        bias_t = jnp.concatenate([bias_t, jnp.zeros((LANES - nb, tq), F32)], axis=0)
        bias = jnp.transpose(bias_t).astype(BF16)
        qp.append(jnp.concatenate([qm, bias], axis=1))

    def keys(off):
        return jnp.concatenate([k_ref[:, pl.ds(off, chunk)], e_ref[:, pl.ds(off, chunk)]], axis=0)

    def fold(fn, acc, x):
        for j in range(n_tiles):
            acc = fn(acc, x[:, j * LANES:(j + 1) * LANES])
        return acc

    n_past = lax.shift_right_logical(c, int(np.log2(blocks_per_chunk)))

    def pass1(i, mrun):
        off = pl.multiple_of(i * chunk, chunk)
        kk = keys(off)
        new = []
        for h in range(HEADS_PER_TILE):
            s = jnp.dot(qp[h], kk, preferred_element_type=F32)
            s_sc[h, :, pl.ds(off, chunk)] = s
            new.append(fold(jnp.maximum, mrun[h], s))
        return tuple(new)

    mrun = lax.fori_loop(0, n_past, pass1, tuple(jnp.full((tq, LANES), NEG_BIG, F32)
                                                 for _ in range(HEADS_PER_TILE)))
    off = pl.multiple_of(n_past * chunk, chunk)
    kk = keys(off)
    kpos = off + lax.broadcasted_iota(jnp.int32, (tq, chunk), 1)
    qpos = c * MOBA_BLOCK + lax.broadcasted_iota(jnp.int32, (tq, chunk), 0)
    m_b = []
    for h in range(HEADS_PER_TILE):
        s = jnp.dot(qp[h], kk, preferred_element_type=F32)
        s = jnp.where(kpos <= qpos, s, NEG_BIG)
        s_sc[h, :, pl.ds(off, chunk)] = s
        m = jnp.max(fold(jnp.maximum, mrun[h], s), axis=1, keepdims=True)
        m_b.append(jnp.broadcast_to(m, (tq, LANES)))

    def pass2(i, carry):
        off = pl.multiple_of(i * chunk, chunk)
        vv = v_ref[:, pl.ds(off, chunk)]
        new = []
        for h in range(HEADS_PER_TILE):
            l, acc = carry[2 * h:2 * h + 2]
            s = s_sc[h, :, pl.ds(off, chunk)]
            p = [jnp.exp(s[:, j * LANES:(j + 1) * LANES] - m_b[h]) for j in range(n_tiles)]
            for pj in p:
                l = l + pj
            pb = jnp.concatenate([pj.astype(BF16) for pj in p], axis=1)
            acc = acc + lax.dot_general(pb, vv, NT, preferred_element_type=F32)
            new += [l, acc]
        return tuple(new)

    zeros = jnp.zeros((tq, LANES), F32)
    carry = lax.fori_loop(0, n_past + 1, pass2, (zeros,) * (2 * HEADS_PER_TILE))
    out = zeros
    for h in range(HEADS_PER_TILE):
        l, acc = carry[2 * h:2 * h + 2]
        in_head = (lane >= h * HEAD_DIM) & (lane < (h + 1) * HEAD_DIM)
        out = jnp.where(in_head, acc / jnp.sum(l, axis=1, keepdims=True), out)
    o_ref[...] = out.astype(o_ref.dtype)


def _moba_prompt(q_bf, kt_bf, vt_bf, ksum):
    B, S, W = q_bf.shape
    nb = S // MOBA_BLOCK
    assert nb <= LANES
    ntile = W // LANES
    chunk = min(S, 4 * MOBA_BLOCK)
    ind = (np.arange(S)[None, :] // MOBA_BLOCK == np.arange(LANES)[:, None]).astype(np.float32)
    blk = pl.BlockSpec((None, MOBA_BLOCK, LANES), lambda b, t, c: (b, c, t))
    seq = pl.BlockSpec((None, LANES, S), lambda b, t, c: (b, t, 0))
    return pl.pallas_call(
        functools.partial(_moba_kernel, chunk=chunk),
        grid=(B, ntile, nb),
        in_specs=[blk, seq, seq, pl.BlockSpec((LANES, S), lambda b, t, c: (0, 0)),
                  pl.BlockSpec((None, nb, LANES), lambda b, t, c: (b, 0, t))],
        out_specs=blk,
        out_shape=jax.ShapeDtypeStruct((B, S, W), BF16),
        scratch_shapes=[pltpu.VMEM((HEADS_PER_TILE, MOBA_BLOCK, S), F32)],
        compiler_params=_cparams(("arbitrary", "arbitrary", "arbitrary")),
        name="prompt_moba",
    )(q_bf, kt_bf, vt_bf, jnp.asarray(ind, BF16), ksum)


def _route(logits):
    lane = lax.broadcasted_iota(jnp.int32, logits.shape, 1)
    is_grp = (lane >= N_EXPERTS) & (lane < N_EXPERTS + N_EXPERT_GROUPS)
    gl = jnp.where(is_grp, logits, NEG_INF)
    ge = jnp.exp(gl - jnp.max(gl, axis=1, keepdims=True))
    pg = ge / jnp.sum(ge, axis=1, keepdims=True)
    pg_sel = jnp.max(pg, axis=1, keepdims=True)
    g_sel = jnp.min(jnp.where(is_grp & (pg == pg_sel), lane - N_EXPERTS, BIG_IDX), axis=1, keepdims=True)
    in_grp = (lane < N_EXPERTS) & (jnp.right_shift(lane, 2) == g_sel)
    el = jnp.where(in_grp, logits, NEG_INF)
    ee = jnp.exp(el - jnp.max(el, axis=1, keepdims=True))
    pe = jnp.where(in_grp, ee / jnp.sum(ee, axis=1, keepdims=True), -1.0)
    p1 = jnp.max(pe, axis=1, keepdims=True)
    i1 = jnp.min(jnp.where(pe == p1, lane, BIG_IDX), axis=1, keepdims=True)
    pe2 = jnp.where(lane == i1, -1.0, pe)
    p2 = jnp.max(pe2, axis=1, keepdims=True)
    i2 = jnp.min(jnp.where(pe2 == p2, lane, BIG_IDX), axis=1, keepdims=True)
    tot = p1 + p2
    gate = (jnp.where(lane == i1, p1 / tot * pg_sel, 0.0)
            + jnp.where(lane == i2, p2 / tot * pg_sel, 0.0))
    return gate, g_sel


def _gate_column(gate, e):
    lane = lax.broadcasted_iota(jnp.int32, gate.shape, 1)
    return jnp.sum(jnp.where(lane == e, gate, 0.0), axis=1, keepdims=True)


def _swiglu(x_bf, wg, wu, wd):
    hg = jnp.dot(x_bf, wg, preferred_element_type=F32)
    hu = jnp.dot(x_bf, wu, preferred_element_type=F32)
    h = hg * (1.0 / (1.0 + jnp.exp(-hg))) * hu
    return jnp.dot(h.astype(BF16), wd, preferred_element_type=F32)


GROUP_LANE = N_EXPERTS + N_EXPERT_GROUPS
RANK_LANE = GROUP_LANE + 1


def _outproj_kernel(x_ref, po_ref, ao_ref, w_ref, g_ref, wr_ref, tri_ref, h_ref, xg_ref, cnt_ref):
    i = pl.program_id(0)
    d = x_ref.shape[1]
    mix = (jnp.dot(po_ref[...], w_ref[0:POOL_WIDTH, :], preferred_element_type=F32)
           + jnp.dot(ao_ref[...], w_ref[POOL_WIDTH:, :], preferred_element_type=F32))
    h = x_ref[...] + mix
    h_ref[...] = h
    xn = _rms(h, g_ref[...])
    gate, g_sel = _route(jnp.dot(xn.astype(BF16), wr_ref[...], preferred_element_type=F32))

    @pl.when(i == 0)
    def _():
        cnt_ref[...] = jnp.zeros_like(cnt_ref)

    lane = lax.broadcasted_iota(jnp.int32, gate.shape, 1)
    onehot = jnp.where(lane == g_sel, 1.0, 0.0)
    before = jnp.dot(tri_ref[...], onehot.astype(BF16), preferred_element_type=F32) + cnt_ref[0:1, :]
    rank = jnp.sum(onehot * before, axis=1, keepdims=True)
    cnt_ref[...] = cnt_ref[...] + jnp.sum(onehot, axis=0, keepdims=True)
    info = jnp.where(lane == GROUP_LANE, g_sel.astype(F32), jnp.where(lane == RANK_LANE, rank, gate))
    xg_ref[:, 0:d] = xn
    xg_ref[:, d:] = info


def _outproj(x2d, po2d, ao2d, w_out_bf, g_ffn, w_router_bf, tm):
    T, D = x2d.shape
    row = lambda i: (i, 0)
    full = lambda shape: pl.BlockSpec(shape, lambda i: (0, 0))
    tri = np.tril(np.ones((tm, tm), np.float32), -1)
    return pl.pallas_call(
        _outproj_kernel,
        grid=(T // tm,),
        in_specs=[pl.BlockSpec((tm, D), row), pl.BlockSpec((tm, POOL_WIDTH), row),
                  pl.BlockSpec((tm, ATTN_WIDTH), row), full(w_out_bf.shape), full((1, D)),
                  full(w_router_bf.shape), full((tm, tm))],
        out_specs=(pl.BlockSpec((tm, D), row), pl.BlockSpec((tm, D + LANES), row), full((SUBLANES, LANES))),
        out_shape=(jax.ShapeDtypeStruct((T, D), F32), jax.ShapeDtypeStruct((T, D + LANES), F32),
                   jax.ShapeDtypeStruct((SUBLANES, LANES), F32)),
        compiler_params=_cparams(("arbitrary",)),
        name="prompt_outproj",
    )(x2d, po2d, ao2d, w_out_bf, g_ffn, w_router_bf, jnp.asarray(tri, BF16))


MOE_TILE = 512
ROW_TILE = 512
DMA_UNROLL = 8


def _dispatch_kernel(dest_ref, xg_ref, zeros_ref, xs_ref, sem_ref):
    del zeros_ref
    n = xg_ref.shape[0]

    def start(r, carry):
        pltpu.make_async_copy(xg_ref.at[pl.ds(r, 1)], xs_ref.at[pl.ds(dest_ref[0, r], 1)], sem_ref.at[0]).start()
        return carry

    lax.fori_loop(0, n, start, 0, unroll=DMA_UNROLL)
    pltpu.make_async_copy(xg_ref, xs_ref.at[pl.ds(0, n)], sem_ref.at[0]).wait()


def _dispatch(dest3d, xg, n_rows):
    T, W = xg.shape
    tm = dest3d.shape[2]
    return pl.pallas_call(
        _dispatch_kernel,
        grid=(T // tm,),
        in_specs=[pl.BlockSpec((None, 1, tm), lambda i: (i, 0, 0), memory_space=pltpu.SMEM),
                  pl.BlockSpec((tm, W), lambda i: (i, 0)),
                  pl.BlockSpec(memory_space=pl.ANY)],
        out_specs=pl.BlockSpec(memory_space=pl.ANY),
        out_shape=jax.ShapeDtypeStruct((n_rows, W), F32),
        scratch_shapes=[pltpu.SemaphoreType.DMA((1,))],
        input_output_aliases={2: 0},
        compiler_params=_cparams(("arbitrary",)),
        name="moe_dispatch",
    )(dest3d, xg, jnp.zeros((n_rows, W), F32))


def _moe_kernel(grp_ref, nused_ref, xs_ref, wg_ref, wu_ref, wd_ref, o_ref):
    i = pl.program_id(0)
    d = o_ref.shape[1]

    @pl.when(i < nused_ref[0])
    def _():
        x = xs_ref[:, 0:d].astype(BF16)
        gate = xs_ref[:, d:]
        first = grp_ref[i] * EXPERTS_PER_GROUP
        acc = jnp.zeros(o_ref.shape, F32)
        for j in range(EXPERTS_PER_GROUP):
            acc = acc + _gate_column(gate, first + j) * _swiglu(x, wg_ref[j], wu_ref[j], wd_ref[j])
        o_ref[...] = acc

    @pl.when(i >= nused_ref[0])
    def _():
        o_ref[...] = jnp.zeros_like(o_ref)


def _moe(tile_group, n_used, xs, wg_bf, wu_bf, wd_bf):
    n_rows, W = xs.shape
    E, D, DE = wg_bf.shape
    G = E // EXPERTS_PER_GROUP
    grouped = lambda w: w.reshape((G, EXPERTS_PER_GROUP) + w.shape[1:])
    tile = lambda i, grp, nu: (jnp.minimum(i, nu[0] - 1), 0)
    wspec = lambda shape: pl.BlockSpec((None, EXPERTS_PER_GROUP) + shape, lambda i, grp, nu: (grp[i], 0, 0, 0))
    return pl.pallas_call(
        _moe_kernel,
        grid_spec=pltpu.PrefetchScalarGridSpec(
            num_scalar_prefetch=2,
            grid=(n_rows // MOE_TILE,),
            in_specs=[pl.BlockSpec((MOE_TILE, W), tile), wspec((D, DE)), wspec((D, DE)), wspec((DE, D))],
            out_specs=pl.BlockSpec((MOE_TILE, D), lambda i, grp, nu: (i, 0)),
        ),
        out_shape=jax.ShapeDtypeStruct((n_rows, D), F32),
        compiler_params=_cparams(("arbitrary",)),
        name="moe_experts",
    )(tile_group, n_used, xs, grouped(wg_bf), grouped(wu_bf), grouped(wd_bf))


def _combine_copies_start(dest_ref, src_ref, buf_ref, sem_ref, slot):
    def start(r, carry):
        pltpu.make_async_copy(src_ref.at[pl.ds(dest_ref[0, r], 1)], buf_ref.at[slot, pl.ds(r, 1)],
                              sem_ref.at[slot]).start()
        return carry

    lax.fori_loop(0, buf_ref.shape[1], start, 0, unroll=DMA_UNROLL)


def _combine_kernel(dest_ref, dest_next_ref, h_ref, gf_ref, moe_ref, y_ref, buf_ref, sem_ref):
    i = pl.program_id(0)
    n = buf_ref.shape[1]
    slot = i % 2

    @pl.when(i == 0)
    def _():
        _combine_copies_start(dest_ref, moe_ref, buf_ref, sem_ref, slot)

    @pl.when(i + 1 < pl.num_programs(0))
    def _():
        _combine_copies_start(dest_next_ref, moe_ref, buf_ref, sem_ref, 1 - slot)

    pltpu.make_async_copy(moe_ref.at[pl.ds(0, n)], buf_ref.at[slot], sem_ref.at[slot]).wait()
    y_ref[...] = _rms(h_ref[...] + buf_ref[slot], gf_ref[...])


def _combine(dest3d, h2d, g_final, moe_sorted):
    T, D = h2d.shape
    nt, _, tm = dest3d.shape
    return pl.pallas_call(
        _combine_kernel,
        grid=(nt,),
        in_specs=[pl.BlockSpec((None, 1, tm), lambda i: (i, 0, 0), memory_space=pltpu.SMEM),
                  pl.BlockSpec((None, 1, tm), lambda i: (jnp.minimum(i + 1, nt - 1), 0, 0),
                               memory_space=pltpu.SMEM),
                  pl.BlockSpec((tm, D), lambda i: (i, 0)),
                  pl.BlockSpec((1, D), lambda i: (0, 0)),
                  pl.BlockSpec(memory_space=pl.ANY)],
        out_specs=pl.BlockSpec((tm, D), lambda i: (i, 0)),
        out_shape=jax.ShapeDtypeStruct((T, D), F32),
        scratch_shapes=[pltpu.VMEM((2, tm, D), F32), pltpu.SemaphoreType.DMA((2,))],
        compiler_params=_cparams(("arbitrary",)),
        name="moe_combine",
    )(dest3d, dest3d, h2d, g_final, moe_sorted)


def _moe_layout(xg, d, counts):
    grp = xg[:, d + GROUP_LANE].astype(jnp.int32)
    rank = xg[:, d + RANK_LANE].astype(jnp.int32)
    cnt = counts[0, :N_EXPERT_GROUPS].astype(jnp.int32)
    tiles = (cnt + MOE_TILE - 1) // MOE_TILE
    tile_end = jnp.cumsum(tiles)
    row_start = (tile_end - tiles) * MOE_TILE
    dest = row_start[grp] + rank
    max_tiles = xg.shape[0] // MOE_TILE + N_EXPERT_GROUPS
    tile_group = jnp.minimum(jnp.searchsorted(tile_end, jnp.arange(max_tiles, dtype=jnp.int32), side="right"),
                             N_EXPERT_GROUPS - 1).astype(jnp.int32)
    return dest, tile_group, tile_end[-1:].astype(jnp.int32), max_tiles * MOE_TILE


def _sample_inproj_kernel(x_ref, g_ref, w_ref, c_ref, slo_ref, shi_ref, sp_ref, wp_ref, ps_ref,
                          u_ref, k_ref, v_ref, po_ref, qt_ref, kt_ref, vt_ref):
    xn = _rms(x_ref[...], g_ref[...])

    def proj(i):
        return jnp.dot(xn, w_ref[:, i * POOL_WIDTH:(i + 1) * POOL_WIDTH],
                       preferred_element_type=F32, precision=HIGHEST)

    u, q, k, v = proj(0), proj(1), proj(2), proj(3)
    c, s_lo, s_hi = c_ref[...], slo_ref[...], shi_ref[...]
    q = _rotate(q, c, s_lo, s_hi)
    k = _rotate(k, c, s_lo, s_hi)
    u_ref[...] = u
    k_ref[...] = k
    v_ref[...] = v
    qt_ref[...] = jnp.transpose(q)
    kt_ref[...] = jnp.transpose(k)
    vt_ref[...] = jnp.transpose(v)
    db = sp_ref.shape[1]
    outs = []
    for g, w in enumerate(POOL_WINDOWS):
        sl = slice(g * POOL_GROUP, (g + 1) * POOL_GROUP)
        ug = u[0:db, sl]
        acc = ug
        for j in range(1, w):
            acc = acc + sp_ref[POOL_HIST - j, :, sl]
        d = acc / float(w) - ug
        outs.append(jnp.dot(d, wp_ref[g], preferred_element_type=F32, precision=HIGHEST))
    po_ref[...] = jnp.zeros_like(po_ref)
    po_ref[0:db, :] = jnp.concatenate(outs, axis=1) * ps_ref[...]


def _sample_inproj(x_pad, g_mix, w_in, pos, sp_t, w_pool, pool_scale):
    R, D = x_pad.shape
    c, s_lo, s_hi = _rope_tables(pos)
    row = jax.ShapeDtypeStruct((R, POOL_WIDTH), F32)
    col = jax.ShapeDtypeStruct((POOL_WIDTH, R), F32)
    return pl.pallas_call(
        _sample_inproj_kernel,
        out_shape=(row, row, row, row, col, col, col),
        compiler_params=pltpu.CompilerParams(vmem_limit_bytes=VMEM_LIMIT),
        name="sample_inproj",
    )(x_pad, g_mix, w_in, c, s_lo, s_hi, sp_t, w_pool, pool_scale)


def _column(mat, b):
    lane = lax.broadcasted_iota(jnp.int32, mat.shape, 1)
    col = jnp.sum(jnp.where(lane == b, mat, 0.0), axis=1, keepdims=True)
    return jnp.broadcast_to(col, mat.shape)


PAGES_PER_STEP = 16


def _scan_copies(pt_ref, cache_ref, buf_ref, sem_ref, step, slot):
    return [pltpu.make_async_copy(cache_ref.at[pt_ref[step * PAGES_PER_STEP + i]],
                                  buf_ref.at[slot, i], sem_ref.at[slot])
            for i in range(PAGES_PER_STEP)]


def _sample_scan_kernel(pt_ref, qt_ref, cache_ref, top_ref, buf_ref, sem_ref, qb_sc, g_sc, *, nblk):
    b, ch = pl.program_id(0), pl.program_id(1)
    nch = pl.num_programs(1)
    step = b * nch + ch
    nsteps = pl.num_programs(0) * nch
    slot = step % 2

    @pl.when(step == 0)
    def _():
        for cp in _scan_copies(pt_ref, cache_ref, buf_ref, sem_ref, step, slot):
            cp.start()

    @pl.when(step + 1 < nsteps)
    def _():
        for cp in _scan_copies(pt_ref, cache_ref, buf_ref, sem_ref, step + 1, 1 - slot):
            cp.start()

    @pl.when(ch == 0)
    def _():
        qb_sc[...] = _column(qt_ref[...], b)
        g_sc[...] = jnp.zeros_like(g_sc)

    for cp in _scan_copies(pt_ref, cache_ref, buf_ref, sem_ref, step, slot):
        cp.wait()
    qb = qb_sc[...]
    lane = lax.broadcasted_iota(jnp.int32, (N_HEADS, LANES), 1)
    pages_per_blk = MOBA_BLOCK // buf_ref.shape[3]
    blks_per_step = PAGES_PER_STEP // pages_per_blk
    g = g_sc[...]
    for j in range(blks_per_step):
        tot = buf_ref[slot, j * pages_per_blk]
        for i in range(1, pages_per_blk):
            tot = tot + buf_ref[slot, j * pages_per_blk + i]
        pr = tot * qb
        per_head = jnp.concatenate(
            [jnp.sum(pr[h * HEAD_DIM:(h + 1) * HEAD_DIM], axis=0, keepdims=True) for h in range(N_HEADS)],
            axis=0)
        gcol = jnp.sum(per_head, axis=1, keepdims=True)
        g = jnp.where(lane == ch * blks_per_step + j, gcol, g)
    g_sc[...] = g

    @pl.when(ch == nch - 1)
    def _():
        gg = jnp.where(lane < nblk, g, NEG_INF)
        top = jnp.zeros((N_HEADS, LANES), jnp.int32)
        for t in range(MOBA_TOPK):
            m = jnp.max(gg, axis=1, keepdims=True)
            idx = jnp.min(jnp.where(gg == m, lane, BIG_IDX), axis=1, keepdims=True)
            top = jnp.where(lane == t, idx, top)
            gg = jnp.where(lane == idx, NEG_INF, gg)
        top_ref[...] = top


def _sample_scan(page_table_flat, q_t, cache_kt, DB, n_pages):
    _, W, page = cache_kt.shape
    nchunk = n_pages // PAGES_PER_STEP
    nblk = n_pages * page // MOBA_BLOCK
    assert nblk <= LANES
    return pl.pallas_call(
        functools.partial(_sample_scan_kernel, nblk=nblk),
        grid_spec=pltpu.PrefetchScalarGridSpec(
            num_scalar_prefetch=1,
            grid=(DB, nchunk),
            in_specs=[pl.BlockSpec(q_t.shape, lambda b, ch, pt: (0, 0)), pl.BlockSpec(memory_space=pl.ANY)],
            out_specs=pl.BlockSpec((None, N_HEADS, LANES), lambda b, ch, pt: (b, 0, 0)),
            scratch_shapes=[pltpu.VMEM((2, PAGES_PER_STEP, W, page), F32), pltpu.SemaphoreType.DMA((2,)),
                            pltpu.VMEM(q_t.shape, F32), pltpu.VMEM((N_HEADS, LANES), F32)],
        ),
        out_shape=jax.ShapeDtypeStruct((DB, N_HEADS, LANES), jnp.int32),
        compiler_params=_cparams(("arbitrary", "arbitrary")),
        name="sample_scan",
    )(page_table_flat, q_t, cache_kt)


def _attn_copies(pt_ref, top_ref, ck_ref, cv_ref, kbuf, vbuf, sem_ref, b, slot, n_pages, pages_per_blk):
    cps = []
    for h in range(N_HEADS):
        rows = pl.ds(h * HEAD_DIM, HEAD_DIM)
        for t in range(MOBA_TOPK):
            blk = top_ref[(b * N_HEADS + h) * MOBA_TOPK + t]
            for i in range(pages_per_blk):
                phys = pt_ref[b * n_pages + blk * pages_per_blk + i]
                j = t * pages_per_blk + i
                cps.append(pltpu.make_async_copy(ck_ref.at[phys, rows, :], kbuf.at[slot, h, j], sem_ref.at[slot]))
                cps.append(pltpu.make_async_copy(cv_ref.at[phys, rows, :], vbuf.at[slot, h, j], sem_ref.at[slot]))
    return cps


def _sample_attn_kernel(pt_ref, top_ref, qt_ref, kt_ref, vt_ref, ck_ref, cv_ref, o_ref, kbuf, vbuf, sem_ref,
                        *, n_pages):
    b = pl.program_id(0)
    nb = pl.num_programs(0)
    slot = b % 2
    n_sel = kbuf.shape[2]
    copies = functools.partial(_attn_copies, pt_ref, top_ref, ck_ref, cv_ref, kbuf, vbuf, sem_ref,
                               n_pages=n_pages, pages_per_blk=n_sel // MOBA_TOPK)

    @pl.when(b == 0)
    def _():
        o_ref[...] = jnp.zeros_like(o_ref)
        for cp in copies(b=b, slot=slot):
            cp.start()

    @pl.when(b + 1 < nb)
    def _():
        for cp in copies(b=b + 1, slot=1 - slot):
            cp.start()

    for cp in copies(b=b, slot=slot):
        cp.wait()

    scale = HEAD_DIM ** -0.5
    lane = lax.broadcasted_iota(jnp.int32, (HEAD_DIM, LANES), 1)
    for h in range(N_HEADS):
        rows = slice(h * HEAD_DIM, (h + 1) * HEAD_DIM)
        qb = _column(qt_ref[rows, :], b)
        kb = _column(kt_ref[rows, :], b)
        vb = _column(vt_ref[rows, :], b)
        s = [jnp.sum(kbuf[slot, h, j] * qb, axis=0, keepdims=True) * scale for j in range(n_sel)]
        s_own = jnp.sum(kb * qb, axis=0, keepdims=True) * scale
        m = s_own
        for sj in s:
            m = jnp.maximum(m, jnp.max(sj, axis=1, keepdims=True))
        p = [jnp.exp(sj - m) for sj in s]
        p_own = jnp.exp(s_own - m)
        l = p_own
        acc = jnp.zeros((HEAD_DIM, LANES), F32)
        for j, pj in enumerate(p):
            l = l + jnp.sum(pj, axis=1, keepdims=True)
            acc = acc + vbuf[slot, h, j] * pj
        col = (jnp.sum(acc, axis=1, keepdims=True) + p_own * vb) / l
        o_ref[rows, :] = jnp.where(lane == b, col, o_ref[rows, :])


def _sample_attn(page_table_flat, top_flat, q_t, k_t, v_t, cache_kt, cache_vt, DB, n_pages):
    _, W, page = cache_kt.shape
    n_sel = MOBA_TOPK * MOBA_BLOCK // page
    vm = lambda: pl.BlockSpec(q_t.shape, lambda b, pt, tp: (0, 0))
    hbm = lambda: pl.BlockSpec(memory_space=pl.ANY)
    return pl.pallas_call(
        functools.partial(_sample_attn_kernel, n_pages=n_pages),
        grid_spec=pltpu.PrefetchScalarGridSpec(
            num_scalar_prefetch=2,
            grid=(DB,),
            in_specs=[vm(), vm(), vm(), hbm(), hbm()],
            out_specs=vm(),
            scratch_shapes=[pltpu.VMEM((2, N_HEADS, n_sel, HEAD_DIM, page), F32),
                            pltpu.VMEM((2, N_HEADS, n_sel, HEAD_DIM, page), F32),
                            pltpu.SemaphoreType.DMA((2,))],
        ),
        out_shape=jax.ShapeDtypeStruct(q_t.shape, F32),
        compiler_params=_cparams(("arbitrary",)),
        name="sample_attn",
    )(page_table_flat, top_flat, q_t, k_t, v_t, cache_kt, cache_vt)


def _sample_tail_kernel(x_ref, po_ref, aot_ref, w_ref, g_ref, wr_ref, wg_ref, wu_ref, wd_ref, gf_ref,
                        y_ref, h_sc, xn_sc, gate_sc, acc_sc):
    e = pl.program_id(0)

    @pl.when(e == 0)
    def _():
        ao = jnp.transpose(aot_ref[...])
        mix = (jnp.dot(po_ref[...], w_ref[0:POOL_WIDTH, :], preferred_element_type=F32, precision=HIGHEST)
               + jnp.dot(ao, w_ref[POOL_WIDTH:, :], preferred_element_type=F32, precision=HIGHEST))
        h = x_ref[...] + mix
        h_sc[...] = h
        xn = _rms(h, g_ref[...])
        xn_sc[...] = xn
        gate_sc[...] = _route(jnp.dot(xn, wr_ref[...], preferred_element_type=F32, precision=HIGHEST))[0]
        acc_sc[...] = jnp.zeros_like(acc_sc)

    o = _swiglu(xn_sc[...].astype(BF16), wg_ref[...], wu_ref[...], wd_ref[...])
    acc_sc[...] += _gate_column(gate_sc[...], e) * o

    @pl.when(e == pl.num_programs(0) - 1)
    def _():
        y_ref[...] = _rms(h_sc[...] + acc_sc[...], gf_ref[...])


def _sample_tail(x_pad, po, ao_t, w_out, g_ffn, w_router, wg_bf, wu_bf, wd_bf, g_final):
    R, D = x_pad.shape
    E, _, DE = wg_bf.shape
    full = lambda shape: pl.BlockSpec(shape, lambda e: (0,) * len(shape))
    return pl.pallas_call(
        _sample_tail_kernel,
        grid=(E,),
        in_specs=[full((R, D)), full((R, POOL_WIDTH)), full((ATTN_WIDTH, R)), full(w_out.shape),
                  full((1, D)), full(w_router.shape),
                  pl.BlockSpec((None, D, DE), lambda e: (e, 0, 0)),
                  pl.BlockSpec((None, D, DE), lambda e: (e, 0, 0)),
                  pl.BlockSpec((None, DE, D), lambda e: (e, 0, 0)),
                  full((1, D))],
        out_specs=full((R, D)),
        out_shape=jax.ShapeDtypeStruct((R, D), F32),
        scratch_shapes=[pltpu.VMEM((R, D), F32), pltpu.VMEM((R, D), F32), pltpu.VMEM((R, LANES), F32),
                        pltpu.VMEM((R, D), F32)],
        compiler_params=_cparams(("arbitrary",)),
        name="sample_tail",
    )(x_pad, po, ao_t, w_out, g_ffn, w_router, wg_bf, wu_bf, wd_bf, g_final)


def _tile(n, pref):
    while n % pref:
        pref //= 2
    return pref


def kernel(x_prompt, x_sample, cache_k, cache_v, state_pool, page_table, norm_mix, w_in, w_pool, pool_scale,
           w_out, norm_ffn, w_router_group, w_router_expert, w_gate, w_up, w_down, norm_final):
    B, S, D = x_prompt.shape
    DB, DS, _ = x_sample.shape
    depth, n_phys, page = cache_k.shape[:3]
    assert depth == 1 and DS == 1, "single layer, one new token per sequence"
    n_pages = page_table.shape[1]
    past_len = n_pages * page
    assert S % MOBA_BLOCK == 0 and past_len % MOBA_BLOCK == 0 and MOBA_BLOCK % page == 0
    assert n_pages % PAGES_PER_STEP == 0 and past_len // MOBA_BLOCK >= MOBA_TOPK and DB <= LANES
    T = B * S

    g_mix = norm_mix[0][None, :]
    g_ffn = norm_ffn[0][None, :]
    g_final = norm_final[None, :]
    ps = pool_scale[0][None, :]
    w_router = jnp.concatenate(
        [w_router_expert[0], w_router_group[0],
         jnp.zeros((D, LANES - N_EXPERTS - N_EXPERT_GROUPS), F32)], axis=1)
    wg_bf, wu_bf, wd_bf = w_gate[0].astype(BF16), w_up[0].astype(BF16), w_down[0].astype(BF16)

    ts = _tile(S, 512)
    kt_f, vt_f, q_bf, kt_bf, vt_bf, po, ksum, u_last = _inproj(
        x_prompt, g_mix, w_in[0], w_pool[0].astype(BF16), ps, ts)
    ksum = ksum[:, :, :ts // MOBA_BLOCK].reshape(B, S // MOBA_BLOCK, ATTN_WIDTH)
    ao = _moba_prompt(q_bf, kt_bf, vt_bf, ksum)
    h2, xg, counts = _outproj(x_prompt.reshape(T, D), po.reshape(T, POOL_WIDTH), ao.reshape(T, ATTN_WIDTH),
                              w_out[0].astype(BF16), g_ffn, w_router.astype(BF16), _tile(T, 512))
    dest, tile_group, n_used, n_rows = _moe_layout(xg, D, counts)
    dest3d = dest.reshape(T // ROW_TILE, 1, ROW_TILE)
    xs = _dispatch(dest3d, xg, n_rows)
    moe_sorted = _moe(tile_group, n_used, xs, wg_bf, wu_bf, wd_bf)
    y_prompt = _combine(dest3d, h2, g_final, moe_sorted).reshape(B, S, D)
    k_prompt = jnp.transpose(kt_f.reshape(B, N_HEADS, HEAD_DIM, S), (0, 3, 1, 2))[None]
    v_prompt = jnp.transpose(vt_f.reshape(B, N_HEADS, HEAD_DIM, S), (0, 3, 1, 2))[None]
    pool_prompt = u_last[None, :, HALO - POOL_HIST:, :]

    xs = x_sample.reshape(DB, D)
    x_pad = jnp.pad(xs, ((0, LANES - DB), (0, 0)))
    pos_s = jnp.full((1,), past_len, jnp.int32)
    sp_t = jnp.swapaxes(state_pool[0], 0, 1)
    u_s, k_s, v_s, po_s, q_t, k_t, v_t = _sample_inproj(x_pad, g_mix, w_in[0], pos_s, sp_t, w_pool[0], ps)
    cache_kt = jnp.transpose(cache_k[0], (0, 2, 3, 1)).reshape(n_phys, ATTN_WIDTH, page)
    cache_vt = jnp.transpose(cache_v[0], (0, 2, 3, 1)).reshape(n_phys, ATTN_WIDTH, page)
    pt_flat = page_table.reshape(-1).astype(jnp.int32)
    top = _sample_scan(pt_flat, q_t, cache_kt, DB, n_pages)
    top_flat = top[:, :, :MOBA_TOPK].reshape(-1)
    ao_t = _sample_attn(pt_flat, top_flat, q_t, k_t, v_t, cache_kt, cache_vt, DB, n_pages)
    y_pad = _sample_tail(x_pad, po_s, ao_t, w_out[0], g_ffn, w_router, wg_bf, wu_bf, wd_bf, g_final)
    y_sample = y_pad[:DB].reshape(DB, 1, D)
    k_sample = k_s[:DB].reshape(1, DB, 1, N_HEADS, HEAD_DIM)
    v_sample = v_s[:DB].reshape(1, DB, 1, N_HEADS, HEAD_DIM)
    pool_sample = jnp.concatenate([state_pool[0][:, 1:], u_s[:DB, None, :]], axis=1)[None]
    return (y_prompt, y_sample, k_prompt, v_prompt, pool_prompt, k_sample, v_sample, pool_sample)
```

```python
import functools

import jax
import jax.numpy as jnp
import numpy as np
from jax import lax
from jax.experimental import pallas as pl
from jax.experimental.pallas import tpu as pltpu

F32 = jnp.float32
BF16 = jnp.bfloat16
HIGHEST = lax.Precision.HIGHEST

POOL_WINDOWS = (2, 4, 8, 16)
POOL_GROUP = 128
POOL_WIDTH = POOL_GROUP * len(POOL_WINDOWS)
POOL_HIST = max(POOL_WINDOWS) - 1
HALO = POOL_HIST + 1
HEAD_DIM = 64
N_HEADS = 8
ATTN_WIDTH = N_HEADS * HEAD_DIM
ROT_DIM = HEAD_DIM // 4
ROT_HALF = ROT_DIM // 2
ROPE_THETA = 500000.0
MOBA_BLOCK = 256
MOBA_TOPK = 3
N_EXPERT_GROUPS = 4
EXPERTS_PER_GROUP = 4
N_EXPERTS = N_EXPERT_GROUPS * EXPERTS_PER_GROUP
RMS_EPS = 1e-6

LANES = 128
SUBLANES = 8
HEADS_PER_TILE = LANES // HEAD_DIM
VMEM_LIMIT = 52 * 1024 * 1024

NEG_INF = float("-inf")
NEG_BIG = -1e30
BIG_IDX = 1 << 20
NT = (((1,), (1,)), ((), ()))
Q_SCALE = HEAD_DIM ** -0.5 * float(np.log2(np.e))


def _cparams(sem):
    return pltpu.CompilerParams(dimension_semantics=sem, vmem_limit_bytes=VMEM_LIMIT)


def _rms(x, g):
    r = lax.rsqrt(jnp.mean(x * x, axis=-1, keepdims=True) + RMS_EPS)
    return x * r * g


def _rope_angles(pos):
    inv = ROPE_THETA ** (-(jnp.arange(0, ROT_DIM, 2, dtype=F32) / ROT_DIM))
    ang = pos.astype(F32)[:, None] * inv[None, :]
    return jnp.cos(ang), jnp.sin(ang)


def _rope_tables(pos):
    cos, sin = _rope_angles(pos)
    d = np.arange(LANES) % HEAD_DIM
    fi = d % ROT_HALF
    cos_l, sin_l = cos[:, fi], sin[:, fi]
    c = jnp.where(d < ROT_DIM, cos_l, 1.0)
    s_lo = jnp.where(d < ROT_HALF, -sin_l, 0.0)
    s_hi = jnp.where((d >= ROT_HALF) & (d < ROT_DIM), sin_l, 0.0)
    return c.astype(F32), s_lo.astype(F32), s_hi.astype(F32)


def _rotate(x, c, s_lo, s_hi):
    outs = []
    for j in range(x.shape[1] // LANES):
        t = x[:, j * LANES:(j + 1) * LANES]
        up = pltpu.roll(t, LANES - ROT_HALF, axis=1)
        dn = pltpu.roll(t, ROT_HALF, axis=1)
        outs.append(t * c + up * s_lo + dn * s_hi)
    return jnp.concatenate(outs, axis=1)


def _rotate_t(xt, cos_t, sin_t):
    pieces = []
    for h in range(N_HEADS):
        r0 = h * HEAD_DIM
        x1 = xt[r0:r0 + ROT_HALF]
        x2 = xt[r0 + ROT_HALF:r0 + ROT_DIM]
        pieces += [x1 * cos_t - x2 * sin_t, x2 * cos_t + x1 * sin_t, xt[r0 + ROT_DIM:r0 + HEAD_DIM]]
    return jnp.concatenate(pieces, axis=0)


def _inproj_kernel(x_ref, g_ref, wuq_ref, wkv_ref, c_ref, slo_ref, shi_ref, ct_ref, st_ref, ind_ref,
                   wp_ref, ps_ref,
                   kf_ref, vf_ref, qb_ref, kb_ref, vb_ref, po_ref, ksum_ref, ulast_ref, ext_ref):
    s = pl.program_id(1)
    ts = x_ref.shape[0]
    xn = _rms(x_ref[...], g_ref[...]).astype(BF16)
    u = jnp.dot(xn, wuq_ref[:, 0:POOL_WIDTH], preferred_element_type=F32)
    q = jnp.dot(xn, wuq_ref[:, POOL_WIDTH:], preferred_element_type=F32)
    kt = lax.dot_general(wkv_ref[0:ATTN_WIDTH, :], xn, NT, preferred_element_type=F32)
    vt = lax.dot_general(wkv_ref[ATTN_WIDTH:, :], xn, NT, preferred_element_type=F32)
    q = _rotate(q, c_ref[...], slo_ref[...], shi_ref[...])
    kt = _rotate_t(kt, ct_ref[...], st_ref[...])
    kf_ref[...] = kt
    vf_ref[...] = vt
    qb_ref[...] = (q * Q_SCALE).astype(BF16)
    kb_ref[...] = kt.astype(BF16)
    ones = jnp.ones((HEAD_DIM, ts), BF16)
    vtb = vt.astype(BF16)
    vb_ref[...] = jnp.concatenate(
        [piece for h in range(N_HEADS) for piece in (vtb[h * HEAD_DIM:(h + 1) * HEAD_DIM], ones)], axis=0)
    ksum_ref[...] = lax.dot_general(ind_ref[...], kt, NT, preferred_element_type=F32, precision=HIGHEST)

    @pl.when(s == 0)
    def _():
        ext_ref[0:HALO, :] = jnp.zeros((HALO, POOL_WIDTH), F32)

    ext_ref[HALO:HALO + ts, :] = u
    pos1 = s * ts + lax.broadcasted_iota(jnp.int32, (ts, POOL_GROUP), 0) + 1
    outs = []
    for g, w in enumerate(POOL_WINDOWS):
        sl = slice(g * POOL_GROUP, (g + 1) * POOL_GROUP)
        ug = u[:, sl]
        acc = ug
        for j in range(1, w):
            acc = acc + ext_ref[HALO - j:HALO - j + ts, sl]
        cnt = jnp.minimum(pos1, w).astype(F32)
        d = (acc / cnt - ug).astype(BF16)
        outs.append(jnp.dot(d, wp_ref[g], preferred_element_type=F32))
    y = jnp.concatenate(outs, axis=1) * ps_ref[...]
    po_ref[...] = y.astype(BF16)
    tail = u[ts - HALO:ts, :]
    ext_ref[0:HALO, :] = tail

    @pl.when(s == pl.num_programs(1) - 1)
    def _():
        ulast_ref[...] = tail


def _inproj(x, g_mix, w_in, w_pool_bf, pool_scale, ts):
    B, S, D = x.shape
    ns = S // ts
    pos = jnp.arange(S, dtype=jnp.int32)
    c, s_lo, s_hi = _rope_tables(pos)
    cos, sin = _rope_angles(pos)
    cos_t, sin_t = cos.T, sin.T
    w_uq = w_in[:, :POOL_WIDTH + ATTN_WIDTH].astype(BF16)
    w_kv_t = w_in[:, POOL_WIDTH + ATTN_WIDTH:].T.astype(BF16)
    ind = (np.arange(ts)[None, :] // MOBA_BLOCK == np.arange(SUBLANES)[:, None]).astype(np.float32)
    row = lambda b, s: (b, s, 0)
    col = lambda b, s: (b, 0, s)
    tab = pl.BlockSpec((ts, LANES), lambda b, s: (s, 0))
    tab_t = pl.BlockSpec((ROT_HALF, ts), lambda b, s: (0, s))
    full2 = lambda shape: pl.BlockSpec(shape, lambda b, s: (0, 0))
    act = lambda: pl.BlockSpec((None, ts, ATTN_WIDTH), row)
    act_t = lambda: pl.BlockSpec((None, ATTN_WIDTH, ts), col)
    out_shapes = (
        jax.ShapeDtypeStruct((B, ATTN_WIDTH, S), F32),
        jax.ShapeDtypeStruct((B, ATTN_WIDTH, S), F32),
        jax.ShapeDtypeStruct((B, S, ATTN_WIDTH), BF16),
        jax.ShapeDtypeStruct((B, ATTN_WIDTH, S), BF16),
        jax.ShapeDtypeStruct((B, 2 * ATTN_WIDTH, S), BF16),
        jax.ShapeDtypeStruct((B, S, POOL_WIDTH), BF16),
        jax.ShapeDtypeStruct((B, ns, SUBLANES, ATTN_WIDTH), F32),
        jax.ShapeDtypeStruct((B, HALO, POOL_WIDTH), F32),
    )
    return pl.pallas_call(
        _inproj_kernel,
        grid=(B, ns),
        in_specs=[
            pl.BlockSpec((None, ts, D), row),
            full2((1, D)),
            full2(w_uq.shape),
            full2(w_kv_t.shape),
            tab, tab, tab, tab_t, tab_t,
            full2(ind.shape),
            pl.BlockSpec(w_pool_bf.shape, lambda b, s: (0, 0, 0)),
            full2((1, POOL_WIDTH)),
        ],
        out_specs=(act_t(), act_t(), act(), act_t(), pl.BlockSpec((None, 2 * ATTN_WIDTH, ts), col), act(),
                   pl.BlockSpec((None, None, SUBLANES, ATTN_WIDTH), lambda b, s: (b, s, 0, 0)),
                   pl.BlockSpec((None, HALO, POOL_WIDTH), lambda b, s: (b, 0, 0))),
        out_shape=out_shapes,
        scratch_shapes=[pltpu.VMEM((HALO + ts, POOL_WIDTH), F32)],
        compiler_params=_cparams(("arbitrary", "arbitrary")),
        name="prompt_inproj",
    )(x, g_mix, w_uq, w_kv_t, c, s_lo, s_hi, cos_t, sin_t, jnp.asarray(ind), w_pool_bf, pool_scale)


def _moba_kernel(q_ref, k_ref, v_ref, e_ref, ksum_ref, o_ref, s_sc, *, chunk):
    c = pl.program_id(2)
    tq = q_ref.shape[0]
    nb = ksum_ref.shape[0]
    blocks_per_chunk = chunk // MOBA_BLOCK
    n_tiles = chunk // LANES
    q = q_ref[...]
    kmean = (ksum_ref[...] * (1.0 / MOBA_BLOCK)).astype(BF16)
    lane = lax.broadcasted_iota(jnp.int32, (tq, LANES), 1)
    blk_iota = lax.broadcasted_iota(jnp.int32, (nb, tq), 0)

    qp = []
    for h in range(HEADS_PER_TILE):
        in_head = (lane >= h * HEAD_DIM) & (lane < (h + 1) * HEAD_DIM)
        qm = jnp.where(in_head, q, jnp.zeros_like(q))
        g = lax.dot_general(kmean, qm, NT, preferred_element_type=F32)
        g = jnp.where(blk_iota < c, g, NEG_INF)
        sel = blk_iota == c
        for t in range(MOBA_TOPK):
            m = jnp.max(g, axis=0, keepdims=True)
            idx = jnp.min(jnp.where(g == m, blk_iota, BIG_IDX), axis=0, keepdims=True)
            hit = blk_iota == idx
            sel = sel | (hit & (t < c))
            g = jnp.where(hit, NEG_INF, g)
        bias_t = jnp.where(sel, 0.0, NEG_BIG)
        bias_t = jnp.concatenate([bias_t, jnp.zeros((LANES - nb, tq), F32)], axis=0)
        bias = jnp.transpose(bias_t).astype(BF16)
        qp.append(jnp.concatenate([qm, bias], axis=1))

    def keys(off):
        return jnp.concatenate([k_ref[:, pl.ds(off, chunk)], e_ref[:, pl.ds(off, chunk)]], axis=0)

    def fold(fn, acc, x):
        for j in range(n_tiles):
            acc = fn(acc, x[:, j * LANES:(j + 1) * LANES])
        return acc

    n_past = lax.shift_right_logical(c, int(np.log2(blocks_per_chunk)))

    def pass1(i, mrun):
        off = pl.multiple_of(i * chunk, chunk)
        kk = keys(off)
        new = []
        for h in range(HEADS_PER_TILE):
            s = jnp.dot(qp[h], kk, preferred_element_type=F32)
            s_sc[h, :, pl.ds(off, chunk)] = s
            new.append(fold(jnp.maximum, mrun[h], s))
        return tuple(new)

    mrun = lax.fori_loop(0, n_past, pass1, tuple(jnp.full((tq, LANES), NEG_BIG, F32)
                                                 for _ in range(HEADS_PER_TILE)))
    off = pl.multiple_of(n_past * chunk, chunk)
    kk = keys(off)
    kpos = off + lax.broadcasted_iota(jnp.int32, (tq, chunk), 1)
    qpos = c * MOBA_BLOCK + lax.broadcasted_iota(jnp.int32, (tq, chunk), 0)
    m_b = []
    for h in range(HEADS_PER_TILE):
        s = jnp.dot(qp[h], kk, preferred_element_type=F32)
        s = jnp.where(kpos <= qpos, s, NEG_BIG)
        s_sc[h, :, pl.ds(off, chunk)] = s
        m = jnp.max(fold(jnp.maximum, mrun[h], s), axis=1, keepdims=True)
        m_b.append(jnp.broadcast_to(m, (tq, LANES)))

    def pass2(i, accs):
        off = pl.multiple_of(i * chunk, chunk)
        new = []
        for h in range(HEADS_PER_TILE):
            s = s_sc[h, :, pl.ds(off, chunk)]
            pb = jnp.concatenate([jnp.exp2(s[:, j * LANES:(j + 1) * LANES] - m_b[h]).astype(BF16)
                                  for j in range(n_tiles)], axis=1)
            vv = v_ref[h * LANES:(h + 1) * LANES, pl.ds(off, chunk)]
            new.append(accs[h] + lax.dot_general(pb, vv, NT, preferred_element_type=F32))
        return tuple(new)

    acc0, acc1 = lax.fori_loop(0, n_past + 1, pass2, (jnp.zeros((tq, LANES), F32),) * HEADS_PER_TILE)
    out = jnp.where(lane < HEAD_DIM, acc0 / pltpu.roll(acc0, HEAD_DIM, axis=1),
                    pltpu.roll(acc1, HEAD_DIM, axis=1) / acc1)
    o_ref[...] = out.astype(o_ref.dtype)


def _moba_prompt(q_bf, kt_bf, vt_bf, ksum):
    B, S, W = q_bf.shape
    nb = S // MOBA_BLOCK
    assert nb <= LANES
    ntile = W // LANES
    chunk = min(S, 4 * MOBA_BLOCK)
    ind = (np.arange(S)[None, :] // MOBA_BLOCK == np.arange(LANES)[:, None]).astype(np.float32)
    blk = pl.BlockSpec((None, MOBA_BLOCK, LANES), lambda b, t, c: (b, c, t))
    assert HEADS_PER_TILE == 2
    seq = pl.BlockSpec((None, LANES, S), lambda b, t, c: (b, t, 0))
    seq_v = pl.BlockSpec((None, HEADS_PER_TILE * LANES, S), lambda b, t, c: (b, t, 0))
    return pl.pallas_call(
        functools.partial(_moba_kernel, chunk=chunk),
        grid=(B, ntile, nb),
        in_specs=[blk, seq, seq_v, pl.BlockSpec((LANES, S), lambda b, t, c: (0, 0)),
                  pl.BlockSpec((None, nb, LANES), lambda b, t, c: (b, 0, t))],
        out_specs=blk,
        out_shape=jax.ShapeDtypeStruct((B, S, W), BF16),
        scratch_shapes=[pltpu.VMEM((HEADS_PER_TILE, MOBA_BLOCK, S), F32)],
        compiler_params=_cparams(("arbitrary", "arbitrary", "arbitrary")),
        name="prompt_moba",
    )(q_bf, kt_bf, vt_bf, jnp.asarray(ind, BF16), ksum)


def _route(logits):
    lane = lax.broadcasted_iota(jnp.int32, logits.shape, 1)
    is_grp = (lane >= N_EXPERTS) & (lane < N_EXPERTS + N_EXPERT_GROUPS)
    gl = jnp.where(is_grp, logits, NEG_INF)
    ge = jnp.exp(gl - jnp.max(gl, axis=1, keepdims=True))
    pg = ge / jnp.sum(ge, axis=1, keepdims=True)
    pg_sel = jnp.max(pg, axis=1, keepdims=True)
    g_sel = jnp.min(jnp.where(is_grp & (pg == pg_sel), lane - N_EXPERTS, BIG_IDX), axis=1, keepdims=True)
    in_grp = (lane < N_EXPERTS) & (jnp.right_shift(lane, 2) == g_sel)
    el = jnp.where(in_grp, logits, NEG_INF)
    ee = jnp.exp(el - jnp.max(el, axis=1, keepdims=True))
    pe = jnp.where(in_grp, ee / jnp.sum(ee, axis=1, keepdims=True), -1.0)
    p1 = jnp.max(pe, axis=1, keepdims=True)
    i1 = jnp.min(jnp.where(pe == p1, lane, BIG_IDX), axis=1, keepdims=True)
    pe2 = jnp.where(lane == i1, -1.0, pe)
    p2 = jnp.max(pe2, axis=1, keepdims=True)
    i2 = jnp.min(jnp.where(pe2 == p2, lane, BIG_IDX), axis=1, keepdims=True)
    tot = p1 + p2
    gate = (jnp.where(lane == i1, p1 / tot * pg_sel, 0.0)
            + jnp.where(lane == i2, p2 / tot * pg_sel, 0.0))
    return gate, g_sel


def _gate_column(gate, e):
    lane = lax.broadcasted_iota(jnp.int32, gate.shape, 1)
    return jnp.sum(jnp.where(lane == e, gate, 0.0), axis=1, keepdims=True)


def _swiglu(x_bf, wg, wu, wd):
    hg = jnp.dot(x_bf, wg, preferred_element_type=F32)
    hu = jnp.dot(x_bf, wu, preferred_element_type=F32)
    h = hg * (1.0 / (1.0 + jnp.exp(-hg))) * hu
    return jnp.dot(h.astype(BF16), wd, preferred_element_type=F32)


GROUP_LANE = N_EXPERTS + N_EXPERT_GROUPS
RANK_LANE = GROUP_LANE + 1


def _outproj_kernel(x_ref, po_ref, ao_ref, w_ref, g_ref, wr_ref, tri_ref, h_ref, xg_ref, cnt_ref):
    i = pl.program_id(0)
    d = x_ref.shape[1]
    mix = (jnp.dot(po_ref[...], w_ref[0:POOL_WIDTH, :], preferred_element_type=F32)
           + jnp.dot(ao_ref[...], w_ref[POOL_WIDTH:, :], preferred_element_type=F32))
    h = x_ref[...] + mix
    h_ref[...] = h
    xn = _rms(h, g_ref[...])
    gate, g_sel = _route(jnp.dot(xn.astype(BF16), wr_ref[...], preferred_element_type=F32))

    @pl.when(i == 0)
    def _():
        cnt_ref[...] = jnp.zeros_like(cnt_ref)

    lane = lax.broadcasted_iota(jnp.int32, gate.shape, 1)
    onehot = jnp.where(lane == g_sel, 1.0, 0.0)
    before = jnp.dot(tri_ref[...], onehot.astype(BF16), preferred_element_type=F32) + cnt_ref[0:1, :]
    rank = jnp.sum(onehot * before, axis=1, keepdims=True)
    cnt_ref[...] = cnt_ref[...] + jnp.sum(onehot, axis=0, keepdims=True)
    info = jnp.where(lane == GROUP_LANE, g_sel.astype(F32), jnp.where(lane == RANK_LANE, rank, gate))
    xg_ref[:, 0:d] = xn
    xg_ref[:, d:] = info


def _outproj(x2d, po2d, ao2d, w_out_bf, g_ffn, w_router_bf, tm):
    T, D = x2d.shape
    row = lambda i: (i, 0)
    full = lambda shape: pl.BlockSpec(shape, lambda i: (0, 0))
    tri = np.tril(np.ones((tm, tm), np.float32), -1)
    return pl.pallas_call(
        _outproj_kernel,
        grid=(T // tm,),
        in_specs=[pl.BlockSpec((tm, D), row), pl.BlockSpec((tm, POOL_WIDTH), row),
                  pl.BlockSpec((tm, ATTN_WIDTH), row), full(w_out_bf.shape), full((1, D)),
                  full(w_router_bf.shape), full((tm, tm))],
        out_specs=(pl.BlockSpec((tm, D), row), pl.BlockSpec((tm, D + LANES), row), full((SUBLANES, LANES))),
        out_shape=(jax.ShapeDtypeStruct((T, D), F32), jax.ShapeDtypeStruct((T, D + LANES), F32),
                   jax.ShapeDtypeStruct((SUBLANES, LANES), F32)),
        compiler_params=_cparams(("arbitrary",)),
        name="prompt_outproj",
    )(x2d, po2d, ao2d, w_out_bf, g_ffn, w_router_bf, jnp.asarray(tri, BF16))


MOE_TILE = 512
ROW_TILE = 512
DMA_UNROLL = 8


def _dispatch_kernel(dest_ref, xg_ref, zeros_ref, xs_ref, sem_ref):
    del zeros_ref
    n = xg_ref.shape[0]

    def start(r, carry):
        pltpu.make_async_copy(xg_ref.at[pl.ds(r, 1)], xs_ref.at[pl.ds(dest_ref[0, r], 1)], sem_ref.at[0]).start()
        return carry

    lax.fori_loop(0, n, start, 0, unroll=DMA_UNROLL)
    pltpu.make_async_copy(xg_ref, xs_ref.at[pl.ds(0, n)], sem_ref.at[0]).wait()


def _dispatch(dest3d, xg, n_rows):
    T, W = xg.shape
    tm = dest3d.shape[2]
    return pl.pallas_call(
        _dispatch_kernel,
        grid=(T // tm,),
        in_specs=[pl.BlockSpec((None, 1, tm), lambda i: (i, 0, 0), memory_space=pltpu.SMEM),
                  pl.BlockSpec((tm, W), lambda i: (i, 0)),
                  pl.BlockSpec(memory_space=pl.ANY)],
        out_specs=pl.BlockSpec(memory_space=pl.ANY),
        out_shape=jax.ShapeDtypeStruct((n_rows, W), F32),
        scratch_shapes=[pltpu.SemaphoreType.DMA((1,))],
        input_output_aliases={2: 0},
        compiler_params=_cparams(("arbitrary",)),
        name="moe_dispatch",
    )(dest3d, xg, jnp.zeros((n_rows, W), F32))


def _moe_kernel(grp_ref, nused_ref, xs_ref, wg_ref, wu_ref, wd_ref, o_ref):
    i = pl.program_id(0)
    d = o_ref.shape[1]

    @pl.when(i < nused_ref[0])
    def _():
        x = xs_ref[:, 0:d].astype(BF16)
        gate = xs_ref[:, d:]
        first = grp_ref[i] * EXPERTS_PER_GROUP
        acc = jnp.zeros(o_ref.shape, F32)
        for j in range(EXPERTS_PER_GROUP):
            acc = acc + _gate_column(gate, first + j) * _swiglu(x, wg_ref[j], wu_ref[j], wd_ref[j])
        o_ref[...] = acc

    @pl.when(i >= nused_ref[0])
    def _():
        o_ref[...] = jnp.zeros_like(o_ref)


def _moe(tile_group, n_used, xs, wg_bf, wu_bf, wd_bf):
    n_rows, W = xs.shape
    E, D, DE = wg_bf.shape
    G = E // EXPERTS_PER_GROUP
    grouped = lambda w: w.reshape((G, EXPERTS_PER_GROUP) + w.shape[1:])
    tile = lambda i, grp, nu: (jnp.minimum(i, nu[0] - 1), 0)
    wspec = lambda shape: pl.BlockSpec((None, EXPERTS_PER_GROUP) + shape, lambda i, grp, nu: (grp[i], 0, 0, 0))
    return pl.pallas_call(
        _moe_kernel,
        grid_spec=pltpu.PrefetchScalarGridSpec(
            num_scalar_prefetch=2,
            grid=(n_rows // MOE_TILE,),
            in_specs=[pl.BlockSpec((MOE_TILE, W), tile), wspec((D, DE)), wspec((D, DE)), wspec((DE, D))],
            out_specs=pl.BlockSpec((MOE_TILE, D), lambda i, grp, nu: (i, 0)),
        ),
        out_shape=jax.ShapeDtypeStruct((n_rows, D), F32),
        compiler_params=_cparams(("arbitrary",)),
        name="moe_experts",
    )(tile_group, n_used, xs, grouped(wg_bf), grouped(wu_bf), grouped(wd_bf))


def _combine_copies_start(dest_ref, src_ref, buf_ref, sem_ref, slot):
    def start(r, carry):
        pltpu.make_async_copy(src_ref.at[pl.ds(dest_ref[0, r], 1)], buf_ref.at[slot, pl.ds(r, 1)],
                              sem_ref.at[slot]).start()
        return carry

    lax.fori_loop(0, buf_ref.shape[1], start, 0, unroll=DMA_UNROLL)


def _combine_kernel(dest_ref, dest_next_ref, h_ref, gf_ref, moe_ref, y_ref, buf_ref, sem_ref):
    i = pl.program_id(0)
    n = buf_ref.shape[1]
    slot = i % 2

    @pl.when(i == 0)
    def _():
        _combine_copies_start(dest_ref, moe_ref, buf_ref, sem_ref, slot)

    @pl.when(i + 1 < pl.num_programs(0))
    def _():
        _combine_copies_start(dest_next_ref, moe_ref, buf_ref, sem_ref, 1 - slot)

    pltpu.make_async_copy(moe_ref.at[pl.ds(0, n)], buf_ref.at[slot], sem_ref.at[slot]).wait()
    y_ref[...] = _rms(h_ref[...] + buf_ref[slot], gf_ref[...])


def _combine(dest3d, h2d, g_final, moe_sorted):
    T, D = h2d.shape
    nt, _, tm = dest3d.shape
    return pl.pallas_call(
        _combine_kernel,
        grid=(nt,),
        in_specs=[pl.BlockSpec((None, 1, tm), lambda i: (i, 0, 0), memory_space=pltpu.SMEM),
                  pl.BlockSpec((None, 1, tm), lambda i: (jnp.minimum(i + 1, nt - 1), 0, 0),
                               memory_space=pltpu.SMEM),
                  pl.BlockSpec((tm, D), lambda i: (i, 0)),
                  pl.BlockSpec((1, D), lambda i: (0, 0)),
                  pl.BlockSpec(memory_space=pl.ANY)],
        out_specs=pl.BlockSpec((tm, D), lambda i: (i, 0)),
        out_shape=jax.ShapeDtypeStruct((T, D), F32),
        scratch_shapes=[pltpu.VMEM((2, tm, D), F32), pltpu.SemaphoreType.DMA((2,))],
        compiler_params=_cparams(("arbitrary",)),
        name="moe_combine",
    )(dest3d, dest3d, h2d, g_final, moe_sorted)


def _moe_layout(xg, d, counts):
    grp = xg[:, d + GROUP_LANE].astype(jnp.int32)
    rank = xg[:, d + RANK_LANE].astype(jnp.int32)
    cnt = counts[0, :N_EXPERT_GROUPS].astype(jnp.int32)
    tiles = (cnt + MOE_TILE - 1) // MOE_TILE
    tile_end = jnp.cumsum(tiles)
    row_start = (tile_end - tiles) * MOE_TILE
    dest = row_start[grp] + rank
    max_tiles = xg.shape[0] // MOE_TILE + N_EXPERT_GROUPS
    tile_ids = jnp.arange(max_tiles, dtype=jnp.int32)
    tile_group = jnp.minimum(jnp.sum((tile_ids[:, None] >= tile_end[None, :]).astype(jnp.int32), axis=1),
                             N_EXPERT_GROUPS - 1)
    return dest, tile_group, tile_end[-1:].astype(jnp.int32), max_tiles * MOE_TILE


def _sample_inproj_kernel(x_ref, g_ref, w_ref, c_ref, slo_ref, shi_ref, sp_ref, wp_ref, ps_ref,
                          u_ref, k_ref, v_ref, po_ref, qt_ref, kt_ref, vt_ref):
    xn = _rms(x_ref[...], g_ref[...])

    def proj(i):
        return jnp.dot(xn, w_ref[:, i * POOL_WIDTH:(i + 1) * POOL_WIDTH],
                       preferred_element_type=F32, precision=HIGHEST)

    u, q, k, v = proj(0), proj(1), proj(2), proj(3)
    c, s_lo, s_hi = c_ref[...], slo_ref[...], shi_ref[...]
    q = _rotate(q, c, s_lo, s_hi)
    k = _rotate(k, c, s_lo, s_hi)
    u_ref[...] = u
    k_ref[...] = k
    v_ref[...] = v
    qt_ref[...] = jnp.transpose(q)
    kt_ref[...] = jnp.transpose(k)
    vt_ref[...] = jnp.transpose(v)
    db = sp_ref.shape[1]
    outs = []
    for g, w in enumerate(POOL_WINDOWS):
        sl = slice(g * POOL_GROUP, (g + 1) * POOL_GROUP)
        ug = u[0:db, sl]
        acc = ug
        for j in range(1, w):
            acc = acc + sp_ref[POOL_HIST - j, :, sl]
        d = acc / float(w) - ug
        outs.append(jnp.dot(d, wp_ref[g], preferred_element_type=F32, precision=HIGHEST))
    po_ref[...] = jnp.zeros_like(po_ref)
    po_ref[0:db, :] = jnp.concatenate(outs, axis=1) * ps_ref[...]


def _sample_inproj(x_pad, g_mix, w_in, pos, sp_t, w_pool, pool_scale):
    R, D = x_pad.shape
    c, s_lo, s_hi = _rope_tables(pos)
    row = jax.ShapeDtypeStruct((R, POOL_WIDTH), F32)
    col = jax.ShapeDtypeStruct((POOL_WIDTH, R), F32)
    return pl.pallas_call(
        _sample_inproj_kernel,
        out_shape=(row, row, row, row, col, col, col),
        compiler_params=pltpu.CompilerParams(vmem_limit_bytes=VMEM_LIMIT),
        name="sample_inproj",
    )(x_pad, g_mix, w_in, c, s_lo, s_hi, sp_t, w_pool, pool_scale)


def _column(mat, b):
    lane = lax.broadcasted_iota(jnp.int32, mat.shape, 1)
    col = jnp.sum(jnp.where(lane == b, mat, 0.0), axis=1, keepdims=True)
    return jnp.broadcast_to(col, mat.shape)


PAGES_PER_STEP = 16


def _scan_copies(pt_ref, cache_ref, buf_ref, sem_ref, step, slot):
    return [pltpu.make_async_copy(cache_ref.at[pt_ref[step * PAGES_PER_STEP + i]],
                                  buf_ref.at[slot, i], sem_ref.at[slot])
            for i in range(PAGES_PER_STEP)]


def _sample_scan_kernel(pt_ref, qt_ref, cache_ref, top_ref, buf_ref, sem_ref, qb_sc, g_sc, *, nblk):
    b, ch = pl.program_id(0), pl.program_id(1)
    nch = pl.num_programs(1)
    step = b * nch + ch
    nsteps = pl.num_programs(0) * nch
    slot = step % 2

    @pl.when(step == 0)
    def _():
        for cp in _scan_copies(pt_ref, cache_ref, buf_ref, sem_ref, step, slot):
            cp.start()

    @pl.when(step + 1 < nsteps)
    def _():
        for cp in _scan_copies(pt_ref, cache_ref, buf_ref, sem_ref, step + 1, 1 - slot):
            cp.start()

    @pl.when(ch == 0)
    def _():
        qb_sc[...] = _column(qt_ref[...], b)
        g_sc[...] = jnp.zeros_like(g_sc)

    for cp in _scan_copies(pt_ref, cache_ref, buf_ref, sem_ref, step, slot):
        cp.wait()
    qb = qb_sc[...]
    lane = lax.broadcasted_iota(jnp.int32, (N_HEADS, LANES), 1)
    pages_per_blk = MOBA_BLOCK // buf_ref.shape[3]
    blks_per_step = PAGES_PER_STEP // pages_per_blk
    g = g_sc[...]
    for j in range(blks_per_step):
        tot = buf_ref[slot, j * pages_per_blk]
        for i in range(1, pages_per_blk):
            tot = tot + buf_ref[slot, j * pages_per_blk + i]
        pr = tot * qb
        per_head = jnp.concatenate(
            [jnp.sum(pr[h * HEAD_DIM:(h + 1) * HEAD_DIM], axis=0, keepdims=True) for h in range(N_HEADS)],
            axis=0)
        gcol = jnp.sum(per_head, axis=1, keepdims=True)
        g = jnp.where(lane == ch * blks_per_step + j, gcol, g)
    g_sc[...] = g

    @pl.when(ch == nch - 1)
    def _():
        gg = jnp.where(lane < nblk, g, NEG_INF)
        top = jnp.zeros((N_HEADS, LANES), jnp.int32)
        for t in range(MOBA_TOPK):
            m = jnp.max(gg, axis=1, keepdims=True)
            idx = jnp.min(jnp.where(gg == m, lane, BIG_IDX), axis=1, keepdims=True)
            top = jnp.where(lane == t, idx, top)
            gg = jnp.where(lane == idx, NEG_INF, gg)
        top_ref[...] = top


def _sample_scan(page_table_flat, q_t, cache_kt, DB, n_pages):
    _, W, page = cache_kt.shape
    nchunk = n_pages // PAGES_PER_STEP
    nblk = n_pages * page // MOBA_BLOCK
    assert nblk <= LANES
    return pl.pallas_call(
        functools.partial(_sample_scan_kernel, nblk=nblk),
        grid_spec=pltpu.PrefetchScalarGridSpec(
            num_scalar_prefetch=1,
            grid=(DB, nchunk),
            in_specs=[pl.BlockSpec(q_t.shape, lambda b, ch, pt: (0, 0)), pl.BlockSpec(memory_space=pl.ANY)],
            out_specs=pl.BlockSpec((None, N_HEADS, LANES), lambda b, ch, pt: (b, 0, 0)),
            scratch_shapes=[pltpu.VMEM((2, PAGES_PER_STEP, W, page), F32), pltpu.SemaphoreType.DMA((2,)),
                            pltpu.VMEM(q_t.shape, F32), pltpu.VMEM((N_HEADS, LANES), F32)],
        ),
        out_shape=jax.ShapeDtypeStruct((DB, N_HEADS, LANES), jnp.int32),
        compiler_params=_cparams(("arbitrary", "arbitrary")),
        name="sample_scan",
    )(page_table_flat, q_t, cache_kt)


def _attn_copies(pt_ref, top_ref, ck_ref, cv_ref, kbuf, vbuf, sem_ref, b, slot, n_pages, pages_per_blk):
    cps = []
    for h in range(N_HEADS):
        rows = pl.ds(h * HEAD_DIM, HEAD_DIM)
        for t in range(MOBA_TOPK):
            blk = top_ref[(b * N_HEADS + h) * MOBA_TOPK + t]
            for i in range(pages_per_blk):
                phys = pt_ref[b * n_pages + blk * pages_per_blk + i]
                j = t * pages_per_blk + i
                cps.append(pltpu.make_async_copy(ck_ref.at[phys, rows, :], kbuf.at[slot, h, j], sem_ref.at[slot]))
                cps.append(pltpu.make_async_copy(cv_ref.at[phys, rows, :], vbuf.at[slot, h, j], sem_ref.at[slot]))
    return cps


def _sample_attn_kernel(pt_ref, top_ref, qt_ref, kt_ref, vt_ref, ck_ref, cv_ref, o_ref, kbuf, vbuf, sem_ref,
                        *, n_pages):
    b = pl.program_id(0)
    nb = pl.num_programs(0)
    slot = b % 2
    n_sel = kbuf.shape[2]
    copies = functools.partial(_attn_copies, pt_ref, top_ref, ck_ref, cv_ref, kbuf, vbuf, sem_ref,
                               n_pages=n_pages, pages_per_blk=n_sel // MOBA_TOPK)

    @pl.when(b == 0)
    def _():
        o_ref[...] = jnp.zeros_like(o_ref)
        for cp in copies(b=b, slot=slot):
            cp.start()

    @pl.when(b + 1 < nb)
    def _():
        for cp in copies(b=b + 1, slot=1 - slot):
            cp.start()

    for cp in copies(b=b, slot=slot):
        cp.wait()

    scale = HEAD_DIM ** -0.5
    lane = lax.broadcasted_iota(jnp.int32, (HEAD_DIM, LANES), 1)
    for h in range(N_HEADS):
        rows = slice(h * HEAD_DIM, (h + 1) * HEAD_DIM)
        qb = _column(qt_ref[rows, :], b)
        kb = _column(kt_ref[rows, :], b)
        vb = _column(vt_ref[rows, :], b)
        s = [jnp.sum(kbuf[slot, h, j] * qb, axis=0, keepdims=True) * scale for j in range(n_sel)]
        s_own = jnp.sum(kb * qb, axis=0, keepdims=True) * scale
        m = s_own
        for sj in s:
            m = jnp.maximum(m, jnp.max(sj, axis=1, keepdims=True))
        p = [jnp.exp(sj - m) for sj in s]
        p_own = jnp.exp(s_own - m)
        l = p_own
        acc = jnp.zeros((HEAD_DIM, LANES), F32)
        for j, pj in enumerate(p):
            l = l + jnp.sum(pj, axis=1, keepdims=True)
            acc = acc + vbuf[slot, h, j] * pj
        col = (jnp.sum(acc, axis=1, keepdims=True) + p_own * vb) / l
        o_ref[rows, :] = jnp.where(lane == b, col, o_ref[rows, :])


def _sample_attn(page_table_flat, top_flat, q_t, k_t, v_t, cache_kt, cache_vt, DB, n_pages):
    _, W, page = cache_kt.shape
    n_sel = MOBA_TOPK * MOBA_BLOCK // page
    vm = lambda: pl.BlockSpec(q_t.shape, lambda b, pt, tp: (0, 0))
    hbm = lambda: pl.BlockSpec(memory_space=pl.ANY)
    return pl.pallas_call(
        functools.partial(_sample_attn_kernel, n_pages=n_pages),
        grid_spec=pltpu.PrefetchScalarGridSpec(
            num_scalar_prefetch=2,
            grid=(DB,),
            in_specs=[vm(), vm(), vm(), hbm(), hbm()],
            out_specs=vm(),
            scratch_shapes=[pltpu.VMEM((2, N_HEADS, n_sel, HEAD_DIM, page), F32),
                            pltpu.VMEM((2, N_HEADS, n_sel, HEAD_DIM, page), F32),
                            pltpu.SemaphoreType.DMA((2,))],
        ),
        out_shape=jax.ShapeDtypeStruct(q_t.shape, F32),
        compiler_params=_cparams(("arbitrary",)),
        name="sample_attn",
    )(page_table_flat, top_flat, q_t, k_t, v_t, cache_kt, cache_vt)


def _sample_tail_kernel(x_ref, po_ref, aot_ref, w_ref, g_ref, wr_ref, wg_ref, wu_ref, wd_ref, gf_ref,
                        y_ref, h_sc, xn_sc, gate_sc, acc_sc):
    e = pl.program_id(0)

    @pl.when(e == 0)
    def _():
        ao = jnp.transpose(aot_ref[...])
        mix = (jnp.dot(po_ref[...], w_ref[0:POOL_WIDTH, :], preferred_element_type=F32, precision=HIGHEST)
               + jnp.dot(ao, w_ref[POOL_WIDTH:, :], preferred_element_type=F32, precision=HIGHEST))
        h = x_ref[...] + mix
        h_sc[...] = h
        xn = _rms(h, g_ref[...])
        xn_sc[...] = xn
        gate_sc[...] = _route(jnp.dot(xn, wr_ref[...], preferred_element_type=F32, precision=HIGHEST))[0]
        acc_sc[...] = jnp.zeros_like(acc_sc)

    o = _swiglu(xn_sc[...].astype(BF16), wg_ref[...], wu_ref[...], wd_ref[...])
    acc_sc[...] += _gate_column(gate_sc[...], e) * o

    @pl.when(e == pl.num_programs(0) - 1)
    def _():
        y_ref[...] = _rms(h_sc[...] + acc_sc[...], gf_ref[...])


def _sample_tail(x_pad, po, ao_t, w_out, g_ffn, w_router, wg_bf, wu_bf, wd_bf, g_final):
    R, D = x_pad.shape
    E, _, DE = wg_bf.shape
    full = lambda shape: pl.BlockSpec(shape, lambda e: (0,) * len(shape))
    return pl.pallas_call(
        _sample_tail_kernel,
        grid=(E,),
        in_specs=[full((R, D)), full((R, POOL_WIDTH)), full((ATTN_WIDTH, R)), full(w_out.shape),
                  full((1, D)), full(w_router.shape),
                  pl.BlockSpec((None, D, DE), lambda e: (e, 0, 0)),
                  pl.BlockSpec((None, D, DE), lambda e: (e, 0, 0)),
                  pl.BlockSpec((None, DE, D), lambda e: (e, 0, 0)),
                  full((1, D))],
        out_specs=full((R, D)),
        out_shape=jax.ShapeDtypeStruct((R, D), F32),
        scratch_shapes=[pltpu.VMEM((R, D), F32), pltpu.VMEM((R, D), F32), pltpu.VMEM((R, LANES), F32),
                        pltpu.VMEM((R, D), F32)],
        compiler_params=_cparams(("arbitrary",)),
        name="sample_tail",
    )(x_pad, po, ao_t, w_out, g_ffn, w_router, wg_bf, wu_bf, wd_bf, g_final)


def _tile(n, pref):
    while n % pref:
        pref //= 2
    return pref


def kernel(x_prompt, x_sample, cache_k, cache_v, state_pool, page_table, norm_mix, w_in, w_pool, pool_scale,
           w_out, norm_ffn, w_router_group, w_router_expert, w_gate, w_up, w_down, norm_final):
    B, S, D = x_prompt.shape
    DB, DS, _ = x_sample.shape
    depth, n_phys, page = cache_k.shape[:3]
    assert depth == 1 and DS == 1, "single layer, one new token per sequence"
    n_pages = page_table.shape[1]
    past_len = n_pages * page
    assert S % MOBA_BLOCK == 0 and past_len % MOBA_BLOCK == 0 and MOBA_BLOCK % page == 0
    assert n_pages % PAGES_PER_STEP == 0 and past_len // MOBA_BLOCK >= MOBA_TOPK and DB <= LANES
    T = B * S

    g_mix = norm_mix[0][None, :]
    g_ffn = norm_ffn[0][None, :]
    g_final = norm_final[None, :]
    ps = pool_scale[0][None, :]
    w_router = jnp.concatenate(
        [w_router_expert[0], w_router_group[0],
         jnp.zeros((D, LANES - N_EXPERTS - N_EXPERT_GROUPS), F32)], axis=1)
    wg_bf, wu_bf, wd_bf = w_gate[0].astype(BF16), w_up[0].astype(BF16), w_down[0].astype(BF16)

    ts = _tile(S, 512)
    kt_f, vt_f, q_bf, kt_bf, vt_bf, po, ksum, u_last = _inproj(
        x_prompt, g_mix, w_in[0], w_pool[0].astype(BF16), ps, ts)
    ksum = ksum[:, :, :ts // MOBA_BLOCK].reshape(B, S // MOBA_BLOCK, ATTN_WIDTH)
    ao = _moba_prompt(q_bf, kt_bf, vt_bf, ksum)
    h2, xg, counts = _outproj(x_prompt.reshape(T, D), po.reshape(T, POOL_WIDTH), ao.reshape(T, ATTN_WIDTH),
                              w_out[0].astype(BF16), g_ffn, w_router.astype(BF16), _tile(T, 512))
    dest, tile_group, n_used, n_rows = _moe_layout(xg, D, counts)
    dest3d = dest.reshape(T // ROW_TILE, 1, ROW_TILE)
    xs = _dispatch(dest3d, xg, n_rows)
    moe_sorted = _moe(tile_group, n_used, xs, wg_bf, wu_bf, wd_bf)
    y_prompt = _combine(dest3d, h2, g_final, moe_sorted).reshape(B, S, D)
    k_prompt = jnp.transpose(kt_f.reshape(B, N_HEADS, HEAD_DIM, S), (0, 3, 1, 2))[None]
    v_prompt = jnp.transpose(vt_f.reshape(B, N_HEADS, HEAD_DIM, S), (0, 3, 1, 2))[None]
    pool_prompt = u_last[None, :, HALO - POOL_HIST:, :]

    xs = x_sample.reshape(DB, D)
    x_pad = jnp.pad(xs, ((0, LANES - DB), (0, 0)))
    pos_s = jnp.full((1,), past_len, jnp.int32)
    sp_t = jnp.swapaxes(state_pool[0], 0, 1)
    u_s, k_s, v_s, po_s, q_t, k_t, v_t = _sample_inproj(x_pad, g_mix, w_in[0], pos_s, sp_t, w_pool[0], ps)
    cache_kt = jnp.transpose(cache_k[0], (0, 2, 3, 1)).reshape(n_phys, ATTN_WIDTH, page)
    cache_vt = jnp.transpose(cache_v[0], (0, 2, 3, 1)).reshape(n_phys, ATTN_WIDTH, page)
    pt_flat = page_table.reshape(-1).astype(jnp.int32)
    top = _sample_scan(pt_flat, q_t, cache_kt, DB, n_pages)
    top_flat = top[:, :, :MOBA_TOPK].reshape(-1)
    ao_t = _sample_attn(pt_flat, top_flat, q_t, k_t, v_t, cache_kt, cache_vt, DB, n_pages)
    y_pad = _sample_tail(x_pad, po_s, ao_t, w_out[0], g_ffn, w_router, wg_bf, wu_bf, wd_bf, g_final)
    y_sample = y_pad[:DB].reshape(DB, 1, D)
    k_sample = k_s[:DB].reshape(1, DB, 1, N_HEADS, HEAD_DIM)
    v_sample = v_s[:DB].reshape(1, DB, 1, N_HEADS, HEAD_DIM)
    pool_sample = jnp.concatenate([state_pool[0][:, 1:], u_s[:DB, None, :]], axis=1)[None]
    return (y_prompt, y_sample, k_prompt, v_prompt, pool_prompt, k_sample, v_sample, pool_sample)
```

```python
import functools

import jax
import jax.numpy as jnp
import numpy as np
from jax import lax
from jax.experimental import pallas as pl
from jax.experimental.pallas import tpu as pltpu

F32 = jnp.float32
BF16 = jnp.bfloat16
HIGHEST = lax.Precision.HIGHEST

POOL_WINDOWS = (2, 4, 8, 16)
POOL_GROUP = 128
POOL_WIDTH = POOL_GROUP * len(POOL_WINDOWS)
POOL_HIST = max(POOL_WINDOWS) - 1
HALO = POOL_HIST + 1
HEAD_DIM = 64
N_HEADS = 8
ATTN_WIDTH = N_HEADS * HEAD_DIM
ROT_DIM = HEAD_DIM // 4
ROT_HALF = ROT_DIM // 2
ROPE_THETA = 500000.0
MOBA_BLOCK = 256
MOBA_TOPK = 3
N_EXPERT_GROUPS = 4
EXPERTS_PER_GROUP = 4
N_EXPERTS = N_EXPERT_GROUPS * EXPERTS_PER_GROUP
RMS_EPS = 1e-6

LANES = 128
SUBLANES = 8
HEADS_PER_TILE = LANES // HEAD_DIM
VMEM_LIMIT = 52 * 1024 * 1024

NEG_INF = float("-inf")
NEG_BIG = -1e30
BIG_IDX = 1 << 20
NT = (((1,), (1,)), ((), ()))
Q_SCALE = HEAD_DIM ** -0.5 * float(np.log2(np.e))


def _cparams(sem):
    return pltpu.CompilerParams(dimension_semantics=sem, vmem_limit_bytes=VMEM_LIMIT)


def _rms(x, g):
    r = lax.rsqrt(jnp.mean(x * x, axis=-1, keepdims=True) + RMS_EPS)
    return x * r * g


def _rope_angles(pos):
    inv = ROPE_THETA ** (-(jnp.arange(0, ROT_DIM, 2, dtype=F32) / ROT_DIM))
    ang = pos.astype(F32)[:, None] * inv[None, :]
    return jnp.cos(ang), jnp.sin(ang)


def _rope_tables(pos):
    cos, sin = _rope_angles(pos)
    d = np.arange(LANES) % HEAD_DIM
    fi = d % ROT_HALF
    cos_l, sin_l = cos[:, fi], sin[:, fi]
    c = jnp.where(d < ROT_DIM, cos_l, 1.0)
    s_lo = jnp.where(d < ROT_HALF, -sin_l, 0.0)
    s_hi = jnp.where((d >= ROT_HALF) & (d < ROT_DIM), sin_l, 0.0)
    return c.astype(F32), s_lo.astype(F32), s_hi.astype(F32)


def _rotate(x, c, s_lo, s_hi):
    outs = []
    for j in range(x.shape[1] // LANES):
        t = x[:, j * LANES:(j + 1) * LANES]
        up = pltpu.roll(t, LANES - ROT_HALF, axis=1)
        dn = pltpu.roll(t, ROT_HALF, axis=1)
        outs.append(t * c + up * s_lo + dn * s_hi)
    return jnp.concatenate(outs, axis=1)


def _rotate_t(xt, cos_t, sin_t):
    pieces = []
    for h in range(N_HEADS):
        r0 = h * HEAD_DIM
        x1 = xt[r0:r0 + ROT_HALF]
        x2 = xt[r0 + ROT_HALF:r0 + ROT_DIM]
        pieces += [x1 * cos_t - x2 * sin_t, x2 * cos_t + x1 * sin_t, xt[r0 + ROT_DIM:r0 + HEAD_DIM]]
    return jnp.concatenate(pieces, axis=0)


def _inproj_kernel(x_ref, g_ref, wuq_ref, wkv_ref, c_ref, slo_ref, shi_ref, ct_ref, st_ref, ind_ref,
                   wp_ref, ps_ref,
                   kf_ref, vf_ref, qb_ref, kb_ref, vb_ref, po_ref, ksum_ref, ulast_ref, ext_ref):
    s = pl.program_id(1)
    ts = x_ref.shape[0]
    xn = _rms(x_ref[...], g_ref[...]).astype(BF16)
    u = jnp.dot(xn, wuq_ref[:, 0:POOL_WIDTH], preferred_element_type=F32)
    q = jnp.dot(xn, wuq_ref[:, POOL_WIDTH:], preferred_element_type=F32)
    kt = lax.dot_general(wkv_ref[0:ATTN_WIDTH, :], xn, NT, preferred_element_type=F32)
    vt = lax.dot_general(wkv_ref[ATTN_WIDTH:, :], xn, NT, preferred_element_type=F32)
    q = _rotate(q, c_ref[...], slo_ref[...], shi_ref[...])
    kt = _rotate_t(kt, ct_ref[...], st_ref[...])
    kf_ref[...] = kt
    vf_ref[...] = vt
    qb_ref[...] = (q * Q_SCALE).astype(BF16)
    kb_ref[...] = kt.astype(BF16)
    ones = jnp.ones((HEAD_DIM, ts), BF16)
    vtb = vt.astype(BF16)
    vb_ref[...] = jnp.concatenate(
        [piece for h in range(N_HEADS) for piece in (vtb[h * HEAD_DIM:(h + 1) * HEAD_DIM], ones)], axis=0)
    ksum_ref[...] = lax.dot_general(ind_ref[...], kt, NT, preferred_element_type=F32, precision=HIGHEST)

    @pl.when(s == 0)
    def _():
        ext_ref[0:HALO, :] = jnp.zeros((HALO, POOL_WIDTH), F32)

    ext_ref[HALO:HALO + ts, :] = u
    pos1 = s * ts + lax.broadcasted_iota(jnp.int32, (ts, POOL_GROUP), 0) + 1
    outs = []
    for g, w in enumerate(POOL_WINDOWS):
        sl = slice(g * POOL_GROUP, (g + 1) * POOL_GROUP)
        ug = u[:, sl]
        acc = ug
        for j in range(1, w):
            acc = acc + ext_ref[HALO - j:HALO - j + ts, sl]
        cnt = jnp.minimum(pos1, w).astype(F32)
        d = (acc / cnt - ug).astype(BF16)
        outs.append(jnp.dot(d, wp_ref[g], preferred_element_type=F32))
    y = jnp.concatenate(outs, axis=1) * ps_ref[...]
    po_ref[...] = y.astype(BF16)
    tail = u[ts - HALO:ts, :]
    ext_ref[0:HALO, :] = tail

    @pl.when(s == pl.num_programs(1) - 1)
    def _():
        ulast_ref[...] = tail


def _inproj(x, g_mix, w_in, w_pool_bf, pool_scale, ts):
    B, S, D = x.shape
    ns = S // ts
    pos = jnp.arange(S, dtype=jnp.int32)
    c, s_lo, s_hi = _rope_tables(pos)
    cos, sin = _rope_angles(pos)
    cos_t, sin_t = cos.T, sin.T
    w_uq = w_in[:, :POOL_WIDTH + ATTN_WIDTH].astype(BF16)
    w_kv_t = w_in[:, POOL_WIDTH + ATTN_WIDTH:].T.astype(BF16)
    ind = (np.arange(ts)[None, :] // MOBA_BLOCK == np.arange(SUBLANES)[:, None]).astype(np.float32)
    row = lambda b, s: (b, s, 0)
    col = lambda b, s: (b, 0, s)
    tab = pl.BlockSpec((ts, LANES), lambda b, s: (s, 0))
    tab_t = pl.BlockSpec((ROT_HALF, ts), lambda b, s: (0, s))
    full2 = lambda shape: pl.BlockSpec(shape, lambda b, s: (0, 0))
    act = lambda: pl.BlockSpec((None, ts, ATTN_WIDTH), row)
    act_t = lambda: pl.BlockSpec((None, ATTN_WIDTH, ts), col)
    out_shapes = (
        jax.ShapeDtypeStruct((B, ATTN_WIDTH, S), F32),
        jax.ShapeDtypeStruct((B, ATTN_WIDTH, S), F32),
        jax.ShapeDtypeStruct((B, S, ATTN_WIDTH), BF16),
        jax.ShapeDtypeStruct((B, ATTN_WIDTH, S), BF16),
        jax.ShapeDtypeStruct((B, 2 * ATTN_WIDTH, S), BF16),
        jax.ShapeDtypeStruct((B, S, POOL_WIDTH), BF16),
        jax.ShapeDtypeStruct((B, ns, SUBLANES, ATTN_WIDTH), F32),
        jax.ShapeDtypeStruct((B, HALO, POOL_WIDTH), F32),
    )
    return pl.pallas_call(
        _inproj_kernel,
        grid=(B, ns),
        in_specs=[
            pl.BlockSpec((None, ts, D), row),
            full2((1, D)),
            full2(w_uq.shape),
            full2(w_kv_t.shape),
            tab, tab, tab, tab_t, tab_t,
            full2(ind.shape),
            pl.BlockSpec(w_pool_bf.shape, lambda b, s: (0, 0, 0)),
            full2((1, POOL_WIDTH)),
        ],
        out_specs=(act_t(), act_t(), act(), act_t(), pl.BlockSpec((None, 2 * ATTN_WIDTH, ts), col), act(),
                   pl.BlockSpec((None, None, SUBLANES, ATTN_WIDTH), lambda b, s: (b, s, 0, 0)),
                   pl.BlockSpec((None, HALO, POOL_WIDTH), lambda b, s: (b, 0, 0))),
        out_shape=out_shapes,
        scratch_shapes=[pltpu.VMEM((HALO + ts, POOL_WIDTH), F32)],
        compiler_params=_cparams(("arbitrary", "arbitrary")),
        name="prompt_inproj",
    )(x, g_mix, w_uq, w_kv_t, c, s_lo, s_hi, cos_t, sin_t, jnp.asarray(ind), w_pool_bf, pool_scale)


def _moba_kernel(pt_ref, q_ref, k_ref, v_ref, e_ref, ksum_ref, qt_ref, cache_ref, o_ref, top_ref,
                 s_sc, buf_ref, sem_ref, qb_sc, g_sc, *, chunk, scan_nch, scan_nblk):
    n1, n2 = pl.num_programs(1), pl.num_programs(2)
    step = (pl.program_id(0) * n1 + pl.program_id(1)) * n2 + pl.program_id(2)
    _scan_step(step, pl.num_programs(0) * n1 * n2, scan_nch, pt_ref, qt_ref, cache_ref, top_ref,
               buf_ref, sem_ref, qb_sc, g_sc, scan_nblk)

    c = pl.program_id(2)
    tq = q_ref.shape[0]
    nb = ksum_ref.shape[0]
    blocks_per_chunk = chunk // MOBA_BLOCK
    n_tiles = chunk // LANES
    q = q_ref[...]
    kmean = (ksum_ref[...] * (1.0 / MOBA_BLOCK)).astype(BF16)
    lane = lax.broadcasted_iota(jnp.int32, (tq, LANES), 1)
    blk_iota = lax.broadcasted_iota(jnp.int32, (nb, tq), 0)

    qp = []
    for h in range(HEADS_PER_TILE):
        in_head = (lane >= h * HEAD_DIM) & (lane < (h + 1) * HEAD_DIM)
        qm = jnp.where(in_head, q, jnp.zeros_like(q))
        g = lax.dot_general(kmean, qm, NT, preferred_element_type=F32)
        g = jnp.where(blk_iota < c, g, NEG_INF)
        sel = blk_iota == c
        for t in range(MOBA_TOPK):
            m = jnp.max(g, axis=0, keepdims=True)
            idx = jnp.min(jnp.where(g == m, blk_iota, BIG_IDX), axis=0, keepdims=True)
            hit = blk_iota == idx
            sel = sel | (hit & (t < c))
            g = jnp.where(hit, NEG_INF, g)
        bias_t = jnp.where(sel, 0.0, NEG_BIG)
        bias_t = jnp.concatenate([bias_t, jnp.zeros((LANES - nb, tq), F32)], axis=0)
        bias = jnp.transpose(bias_t).astype(BF16)
        qp.append(jnp.concatenate([qm, bias], axis=1))

    def keys(off):
        return jnp.concatenate([k_ref[:, pl.ds(off, chunk)], e_ref[:, pl.ds(off, chunk)]], axis=0)

    def fold(fn, acc, x):
        for j in range(n_tiles):
            acc = fn(acc, x[:, j * LANES:(j + 1) * LANES])
        return acc

    n_past = lax.shift_right_logical(c, int(np.log2(blocks_per_chunk)))

    def pass1(i, mrun):
        off = pl.multiple_of(i * chunk, chunk)
        kk = keys(off)
        new = []
        for h in range(HEADS_PER_TILE):
            s = jnp.dot(qp[h], kk, preferred_element_type=F32)
            s_sc[h, :, pl.ds(off, chunk)] = s
            new.append(fold(jnp.maximum, mrun[h], s))
        return tuple(new)

    mrun = lax.fori_loop(0, n_past, pass1, tuple(jnp.full((tq, LANES), NEG_BIG, F32)
                                                 for _ in range(HEADS_PER_TILE)))
    off = pl.multiple_of(n_past * chunk, chunk)
    kk = keys(off)
    kpos = off + lax.broadcasted_iota(jnp.int32, (tq, chunk), 1)
    qpos = c * MOBA_BLOCK + lax.broadcasted_iota(jnp.int32, (tq, chunk), 0)
    m_b = []
    for h in range(HEADS_PER_TILE):
        s = jnp.dot(qp[h], kk, preferred_element_type=F32)
        s = jnp.where(kpos <= qpos, s, NEG_BIG)
        s_sc[h, :, pl.ds(off, chunk)] = s
        m = jnp.max(fold(jnp.maximum, mrun[h], s), axis=1, keepdims=True)
        m_b.append(jnp.broadcast_to(m, (tq, LANES)))

    def pass2(i, accs):
        off = pl.multiple_of(i * chunk, chunk)
        new = []
        for h in range(HEADS_PER_TILE):
            s = s_sc[h, :, pl.ds(off, chunk)]
            pb = jnp.concatenate([jnp.exp2(s[:, j * LANES:(j + 1) * LANES] - m_b[h]).astype(BF16)
                                  for j in range(n_tiles)], axis=1)
            vv = v_ref[h * LANES:(h + 1) * LANES, pl.ds(off, chunk)]
            new.append(accs[h] + lax.dot_general(pb, vv, NT, preferred_element_type=F32))
        return tuple(new)

    acc0, acc1 = lax.fori_loop(0, n_past + 1, pass2, (jnp.zeros((tq, LANES), F32),) * HEADS_PER_TILE)
    out = jnp.where(lane < HEAD_DIM, acc0 / pltpu.roll(acc0, HEAD_DIM, axis=1),
                    pltpu.roll(acc1, HEAD_DIM, axis=1) / acc1)
    o_ref[...] = out.astype(o_ref.dtype)


def _moba_prompt(q_bf, kt_bf, vt_bf, ksum, page_table_flat, q_t, cache_kt, DB, n_pages):
    B, S, W = q_bf.shape
    nb = S // MOBA_BLOCK
    assert nb <= LANES and HEADS_PER_TILE == 2
    ntile = W // LANES
    chunk = min(S, 4 * MOBA_BLOCK)
    ind = (np.arange(S)[None, :] // MOBA_BLOCK == np.arange(LANES)[:, None]).astype(np.float32)
    _, cw, page = cache_kt.shape
    nsteps = B * ntile * nb
    pages_per_step, rem = divmod(DB * n_pages, nsteps)
    scan_nch, rem2 = divmod(n_pages, max(pages_per_step, 1))
    assert rem == 0 and rem2 == 0 and pages_per_step % (MOBA_BLOCK // page) == 0, "cache sweep does not tile"
    scan_nblk = n_pages * page // MOBA_BLOCK
    assert scan_nblk <= LANES

    blk = pl.BlockSpec((None, MOBA_BLOCK, LANES), lambda b, t, c, pt: (b, c, t))
    seq = pl.BlockSpec((None, LANES, S), lambda b, t, c, pt: (b, t, 0))
    seq_v = pl.BlockSpec((None, HEADS_PER_TILE * LANES, S), lambda b, t, c, pt: (b, t, 0))
    const = lambda shape: pl.BlockSpec(shape, lambda b, t, c, pt: (0, 0))
    sample_of_step = lambda b, t, c, pt: (((b * ntile + t) * nb + c) // scan_nch, 0, 0)
    return pl.pallas_call(
        functools.partial(_moba_kernel, chunk=chunk, scan_nch=scan_nch, scan_nblk=scan_nblk),
        grid_spec=pltpu.PrefetchScalarGridSpec(
            num_scalar_prefetch=1,
            grid=(B, ntile, nb),
            in_specs=[blk, seq, seq_v, const((LANES, S)),
                      pl.BlockSpec((None, nb, LANES), lambda b, t, c, pt: (b, 0, t)),
                      const(q_t.shape), pl.BlockSpec(memory_space=pl.ANY)],
            out_specs=(blk, pl.BlockSpec((None, N_HEADS, LANES), sample_of_step)),
            scratch_shapes=[pltpu.VMEM((HEADS_PER_TILE, MOBA_BLOCK, S), F32),
                            pltpu.VMEM((2, pages_per_step, cw, page), F32), pltpu.SemaphoreType.DMA((2,)),
                            pltpu.VMEM(q_t.shape, F32), pltpu.VMEM((N_HEADS, LANES), F32)],
        ),
        out_shape=(jax.ShapeDtypeStruct((B, S, W), BF16), jax.ShapeDtypeStruct((DB, N_HEADS, LANES), jnp.int32)),
        compiler_params=_cparams(("arbitrary", "arbitrary", "arbitrary")),
        name="prompt_moba",
    )(page_table_flat, q_bf, kt_bf, vt_bf, jnp.asarray(ind, BF16), ksum, q_t, cache_kt)


def _route(logits):
    lane = lax.broadcasted_iota(jnp.int32, logits.shape, 1)
    is_grp = (lane >= N_EXPERTS) & (lane < N_EXPERTS + N_EXPERT_GROUPS)
    gl = jnp.where(is_grp, logits, NEG_INF)
    ge = jnp.exp(gl - jnp.max(gl, axis=1, keepdims=True))
    pg = ge / jnp.sum(ge, axis=1, keepdims=True)
    pg_sel = jnp.max(pg, axis=1, keepdims=True)
    g_sel = jnp.min(jnp.where(is_grp & (pg == pg_sel), lane - N_EXPERTS, BIG_IDX), axis=1, keepdims=True)
    in_grp = (lane < N_EXPERTS) & (jnp.right_shift(lane, 2) == g_sel)
    el = jnp.where(in_grp, logits, NEG_INF)
    ee = jnp.exp(el - jnp.max(el, axis=1, keepdims=True))
    pe = jnp.where(in_grp, ee / jnp.sum(ee, axis=1, keepdims=True), -1.0)
    p1 = jnp.max(pe, axis=1, keepdims=True)
    i1 = jnp.min(jnp.where(pe == p1, lane, BIG_IDX), axis=1, keepdims=True)
    pe2 = jnp.where(lane == i1, -1.0, pe)
    p2 = jnp.max(pe2, axis=1, keepdims=True)
    i2 = jnp.min(jnp.where(pe2 == p2, lane, BIG_IDX), axis=1, keepdims=True)
    tot = p1 + p2
    gate = (jnp.where(lane == i1, p1 / tot * pg_sel, 0.0)
            + jnp.where(lane == i2, p2 / tot * pg_sel, 0.0))
    return gate, (g_sel, i1, i2)


PAIRS_PER_GROUP = EXPERTS_PER_GROUP * (EXPERTS_PER_GROUP - 1) // 2
N_CLASSES = N_EXPERT_GROUPS * PAIRS_PER_GROUP
_PAIRS = [(a, b) for a in range(EXPERTS_PER_GROUP) for b in range(a + 1, EXPERTS_PER_GROUP)]
CLASS_EXPERTS = np.array([[g * EXPERTS_PER_GROUP + a, g * EXPERTS_PER_GROUP + b]
                          for g in range(N_EXPERT_GROUPS) for a, b in _PAIRS], np.int32)


def _route_class(g_sel, i1, i2):
    assert EXPERTS_PER_GROUP == 4
    base = g_sel * EXPERTS_PER_GROUP
    a = jnp.minimum(i1, i2) - base
    b = jnp.maximum(i1, i2) - base
    pair = jnp.right_shift(a * (5 - a), 1) + b - 1
    return g_sel * PAIRS_PER_GROUP + pair


def _gate_column(gate, e):
    lane = lax.broadcasted_iota(jnp.int32, gate.shape, 1)
    return jnp.sum(jnp.where(lane == e, gate, 0.0), axis=1, keepdims=True)


def _swiglu(x_bf, wg, wu, wd):
    hg = jnp.dot(x_bf, wg, preferred_element_type=F32)
    hu = jnp.dot(x_bf, wu, preferred_element_type=F32)
    h = hg * (1.0 / (1.0 + jnp.exp(-hg))) * hu
    return jnp.dot(h.astype(BF16), wd, preferred_element_type=F32)


CLASS_LANE = N_EXPERTS + N_EXPERT_GROUPS
RANK_LANE = CLASS_LANE + 1


def _outproj_kernel(x_ref, po_ref, ao_ref, w_ref, g_ref, wr_ref, tri_ref, h_ref, xg_ref, cnt_ref):
    i = pl.program_id(0)
    d = x_ref.shape[1]
    mix = (jnp.dot(po_ref[...], w_ref[0:POOL_WIDTH, :], preferred_element_type=F32)
           + jnp.dot(ao_ref[...], w_ref[POOL_WIDTH:, :], preferred_element_type=F32))
    h = x_ref[...] + mix
    h_ref[...] = h
    xn = _rms(h, g_ref[...])
    gate, picks = _route(jnp.dot(xn.astype(BF16), wr_ref[...], preferred_element_type=F32))
    cls = _route_class(*picks)

    @pl.when(i == 0)
    def _():
        cnt_ref[...] = jnp.zeros_like(cnt_ref)

    lane = lax.broadcasted_iota(jnp.int32, gate.shape, 1)
    onehot = jnp.where(lane == cls, 1.0, 0.0)
    before = jnp.dot(tri_ref[...], onehot.astype(BF16), preferred_element_type=F32) + cnt_ref[0:1, :]
    rank = jnp.sum(onehot * before, axis=1, keepdims=True)
    cnt_ref[...] = cnt_ref[...] + jnp.sum(onehot, axis=0, keepdims=True)
    info = jnp.where(lane == CLASS_LANE, cls.astype(F32), jnp.where(lane == RANK_LANE, rank, gate))
    xg_ref[:, 0:d] = xn
    xg_ref[:, d:] = info


def _outproj(x2d, po2d, ao2d, w_out_bf, g_ffn, w_router_bf, tm):
    T, D = x2d.shape
    row = lambda i: (i, 0)
    full = lambda shape: pl.BlockSpec(shape, lambda i: (0, 0))
    tri = np.tril(np.ones((tm, tm), np.float32), -1)
    return pl.pallas_call(
        _outproj_kernel,
        grid=(T // tm,),
        in_specs=[pl.BlockSpec((tm, D), row), pl.BlockSpec((tm, POOL_WIDTH), row),
                  pl.BlockSpec((tm, ATTN_WIDTH), row), full(w_out_bf.shape), full((1, D)),
                  full(w_router_bf.shape), full((tm, tm))],
        out_specs=(pl.BlockSpec((tm, D), row), pl.BlockSpec((tm, D + LANES), row), full((SUBLANES, LANES))),
        out_shape=(jax.ShapeDtypeStruct((T, D), F32), jax.ShapeDtypeStruct((T, D + LANES), F32),
                   jax.ShapeDtypeStruct((SUBLANES, LANES), F32)),
        compiler_params=_cparams(("arbitrary",)),
        name="prompt_outproj",
    )(x2d, po2d, ao2d, w_out_bf, g_ffn, w_router_bf, jnp.asarray(tri, BF16))


MOE_TILE = 512
ROW_TILE = 512
DMA_UNROLL = 8


def _dispatch_kernel(dest_ref, xg_ref, zeros_ref, xs_ref, sem_ref):
    del zeros_ref
    n = xg_ref.shape[0]

    def start(r, carry):
        pltpu.make_async_copy(xg_ref.at[pl.ds(r, 1)], xs_ref.at[pl.ds(dest_ref[0, r], 1)], sem_ref.at[0]).start()
        return carry

    lax.fori_loop(0, n, start, 0, unroll=DMA_UNROLL)
    pltpu.make_async_copy(xg_ref, xs_ref.at[pl.ds(0, n)], sem_ref.at[0]).wait()


def _dispatch(dest3d, xg, n_rows):
    T, W = xg.shape
    tm = dest3d.shape[2]
    return pl.pallas_call(
        _dispatch_kernel,
        grid=(T // tm,),
        in_specs=[pl.BlockSpec((None, 1, tm), lambda i: (i, 0, 0), memory_space=pltpu.SMEM),
                  pl.BlockSpec((tm, W), lambda i: (i, 0)),
                  pl.BlockSpec(memory_space=pl.ANY)],
        out_specs=pl.BlockSpec(memory_space=pl.ANY),
        out_shape=jax.ShapeDtypeStruct((n_rows, W), F32),
        scratch_shapes=[pltpu.SemaphoreType.DMA((1,))],
        input_output_aliases={2: 0},
        compiler_params=_cparams(("arbitrary",)),
        name="moe_dispatch",
    )(dest3d, xg, jnp.zeros((n_rows, W), F32))


def _moe_kernel(e1_ref, e2_ref, nused_ref, xs_ref, wg1_ref, wu1_ref, wd1_ref, wg2_ref, wu2_ref, wd2_ref, o_ref):
    i = pl.program_id(0)
    d = o_ref.shape[1]

    @pl.when(i < nused_ref[0])
    def _():
        x = xs_ref[:, 0:d].astype(BF16)
        gate = xs_ref[:, d:]
        o_ref[...] = (_gate_column(gate, e1_ref[i]) * _swiglu(x, wg1_ref[...], wu1_ref[...], wd1_ref[...])
                      + _gate_column(gate, e2_ref[i]) * _swiglu(x, wg2_ref[...], wu2_ref[...], wd2_ref[...]))

    @pl.when(i >= nused_ref[0])
    def _():
        o_ref[...] = jnp.zeros_like(o_ref)


def _moe(tile_e1, tile_e2, n_used, xs, wg_bf, wu_bf, wd_bf):
    n_rows, W = xs.shape
    E, D, DE = wg_bf.shape
    tile = lambda i, e1, e2, nu: (jnp.maximum(jnp.minimum(i, nu[0] - 1), 0), 0)
    first = lambda shape: pl.BlockSpec((None,) + shape, lambda i, e1, e2, nu: (e1[i], 0, 0))
    second = lambda shape: pl.BlockSpec((None,) + shape, lambda i, e1, e2, nu: (e2[i], 0, 0))
    return pl.pallas_call(
        _moe_kernel,
        grid_spec=pltpu.PrefetchScalarGridSpec(
            num_scalar_prefetch=3,
            grid=(n_rows // MOE_TILE,),
            in_specs=[pl.BlockSpec((MOE_TILE, W), tile),
                      first((D, DE)), first((D, DE)), first((DE, D)),
                      second((D, DE)), second((D, DE)), second((DE, D))],
            out_specs=pl.BlockSpec((MOE_TILE, D), lambda i, e1, e2, nu: (i, 0)),
        ),
        out_shape=jax.ShapeDtypeStruct((n_rows, D), F32),
        compiler_params=_cparams(("arbitrary",)),
        name="moe_experts",
    )(tile_e1, tile_e2, n_used, xs, wg_bf, wu_bf, wd_bf, wg_bf, wu_bf, wd_bf)


def _combine_copies_start(dest_ref, src_ref, buf_ref, sem_ref, slot):
    def start(r, carry):
        pltpu.make_async_copy(src_ref.at[pl.ds(dest_ref[0, r], 1)], buf_ref.at[slot, pl.ds(r, 1)],
                              sem_ref.at[slot]).start()
        return carry

    lax.fori_loop(0, buf_ref.shape[1], start, 0, unroll=DMA_UNROLL)


def _combine_kernel(dest_ref, dest_next_ref, h_ref, gf_ref, moe_ref, y_ref, buf_ref, sem_ref):
    i = pl.program_id(0)
    n = buf_ref.shape[1]
    slot = i % 2

    @pl.when(i == 0)
    def _():
        _combine_copies_start(dest_ref, moe_ref, buf_ref, sem_ref, slot)

    @pl.when(i + 1 < pl.num_programs(0))
    def _():
        _combine_copies_start(dest_next_ref, moe_ref, buf_ref, sem_ref, 1 - slot)

    pltpu.make_async_copy(moe_ref.at[pl.ds(0, n)], buf_ref.at[slot], sem_ref.at[slot]).wait()
    y_ref[...] = _rms(h_ref[...] + buf_ref[slot], gf_ref[...])


def _combine(dest3d, h2d, g_final, moe_sorted):
    T, D = h2d.shape
    nt, _, tm = dest3d.shape
    return pl.pallas_call(
        _combine_kernel,
        grid=(nt,),
        in_specs=[pl.BlockSpec((None, 1, tm), lambda i: (i, 0, 0), memory_space=pltpu.SMEM),
                  pl.BlockSpec((None, 1, tm), lambda i: (jnp.minimum(i + 1, nt - 1), 0, 0),
                               memory_space=pltpu.SMEM),
                  pl.BlockSpec((tm, D), lambda i: (i, 0)),
                  pl.BlockSpec((1, D), lambda i: (0, 0)),
                  pl.BlockSpec(memory_space=pl.ANY)],
        out_specs=pl.BlockSpec((tm, D), lambda i: (i, 0)),
        out_shape=jax.ShapeDtypeStruct((T, D), F32),
        scratch_shapes=[pltpu.VMEM((2, tm, D), F32), pltpu.SemaphoreType.DMA((2,))],
        compiler_params=_cparams(("arbitrary",)),
        name="moe_combine",
    )(dest3d, dest3d, h2d, g_final, moe_sorted)


def _moe_layout(xg, d, counts):
    cls = xg[:, d + CLASS_LANE].astype(jnp.int32)
    rank = xg[:, d + RANK_LANE].astype(jnp.int32)
    cnt = counts[0, :N_CLASSES].astype(jnp.int32)
    tiles = (cnt + MOE_TILE - 1) // MOE_TILE
    tile_end = jnp.cumsum(tiles)
    row_start = (tile_end - tiles) * MOE_TILE
    dest = row_start[cls] + rank
    max_tiles = xg.shape[0] // MOE_TILE + N_CLASSES
    tile_ids = jnp.arange(max_tiles, dtype=jnp.int32)
    tile_class = jnp.minimum(jnp.sum((tile_ids[:, None] >= tile_end[None, :]).astype(jnp.int32), axis=1),
                             N_CLASSES - 1)
    experts = jnp.asarray(CLASS_EXPERTS)[tile_class]
    return dest, experts[:, 0], experts[:, 1], tile_end[-1:].astype(jnp.int32), max_tiles * MOE_TILE


def _sample_inproj_kernel(x_ref, g_ref, w_ref, c_ref, slo_ref, shi_ref, sp_ref, wp_ref, ps_ref,
                          u_ref, k_ref, v_ref, po_ref, qt_ref, kt_ref, vt_ref):
    xn = _rms(x_ref[...], g_ref[...])

    def proj(i):
        return jnp.dot(xn, w_ref[:, i * POOL_WIDTH:(i + 1) * POOL_WIDTH],
                       preferred_element_type=F32, precision=HIGHEST)

    u, q, k, v = proj(0), proj(1), proj(2), proj(3)
    c, s_lo, s_hi = c_ref[...], slo_ref[...], shi_ref[...]
    q = _rotate(q, c, s_lo, s_hi)
    k = _rotate(k, c, s_lo, s_hi)
    u_ref[...] = u
    k_ref[...] = k
    v_ref[...] = v
    qt_ref[...] = jnp.transpose(q)
    kt_ref[...] = jnp.transpose(k)
    vt_ref[...] = jnp.transpose(v)
    db = sp_ref.shape[1]
    outs = []
    for g, w in enumerate(POOL_WINDOWS):
        sl = slice(g * POOL_GROUP, (g + 1) * POOL_GROUP)
        ug = u[0:db, sl]
        acc = ug
        for j in range(1, w):
            acc = acc + sp_ref[POOL_HIST - j, :, sl]
        d = acc / float(w) - ug
        outs.append(jnp.dot(d, wp_ref[g], preferred_element_type=F32, precision=HIGHEST))
    po_ref[...] = jnp.zeros_like(po_ref)
    po_ref[0:db, :] = jnp.concatenate(outs, axis=1) * ps_ref[...]


def _sample_inproj(x_pad, g_mix, w_in, pos, sp_t, w_pool, pool_scale):
    R, D = x_pad.shape
    c, s_lo, s_hi = _rope_tables(pos)
    row = jax.ShapeDtypeStruct((R, POOL_WIDTH), F32)
    col = jax.ShapeDtypeStruct((POOL_WIDTH, R), F32)
    return pl.pallas_call(
        _sample_inproj_kernel,
        out_shape=(row, row, row, row, col, col, col),
        compiler_params=pltpu.CompilerParams(vmem_limit_bytes=VMEM_LIMIT),
        name="sample_inproj",
    )(x_pad, g_mix, w_in, c, s_lo, s_hi, sp_t, w_pool, pool_scale)


def _column(mat, b):
    lane = lax.broadcasted_iota(jnp.int32, mat.shape, 1)
    col = jnp.sum(jnp.where(lane == b, mat, 0.0), axis=1, keepdims=True)
    return jnp.broadcast_to(col, mat.shape)


def _scan_copies(pt_ref, cache_ref, buf_ref, sem_ref, step, slot):
    n = buf_ref.shape[1]
    return [pltpu.make_async_copy(cache_ref.at[pt_ref[step * n + i]], buf_ref.at[slot, i], sem_ref.at[slot])
            for i in range(n)]


def _scan_step(step, nsteps, nch, pt_ref, qt_ref, cache_ref, top_ref, buf_ref, sem_ref, qb_sc, g_sc, nblk):
    b = step // nch
    ch = step - b * nch
    slot = step % 2

    @pl.when(step == 0)
    def _():
        for cp in _scan_copies(pt_ref, cache_ref, buf_ref, sem_ref, step, slot):
            cp.start()

    @pl.when(step + 1 < nsteps)
    def _():
        for cp in _scan_copies(pt_ref, cache_ref, buf_ref, sem_ref, step + 1, 1 - slot):
            cp.start()

    @pl.when(ch == 0)
    def _():
        qb_sc[...] = _column(qt_ref[...], b)
        g_sc[...] = jnp.zeros_like(g_sc)

    for cp in _scan_copies(pt_ref, cache_ref, buf_ref, sem_ref, step, slot):
        cp.wait()
    qb = qb_sc[...]
    lane = lax.broadcasted_iota(jnp.int32, (N_HEADS, LANES), 1)
    pages_per_blk = MOBA_BLOCK // buf_ref.shape[3]
    blks_per_step = buf_ref.shape[1] // pages_per_blk
    g = g_sc[...]
    for j in range(blks_per_step):
        tot = buf_ref[slot, j * pages_per_blk]
        for i in range(1, pages_per_blk):
            tot = tot + buf_ref[slot, j * pages_per_blk + i]
        pr = tot * qb
        per_head = jnp.concatenate(
            [jnp.sum(pr[h * HEAD_DIM:(h + 1) * HEAD_DIM], axis=0, keepdims=True) for h in range(N_HEADS)],
            axis=0)
        gcol = jnp.sum(per_head, axis=1, keepdims=True)
        g = jnp.where(lane == ch * blks_per_step + j, gcol, g)
    g_sc[...] = g

    @pl.when(ch == nch - 1)
    def _():
        gg = jnp.where(lane < nblk, g, NEG_INF)
        top = jnp.zeros((N_HEADS, LANES), jnp.int32)
        for t in range(MOBA_TOPK):
            m = jnp.max(gg, axis=1, keepdims=True)
            idx = jnp.min(jnp.where(gg == m, lane, BIG_IDX), axis=1, keepdims=True)
            top = jnp.where(lane == t, idx, top)
            gg = jnp.where(lane == idx, NEG_INF, gg)
        top_ref[...] = top


def _attn_copies(pt_ref, top_ref, ck_ref, cv_ref, kbuf, vbuf, sem_ref, b, slot, n_pages, pages_per_blk):
    cps = []
    for h in range(N_HEADS):
        rows = pl.ds(h * HEAD_DIM, HEAD_DIM)
        for t in range(MOBA_TOPK):
            blk = top_ref[(b * N_HEADS + h) * MOBA_TOPK + t]
            for i in range(pages_per_blk):
                phys = pt_ref[b * n_pages + blk * pages_per_blk + i]
                j = t * pages_per_blk + i
                cps.append(pltpu.make_async_copy(ck_ref.at[phys, rows, :], kbuf.at[slot, h, j], sem_ref.at[slot]))
                cps.append(pltpu.make_async_copy(cv_ref.at[phys, rows, :], vbuf.at[slot, h, j], sem_ref.at[slot]))
    return cps


def _sample_attn_kernel(pt_ref, top_ref, qt_ref, kt_ref, vt_ref, ck_ref, cv_ref, o_ref, kbuf, vbuf, sem_ref,
                        *, n_pages):
    b = pl.program_id(0)
    nb = pl.num_programs(0)
    slot = b % 2
    n_sel = kbuf.shape[2]
    copies = functools.partial(_attn_copies, pt_ref, top_ref, ck_ref, cv_ref, kbuf, vbuf, sem_ref,
                               n_pages=n_pages, pages_per_blk=n_sel // MOBA_TOPK)

    @pl.when(b == 0)
    def _():
        o_ref[...] = jnp.zeros_like(o_ref)
        for cp in copies(b=b, slot=slot):
            cp.start()

    @pl.when(b + 1 < nb)
    def _():
        for cp in copies(b=b + 1, slot=1 - slot):
            cp.start()

    for cp in copies(b=b, slot=slot):
        cp.wait()

    scale = HEAD_DIM ** -0.5
    lane = lax.broadcasted_iota(jnp.int32, (HEAD_DIM, LANES), 1)
    for h in range(N_HEADS):
        rows = slice(h * HEAD_DIM, (h + 1) * HEAD_DIM)
        qb = _column(qt_ref[rows, :], b)
        kb = _column(kt_ref[rows, :], b)
        vb = _column(vt_ref[rows, :], b)
        s = [jnp.sum(kbuf[slot, h, j] * qb, axis=0, keepdims=True) * scale for j in range(n_sel)]
        s_own = jnp.sum(kb * qb, axis=0, keepdims=True) * scale
        m = s_own
        for sj in s:
            m = jnp.maximum(m, jnp.max(sj, axis=1, keepdims=True))
        p = [jnp.exp(sj - m) for sj in s]
        p_own = jnp.exp(s_own - m)
        l = p_own
        acc = jnp.zeros((HEAD_DIM, LANES), F32)
        for j, pj in enumerate(p):
            l = l + jnp.sum(pj, axis=1, keepdims=True)
            acc = acc + vbuf[slot, h, j] * pj
        col = (jnp.sum(acc, axis=1, keepdims=True) + p_own * vb) / l
        o_ref[rows, :] = jnp.where(lane == b, col, o_ref[rows, :])


def _sample_attn(page_table_flat, top_flat, q_t, k_t, v_t, cache_kt, cache_vt, DB, n_pages):
    _, W, page = cache_kt.shape
    n_sel = MOBA_TOPK * MOBA_BLOCK // page
    vm = lambda: pl.BlockSpec(q_t.shape, lambda b, pt, tp: (0, 0))
    hbm = lambda: pl.BlockSpec(memory_space=pl.ANY)
    return pl.pallas_call(
        functools.partial(_sample_attn_kernel, n_pages=n_pages),
        grid_spec=pltpu.PrefetchScalarGridSpec(
            num_scalar_prefetch=2,
            grid=(DB,),
            in_specs=[vm(), vm(), vm(), hbm(), hbm()],
            out_specs=vm(),
            scratch_shapes=[pltpu.VMEM((2, N_HEADS, n_sel, HEAD_DIM, page), F32),
                            pltpu.VMEM((2, N_HEADS, n_sel, HEAD_DIM, page), F32),
                            pltpu.SemaphoreType.DMA((2,))],
        ),
        out_shape=jax.ShapeDtypeStruct(q_t.shape, F32),
        compiler_params=_cparams(("arbitrary",)),
        name="sample_attn",
    )(page_table_flat, top_flat, q_t, k_t, v_t, cache_kt, cache_vt)


def _sample_tail_kernel(x_ref, po_ref, aot_ref, w_ref, g_ref, wr_ref, wg_ref, wu_ref, wd_ref, gf_ref,
                        y_ref, h_sc, xn_sc, gate_sc, acc_sc):
    e = pl.program_id(0)

    @pl.when(e == 0)
    def _():
        ao = jnp.transpose(aot_ref[...])
        mix = (jnp.dot(po_ref[...], w_ref[0:POOL_WIDTH, :], preferred_element_type=F32, precision=HIGHEST)
               + jnp.dot(ao, w_ref[POOL_WIDTH:, :], preferred_element_type=F32, precision=HIGHEST))
        h = x_ref[...] + mix
        h_sc[...] = h
        xn = _rms(h, g_ref[...])
        xn_sc[...] = xn
        gate_sc[...] = _route(jnp.dot(xn, wr_ref[...], preferred_element_type=F32, precision=HIGHEST))[0]
        acc_sc[...] = jnp.zeros_like(acc_sc)

    o = _swiglu(xn_sc[...].astype(BF16), wg_ref[...], wu_ref[...], wd_ref[...])
    acc_sc[...] += _gate_column(gate_sc[...], e) * o

    @pl.when(e == pl.num_programs(0) - 1)
    def _():
        y_ref[...] = _rms(h_sc[...] + acc_sc[...], gf_ref[...])


def _sample_tail(x_pad, po, ao_t, w_out, g_ffn, w_router, wg_bf, wu_bf, wd_bf, g_final):
    R, D = x_pad.shape
    E, _, DE = wg_bf.shape
    full = lambda shape: pl.BlockSpec(shape, lambda e: (0,) * len(shape))
    return pl.pallas_call(
        _sample_tail_kernel,
        grid=(E,),
        in_specs=[full((R, D)), full((R, POOL_WIDTH)), full((ATTN_WIDTH, R)), full(w_out.shape),
                  full((1, D)), full(w_router.shape),
                  pl.BlockSpec((None, D, DE), lambda e: (e, 0, 0)),
                  pl.BlockSpec((None, D, DE), lambda e: (e, 0, 0)),
                  pl.BlockSpec((None, DE, D), lambda e: (e, 0, 0)),
                  full((1, D))],
        out_specs=full((R, D)),
        out_shape=jax.ShapeDtypeStruct((R, D), F32),
        scratch_shapes=[pltpu.VMEM((R, D), F32), pltpu.VMEM((R, D), F32), pltpu.VMEM((R, LANES), F32),
                        pltpu.VMEM((R, D), F32)],
        compiler_params=_cparams(("arbitrary",)),
        name="sample_tail",
    )(x_pad, po, ao_t, w_out, g_ffn, w_router, wg_bf, wu_bf, wd_bf, g_final)


def _tile(n, pref):
    while n % pref:
        pref //= 2
    return pref


def kernel(x_prompt, x_sample, cache_k, cache_v, state_pool, page_table, norm_mix, w_in, w_pool, pool_scale,
           w_out, norm_ffn, w_router_group, w_router_expert, w_gate, w_up, w_down, norm_final):
    B, S, D = x_prompt.shape
    DB, DS, _ = x_sample.shape
    depth, n_phys, page = cache_k.shape[:3]
    assert depth == 1 and DS == 1, "single layer, one new token per sequence"
    n_pages = page_table.shape[1]
    past_len = n_pages * page
    assert S % MOBA_BLOCK == 0 and past_len % MOBA_BLOCK == 0 and MOBA_BLOCK % page == 0
    assert past_len // MOBA_BLOCK >= MOBA_TOPK and DB <= LANES
    T = B * S

    g_mix = norm_mix[0][None, :]
    g_ffn = norm_ffn[0][None, :]
    g_final = norm_final[None, :]
    ps = pool_scale[0][None, :]
    w_router = jnp.concatenate(
        [w_router_expert[0], w_router_group[0],
         jnp.zeros((D, LANES - N_EXPERTS - N_EXPERT_GROUPS), F32)], axis=1)
    wg_bf, wu_bf, wd_bf = w_gate[0].astype(BF16), w_up[0].astype(BF16), w_down[0].astype(BF16)

    xs = x_sample.reshape(DB, D)
    x_pad = jnp.pad(xs, ((0, LANES - DB), (0, 0)))
    pos_s = jnp.full((1,), past_len, jnp.int32)
    sp_t = jnp.swapaxes(state_pool[0], 0, 1)
    u_s, k_s, v_s, po_s, q_t, k_t, v_t = _sample_inproj(x_pad, g_mix, w_in[0], pos_s, sp_t, w_pool[0], ps)
    cache_kt = jnp.transpose(cache_k[0], (0, 2, 3, 1)).reshape(n_phys, ATTN_WIDTH, page)
    cache_vt = jnp.transpose(cache_v[0], (0, 2, 3, 1)).reshape(n_phys, ATTN_WIDTH, page)
    pt_flat = page_table.reshape(-1).astype(jnp.int32)

    ts = _tile(S, 512)
    kt_f, vt_f, q_bf, kt_bf, vt_bf, po, ksum, u_last = _inproj(
        x_prompt, g_mix, w_in[0], w_pool[0].astype(BF16), ps, ts)
    ksum = ksum[:, :, :ts // MOBA_BLOCK].reshape(B, S // MOBA_BLOCK, ATTN_WIDTH)
    ao, top = _moba_prompt(q_bf, kt_bf, vt_bf, ksum, pt_flat, q_t, cache_kt, DB, n_pages)
    h2, xg, counts = _outproj(x_prompt.reshape(T, D), po.reshape(T, POOL_WIDTH), ao.reshape(T, ATTN_WIDTH),
                              w_out[0].astype(BF16), g_ffn, w_router.astype(BF16), _tile(T, 512))
    dest, tile_e1, tile_e2, n_used, n_rows = _moe_layout(xg, D, counts)
    dest3d = dest.reshape(T // ROW_TILE, 1, ROW_TILE)
    x_grouped = _dispatch(dest3d, xg, n_rows)
    moe_sorted = _moe(tile_e1, tile_e2, n_used, x_grouped, wg_bf, wu_bf, wd_bf)
    y_prompt = _combine(dest3d, h2, g_final, moe_sorted).reshape(B, S, D)
    k_prompt = jnp.transpose(kt_f.reshape(B, N_HEADS, HEAD_DIM, S), (0, 3, 1, 2))[None]
    v_prompt = jnp.transpose(vt_f.reshape(B, N_HEADS, HEAD_DIM, S), (0, 3, 1, 2))[None]
    pool_prompt = u_last[None, :, HALO - POOL_HIST:, :]

    top_flat = top[:, :, :MOBA_TOPK].reshape(-1)
    ao_t = _sample_attn(pt_flat, top_flat, q_t, k_t, v_t, cache_kt, cache_vt, DB, n_pages)
    y_pad = _sample_tail(x_pad, po_s, ao_t, w_out[0], g_ffn, w_router, wg_bf, wu_bf, wd_bf, g_final)
    y_sample = y_pad[:DB].reshape(DB, 1, D)
    k_sample = k_s[:DB].reshape(1, DB, 1, N_HEADS, HEAD_DIM)
    v_sample = v_s[:DB].reshape(1, DB, 1, N_HEADS, HEAD_DIM)
    pool_sample = jnp.concatenate([state_pool[0][:, 1:], u_s[:DB, None, :]], axis=1)[None]
    return (y_prompt, y_sample, k_prompt, v_prompt, pool_prompt, k_sample, v_sample, pool_sample)
```

```python
import functools

import jax
import jax.numpy as jnp
import numpy as np
from jax import lax
from jax.experimental import pallas as pl
from jax.experimental.pallas import tpu as pltpu

F32 = jnp.float32
BF16 = jnp.bfloat16
HIGHEST = lax.Precision.HIGHEST

POOL_WINDOWS = (2, 4, 8, 16)
POOL_GROUP = 128
POOL_WIDTH = POOL_GROUP * len(POOL_WINDOWS)
POOL_HIST = max(POOL_WINDOWS) - 1
HALO = POOL_HIST + 1
HEAD_DIM = 64
N_HEADS = 8
ATTN_WIDTH = N_HEADS * HEAD_DIM
ROT_DIM = HEAD_DIM // 4
ROT_HALF = ROT_DIM // 2
ROPE_THETA = 500000.0
MOBA_BLOCK = 256
MOBA_TOPK = 3
N_EXPERT_GROUPS = 4
EXPERTS_PER_GROUP = 4
N_EXPERTS = N_EXPERT_GROUPS * EXPERTS_PER_GROUP
RMS_EPS = 1e-6

LANES = 128
SUBLANES = 8
HEADS_PER_TILE = LANES // HEAD_DIM
VMEM_LIMIT = 52 * 1024 * 1024

NEG_INF = float("-inf")
NEG_BIG = -1e30
BIG_IDX = 1 << 20
NT = (((1,), (1,)), ((), ()))
Q_SCALE = HEAD_DIM ** -0.5 * float(np.log2(np.e))


def _cparams(sem):
    return pltpu.CompilerParams(dimension_semantics=sem, vmem_limit_bytes=VMEM_LIMIT)


def _rms(x, g):
    r = lax.rsqrt(jnp.mean(x * x, axis=-1, keepdims=True) + RMS_EPS)
    return x * r * g


def _rope_angles(pos):
    inv = ROPE_THETA ** (-(jnp.arange(0, ROT_DIM, 2, dtype=F32) / ROT_DIM))
    ang = pos.astype(F32)[:, None] * inv[None, :]
    return jnp.cos(ang), jnp.sin(ang)


def _rope_tables(pos):
    cos, sin = _rope_angles(pos)
    d = np.arange(LANES) % HEAD_DIM
    fi = d % ROT_HALF
    cos_l, sin_l = cos[:, fi], sin[:, fi]
    c = jnp.where(d < ROT_DIM, cos_l, 1.0)
    s_lo = jnp.where(d < ROT_HALF, -sin_l, 0.0)
    s_hi = jnp.where((d >= ROT_HALF) & (d < ROT_DIM), sin_l, 0.0)
    return c.astype(F32), s_lo.astype(F32), s_hi.astype(F32)


def _rotate(x, c, s_lo, s_hi):
    outs = []
    for j in range(x.shape[1] // LANES):
        t = x[:, j * LANES:(j + 1) * LANES]
        up = pltpu.roll(t, LANES - ROT_HALF, axis=1)
        dn = pltpu.roll(t, ROT_HALF, axis=1)
        outs.append(t * c + up * s_lo + dn * s_hi)
    return jnp.concatenate(outs, axis=1)


def _rotate_t(xt, cos_t, sin_t):
    pieces = []
    for h in range(N_HEADS):
        r0 = h * HEAD_DIM
        x1 = xt[r0:r0 + ROT_HALF]
        x2 = xt[r0 + ROT_HALF:r0 + ROT_DIM]
        pieces += [x1 * cos_t - x2 * sin_t, x2 * cos_t + x1 * sin_t, xt[r0 + ROT_DIM:r0 + HEAD_DIM]]
    return jnp.concatenate(pieces, axis=0)


def _inproj_kernel(x_ref, g_ref, wuq_ref, wkv_ref, c_ref, slo_ref, shi_ref, ct_ref, st_ref, ind_ref,
                   wp_ref, ps_ref,
                   kf_ref, vf_ref, qb_ref, kb_ref, vb_ref, po_ref, ksum_ref, ulast_ref, ext_ref):
    s = pl.program_id(1)
    ts = x_ref.shape[0]
    xn = _rms(x_ref[...], g_ref[...]).astype(BF16)
    u = jnp.dot(xn, wuq_ref[:, 0:POOL_WIDTH], preferred_element_type=F32)
    q = jnp.dot(xn, wuq_ref[:, POOL_WIDTH:], preferred_element_type=F32)
    kt = lax.dot_general(wkv_ref[0:ATTN_WIDTH, :], xn, NT, preferred_element_type=F32)
    vt = lax.dot_general(wkv_ref[ATTN_WIDTH:, :], xn, NT, preferred_element_type=F32)
    q = _rotate(q, c_ref[...], slo_ref[...], shi_ref[...])
    kt = _rotate_t(kt, ct_ref[...], st_ref[...])
    kf_ref[...] = kt
    vf_ref[...] = vt
    qb_ref[...] = (q * Q_SCALE).astype(BF16)
    kb_ref[...] = kt.astype(BF16)
    ones = jnp.ones((HEAD_DIM, ts), BF16)
    vtb = vt.astype(BF16)
    vb_ref[...] = jnp.concatenate(
        [piece for h in range(N_HEADS) for piece in (vtb[h * HEAD_DIM:(h + 1) * HEAD_DIM], ones)], axis=0)
    ksum_ref[...] = lax.dot_general(ind_ref[...], kt, NT, preferred_element_type=F32, precision=HIGHEST)

    @pl.when(s == 0)
    def _():
        ext_ref[0:HALO, :] = jnp.zeros((HALO, POOL_WIDTH), F32)

    ext_ref[HALO:HALO + ts, :] = u
    pos1 = s * ts + lax.broadcasted_iota(jnp.int32, (ts, POOL_GROUP), 0) + 1
    outs = []
    for g, w in enumerate(POOL_WINDOWS):
        sl = slice(g * POOL_GROUP, (g + 1) * POOL_GROUP)
        ug = u[:, sl]
        acc = ug
        for j in range(1, w):
            acc = acc + ext_ref[HALO - j:HALO - j + ts, sl]
        cnt = jnp.minimum(pos1, w).astype(F32)
        d = (acc / cnt - ug).astype(BF16)
        outs.append(jnp.dot(d, wp_ref[g], preferred_element_type=F32))
    y = jnp.concatenate(outs, axis=1) * ps_ref[...]
    po_ref[...] = y.astype(BF16)
    tail = u[ts - HALO:ts, :]
    ext_ref[0:HALO, :] = tail

    @pl.when(s == pl.num_programs(1) - 1)
    def _():
        ulast_ref[...] = tail


def _inproj(x, g_mix, w_in, w_pool_bf, pool_scale, ts):
    B, S, D = x.shape
    ns = S // ts
    pos = jnp.arange(S, dtype=jnp.int32)
    c, s_lo, s_hi = _rope_tables(pos)
    cos, sin = _rope_angles(pos)
    cos_t, sin_t = cos.T, sin.T
    w_uq = w_in[:, :POOL_WIDTH + ATTN_WIDTH].astype(BF16)
    w_kv_t = w_in[:, POOL_WIDTH + ATTN_WIDTH:].T.astype(BF16)
    ind = (np.arange(ts)[None, :] // MOBA_BLOCK == np.arange(SUBLANES)[:, None]).astype(np.float32)
    row = lambda b, s: (b, s, 0)
    col = lambda b, s: (b, 0, s)
    tab = pl.BlockSpec((ts, LANES), lambda b, s: (s, 0))
    tab_t = pl.BlockSpec((ROT_HALF, ts), lambda b, s: (0, s))
    full2 = lambda shape: pl.BlockSpec(shape, lambda b, s: (0, 0))
    act = lambda: pl.BlockSpec((None, ts, ATTN_WIDTH), row)
    act_t = lambda: pl.BlockSpec((None, ATTN_WIDTH, ts), col)
    out_shapes = (
        jax.ShapeDtypeStruct((B, ATTN_WIDTH, S), F32),
        jax.ShapeDtypeStruct((B, ATTN_WIDTH, S), F32),
        jax.ShapeDtypeStruct((B, S, ATTN_WIDTH), BF16),
        jax.ShapeDtypeStruct((B, ATTN_WIDTH, S), BF16),
        jax.ShapeDtypeStruct((B, 2 * ATTN_WIDTH, S), BF16),
        jax.ShapeDtypeStruct((B, S, POOL_WIDTH), BF16),
        jax.ShapeDtypeStruct((B, ns, SUBLANES, ATTN_WIDTH), F32),
        jax.ShapeDtypeStruct((B, HALO, POOL_WIDTH), F32),
    )
    return pl.pallas_call(
        _inproj_kernel,
        grid=(B, ns),
        in_specs=[
            pl.BlockSpec((None, ts, D), row),
            full2((1, D)),
            full2(w_uq.shape),
            full2(w_kv_t.shape),
            tab, tab, tab, tab_t, tab_t,
            full2(ind.shape),
            pl.BlockSpec(w_pool_bf.shape, lambda b, s: (0, 0, 0)),
            full2((1, POOL_WIDTH)),
        ],
        out_specs=(act_t(), act_t(), act(), act_t(), pl.BlockSpec((None, 2 * ATTN_WIDTH, ts), col), act(),
                   pl.BlockSpec((None, None, SUBLANES, ATTN_WIDTH), lambda b, s: (b, s, 0, 0)),
                   pl.BlockSpec((None, HALO, POOL_WIDTH), lambda b, s: (b, 0, 0))),
        out_shape=out_shapes,
        scratch_shapes=[pltpu.VMEM((HALO + ts, POOL_WIDTH), F32)],
        compiler_params=_cparams(("arbitrary", "arbitrary")),
        name="prompt_inproj",
    )(x, g_mix, w_uq, w_kv_t, c, s_lo, s_hi, cos_t, sin_t, jnp.asarray(ind), w_pool_bf, pool_scale)


def _moba_kernel(pt_ref, q_ref, k_ref, v_ref, e_ref, ksum_ref, qt_ref, cache_ref, o_ref, top_ref,
                 s_sc, buf_ref, sem_ref, qb_sc, g_sc, *, chunk, scan_nch, scan_nblk):
    n1, n2 = pl.num_programs(1), pl.num_programs(2)
    step = (pl.program_id(0) * n1 + pl.program_id(1)) * n2 + pl.program_id(2)
    _scan_step(step, pl.num_programs(0) * n1 * n2, scan_nch, pt_ref, qt_ref, cache_ref, top_ref,
               buf_ref, sem_ref, qb_sc, g_sc, scan_nblk)

    c = pl.program_id(2)
    tq = q_ref.shape[0]
    nb = ksum_ref.shape[0]
    blocks_per_chunk = chunk // MOBA_BLOCK
    n_tiles = chunk // LANES
    q = q_ref[...]
    kmean = (ksum_ref[...] * (1.0 / MOBA_BLOCK)).astype(BF16)
    lane = lax.broadcasted_iota(jnp.int32, (tq, LANES), 1)
    blk_iota = lax.broadcasted_iota(jnp.int32, (nb, tq), 0)

    qp = []
    for h in range(HEADS_PER_TILE):
        in_head = (lane >= h * HEAD_DIM) & (lane < (h + 1) * HEAD_DIM)
        qm = jnp.where(in_head, q, jnp.zeros_like(q))
        g = lax.dot_general(kmean, qm, NT, preferred_element_type=F32)
        g = jnp.where(blk_iota < c, g, NEG_INF)
        sel = blk_iota == c
        for t in range(MOBA_TOPK):
            m = jnp.max(g, axis=0, keepdims=True)
            idx = jnp.min(jnp.where(g == m, blk_iota, BIG_IDX), axis=0, keepdims=True)
            hit = blk_iota == idx
            sel = sel | (hit & (t < c))
            g = jnp.where(hit, NEG_INF, g)
        bias_t = jnp.where(sel, 0.0, NEG_BIG)
        bias_t = jnp.concatenate([bias_t, jnp.zeros((LANES - nb, tq), F32)], axis=0)
        bias = jnp.transpose(bias_t).astype(BF16)
        qp.append(jnp.concatenate([qm, bias], axis=1))

    def keys(off):
        return jnp.concatenate([k_ref[:, pl.ds(off, chunk)], e_ref[:, pl.ds(off, chunk)]], axis=0)

    def fold(fn, acc, x):
        for j in range(n_tiles):
            acc = fn(acc, x[:, j * LANES:(j + 1) * LANES])
        return acc

    n_past = lax.shift_right_logical(c, int(np.log2(blocks_per_chunk)))

    def pass1(i, mrun):
        off = pl.multiple_of(i * chunk, chunk)
        kk = keys(off)
        new = []
        for h in range(HEADS_PER_TILE):
            s = jnp.dot(qp[h], kk, preferred_element_type=F32)
            s_sc[h, :, pl.ds(off, chunk)] = s
            new.append(fold(jnp.maximum, mrun[h], s))
        return tuple(new)

    mrun = lax.fori_loop(0, n_past, pass1, tuple(jnp.full((tq, LANES), NEG_BIG, F32)
                                                 for _ in range(HEADS_PER_TILE)))
    off = pl.multiple_of(n_past * chunk, chunk)
    kk = keys(off)
    kpos = off + lax.broadcasted_iota(jnp.int32, (tq, chunk), 1)
    qpos = c * MOBA_BLOCK + lax.broadcasted_iota(jnp.int32, (tq, chunk), 0)
    m_b = []
    for h in range(HEADS_PER_TILE):
        s = jnp.dot(qp[h], kk, preferred_element_type=F32)
        s = jnp.where(kpos <= qpos, s, NEG_BIG)
        s_sc[h, :, pl.ds(off, chunk)] = s
        m = jnp.max(fold(jnp.maximum, mrun[h], s), axis=1, keepdims=True)
        m_b.append(jnp.broadcast_to(m, (tq, LANES)))

    def pass2(i, accs):
        off = pl.multiple_of(i * chunk, chunk)
        new = []
        for h in range(HEADS_PER_TILE):
            s = s_sc[h, :, pl.ds(off, chunk)]
            pb = jnp.concatenate([jnp.exp2(s[:, j * LANES:(j + 1) * LANES] - m_b[h]).astype(BF16)
                                  for j in range(n_tiles)], axis=1)
            vv = v_ref[h * LANES:(h + 1) * LANES, pl.ds(off, chunk)]
            new.append(accs[h] + lax.dot_general(pb, vv, NT, preferred_element_type=F32))
        return tuple(new)

    acc0, acc1 = lax.fori_loop(0, n_past + 1, pass2, (jnp.zeros((tq, LANES), F32),) * HEADS_PER_TILE)
    out = jnp.where(lane < HEAD_DIM, acc0 / pltpu.roll(acc0, HEAD_DIM, axis=1),
                    pltpu.roll(acc1, HEAD_DIM, axis=1) / acc1)
    o_ref[...] = out.astype(o_ref.dtype)


def _moba_prompt(q_bf, kt_bf, vt_bf, ksum, page_table_flat, q_t, cache_kt, DB, n_pages):
    B, S, W = q_bf.shape
    nb = S // MOBA_BLOCK
    assert nb <= LANES and HEADS_PER_TILE == 2
    ntile = W // LANES
    chunk = min(S, 4 * MOBA_BLOCK)
    ind = (np.arange(S)[None, :] // MOBA_BLOCK == np.arange(LANES)[:, None]).astype(np.float32)
    _, cw, page = cache_kt.shape
    nsteps = B * ntile * nb
    pages_per_step, rem = divmod(DB * n_pages, nsteps)
    scan_nch, rem2 = divmod(n_pages, max(pages_per_step, 1))
    assert rem == 0 and rem2 == 0 and pages_per_step % (MOBA_BLOCK // page) == 0, "cache sweep does not tile"
    scan_nblk = n_pages * page // MOBA_BLOCK
    assert scan_nblk <= LANES

    blk = pl.BlockSpec((None, MOBA_BLOCK, LANES), lambda b, t, c, pt: (b, c, t))
    seq = pl.BlockSpec((None, LANES, S), lambda b, t, c, pt: (b, t, 0))
    seq_v = pl.BlockSpec((None, HEADS_PER_TILE * LANES, S), lambda b, t, c, pt: (b, t, 0))
    const = lambda shape: pl.BlockSpec(shape, lambda b, t, c, pt: (0, 0))
    sample_of_step = lambda b, t, c, pt: (((b * ntile + t) * nb + c) // scan_nch, 0, 0)
    return pl.pallas_call(
        functools.partial(_moba_kernel, chunk=chunk, scan_nch=scan_nch, scan_nblk=scan_nblk),
        grid_spec=pltpu.PrefetchScalarGridSpec(
            num_scalar_prefetch=1,
            grid=(B, ntile, nb),
            in_specs=[blk, seq, seq_v, const((LANES, S)),
                      pl.BlockSpec((None, nb, LANES), lambda b, t, c, pt: (b, 0, t)),
                      const(q_t.shape), pl.BlockSpec(memory_space=pl.ANY)],
            out_specs=(blk, pl.BlockSpec((None, N_HEADS, LANES), sample_of_step)),
            scratch_shapes=[pltpu.VMEM((HEADS_PER_TILE, MOBA_BLOCK, S), F32),
                            pltpu.VMEM((2, pages_per_step, cw, page), F32), pltpu.SemaphoreType.DMA((2,)),
                            pltpu.VMEM(q_t.shape, F32), pltpu.VMEM((N_HEADS * SUBLANES, LANES), F32)],
        ),
        out_shape=(jax.ShapeDtypeStruct((B, S, W), BF16), jax.ShapeDtypeStruct((DB, N_HEADS, LANES), jnp.int32)),
        compiler_params=_cparams(("arbitrary", "arbitrary", "arbitrary")),
        name="prompt_moba",
    )(page_table_flat, q_bf, kt_bf, vt_bf, jnp.asarray(ind, BF16), ksum, q_t, cache_kt)


def _route(logits):
    lane = lax.broadcasted_iota(jnp.int32, logits.shape, 1)
    is_grp = (lane >= N_EXPERTS) & (lane < N_EXPERTS + N_EXPERT_GROUPS)
    gl = jnp.where(is_grp, logits, NEG_INF)
    ge = jnp.exp(gl - jnp.max(gl, axis=1, keepdims=True))
    pg = ge / jnp.sum(ge, axis=1, keepdims=True)
    pg_sel = jnp.max(pg, axis=1, keepdims=True)
    g_sel = jnp.min(jnp.where(is_grp & (pg == pg_sel), lane - N_EXPERTS, BIG_IDX), axis=1, keepdims=True)
    in_grp = (lane < N_EXPERTS) & (jnp.right_shift(lane, 2) == g_sel)
    el = jnp.where(in_grp, logits, NEG_INF)
    ee = jnp.exp(el - jnp.max(el, axis=1, keepdims=True))
    pe = jnp.where(in_grp, ee / jnp.sum(ee, axis=1, keepdims=True), -1.0)
    p1 = jnp.max(pe, axis=1, keepdims=True)
    i1 = jnp.min(jnp.where(pe == p1, lane, BIG_IDX), axis=1, keepdims=True)
    pe2 = jnp.where(lane == i1, -1.0, pe)
    p2 = jnp.max(pe2, axis=1, keepdims=True)
    i2 = jnp.min(jnp.where(pe2 == p2, lane, BIG_IDX), axis=1, keepdims=True)
    tot = p1 + p2
    gate = (jnp.where(lane == i1, p1 / tot * pg_sel, 0.0)
            + jnp.where(lane == i2, p2 / tot * pg_sel, 0.0))
    return gate, (g_sel, i1, i2)


PAIRS_PER_GROUP = EXPERTS_PER_GROUP * (EXPERTS_PER_GROUP - 1) // 2
N_CLASSES = N_EXPERT_GROUPS * PAIRS_PER_GROUP
_PAIRS = [(a, b) for a in range(EXPERTS_PER_GROUP) for b in range(a + 1, EXPERTS_PER_GROUP)]
CLASS_EXPERTS = np.array([[g * EXPERTS_PER_GROUP + a, g * EXPERTS_PER_GROUP + b]
                          for g in range(N_EXPERT_GROUPS) for a, b in _PAIRS], np.int32)


def _route_class(g_sel, i1, i2):
    assert EXPERTS_PER_GROUP == 4
    base = g_sel * EXPERTS_PER_GROUP
    a = jnp.minimum(i1, i2) - base
    b = jnp.maximum(i1, i2) - base
    pair = jnp.right_shift(a * (5 - a), 1) + b - 1
    return g_sel * PAIRS_PER_GROUP + pair


def _gate_column(gate, e):
    lane = lax.broadcasted_iota(jnp.int32, gate.shape, 1)
    return jnp.sum(jnp.where(lane == e, gate, 0.0), axis=1, keepdims=True)


def _swiglu(x_bf, wg, wu, wd):
    hg = jnp.dot(x_bf, wg, preferred_element_type=F32)
    hu = jnp.dot(x_bf, wu, preferred_element_type=F32)
    h = hg * (1.0 / (1.0 + jnp.exp(-hg))) * hu
    return jnp.dot(h.astype(BF16), wd, preferred_element_type=F32)


CLASS_LANE = N_EXPERTS + N_EXPERT_GROUPS
RANK_LANE = CLASS_LANE + 1


def _outproj_kernel(x_ref, po_ref, ao_ref, w_ref, g_ref, wr_ref, tri_ref, h_ref, xg_ref, cnt_ref, meta_ref):
    i = pl.program_id(0)
    d = x_ref.shape[1]
    mix = (jnp.dot(po_ref[...], w_ref[0:POOL_WIDTH, :], preferred_element_type=F32)
           + jnp.dot(ao_ref[...], w_ref[POOL_WIDTH:, :], preferred_element_type=F32))
    h = x_ref[...] + mix
    h_ref[...] = h
    xn = _rms(h, g_ref[...])
    gate, picks = _route(jnp.dot(xn.astype(BF16), wr_ref[...], preferred_element_type=F32))
    cls = _route_class(*picks)

    @pl.when(i == 0)
    def _():
        cnt_ref[...] = jnp.zeros_like(cnt_ref)

    lane = lax.broadcasted_iota(jnp.int32, gate.shape, 1)
    onehot = jnp.where(lane == cls, 1.0, 0.0)
    before = jnp.dot(tri_ref[...], onehot.astype(BF16), preferred_element_type=F32) + cnt_ref[0:1, :]
    rank = jnp.sum(onehot * before, axis=1, keepdims=True)
    cnt_ref[...] = cnt_ref[...] + jnp.sum(onehot, axis=0, keepdims=True)
    info = jnp.where(lane == CLASS_LANE, cls.astype(F32), jnp.where(lane == RANK_LANE, rank, gate))
    xg_ref[:, 0:d] = xn
    xg_ref[:, d:] = info
    meta_ref[...] = jnp.transpose(info)[META_ROW0:META_ROW0 + SUBLANES, :]


META_ROW0 = CLASS_LANE // SUBLANES * SUBLANES


def _outproj(x2d, po2d, ao2d, w_out_bf, g_ffn, w_router_bf, tm):
    T, D = x2d.shape
    row = lambda i: (i, 0)
    full = lambda shape: pl.BlockSpec(shape, lambda i: (0, 0))
    tri = np.tril(np.ones((tm, tm), np.float32), -1)
    return pl.pallas_call(
        _outproj_kernel,
        grid=(T // tm,),
        in_specs=[pl.BlockSpec((tm, D), row), pl.BlockSpec((tm, POOL_WIDTH), row),
                  pl.BlockSpec((tm, ATTN_WIDTH), row), full(w_out_bf.shape), full((1, D)),
                  full(w_router_bf.shape), full((tm, tm))],
        out_specs=(pl.BlockSpec((tm, D), row), pl.BlockSpec((tm, D + LANES), row), full((SUBLANES, LANES)),
                   pl.BlockSpec((None, SUBLANES, tm), lambda i: (i, 0, 0))),
        out_shape=(jax.ShapeDtypeStruct((T, D), F32), jax.ShapeDtypeStruct((T, D + LANES), F32),
                   jax.ShapeDtypeStruct((SUBLANES, LANES), F32),
                   jax.ShapeDtypeStruct((T // tm, SUBLANES, tm), F32)),
        compiler_params=_cparams(("arbitrary",)),
        name="prompt_outproj",
    )(x2d, po2d, ao2d, w_out_bf, g_ffn, w_router_bf, jnp.asarray(tri, BF16))


MOE_TILE = 512
ROW_TILE = 512
DMA_UNROLL = 8


def _dispatch_kernel(dest_ref, xg_ref, zeros_ref, xs_ref, sem_ref):
    del zeros_ref
    n = xg_ref.shape[0]

    def start(r, carry):
        pltpu.make_async_copy(xg_ref.at[pl.ds(r, 1)], xs_ref.at[pl.ds(dest_ref[0, r], 1)], sem_ref.at[0]).start()
        return carry

    lax.fori_loop(0, n, start, 0, unroll=DMA_UNROLL)
    pltpu.make_async_copy(xg_ref, xs_ref.at[pl.ds(0, n)], sem_ref.at[0]).wait()


def _dispatch(dest3d, xg, n_rows):
    T, W = xg.shape
    tm = dest3d.shape[2]
    return pl.pallas_call(
        _dispatch_kernel,
        grid=(T // tm,),
        in_specs=[pl.BlockSpec((None, 1, tm), lambda i: (i, 0, 0), memory_space=pltpu.SMEM),
                  pl.BlockSpec((tm, W), lambda i: (i, 0)),
                  pl.BlockSpec(memory_space=pl.ANY)],
        out_specs=pl.BlockSpec(memory_space=pl.ANY),
        out_shape=jax.ShapeDtypeStruct((n_rows, W), F32),
        scratch_shapes=[pltpu.SemaphoreType.DMA((1,))],
        input_output_aliases={2: 0},
        compiler_params=_cparams(("arbitrary",)),
        name="moe_dispatch",
    )(dest3d, xg, jnp.zeros((n_rows, W), F32))


def _moe_kernel(e1_ref, e2_ref, nused_ref, xs_ref, wg1_ref, wu1_ref, wd1_ref, wg2_ref, wu2_ref, wd2_ref, o_ref):
    i = pl.program_id(0)
    d = o_ref.shape[1]

    @pl.when(i < nused_ref[0])
    def _():
        x = xs_ref[:, 0:d].astype(BF16)
        gate = xs_ref[:, d:]
        o_ref[...] = (_gate_column(gate, e1_ref[i]) * _swiglu(x, wg1_ref[...], wu1_ref[...], wd1_ref[...])
                      + _gate_column(gate, e2_ref[i]) * _swiglu(x, wg2_ref[...], wu2_ref[...], wd2_ref[...]))

    @pl.when(i >= nused_ref[0])
    def _():
        o_ref[...] = jnp.zeros_like(o_ref)


def _moe(tile_e1, tile_e2, n_used, xs, wg_bf, wu_bf, wd_bf):
    n_rows, W = xs.shape
    E, D, DE = wg_bf.shape
    tile = lambda i, e1, e2, nu: (jnp.maximum(jnp.minimum(i, nu[0] - 1), 0), 0)
    first = lambda shape: pl.BlockSpec((None,) + shape, lambda i, e1, e2, nu: (e1[i], 0, 0))
    second = lambda shape: pl.BlockSpec((None,) + shape, lambda i, e1, e2, nu: (e2[i], 0, 0))
    return pl.pallas_call(
        _moe_kernel,
        grid_spec=pltpu.PrefetchScalarGridSpec(
            num_scalar_prefetch=3,
            grid=(n_rows // MOE_TILE,),
            in_specs=[pl.BlockSpec((MOE_TILE, W), tile),
                      first((D, DE)), first((D, DE)), first((DE, D)),
                      second((D, DE)), second((D, DE)), second((DE, D))],
            out_specs=pl.BlockSpec((MOE_TILE, D), lambda i, e1, e2, nu: (i, 0)),
        ),
        out_shape=jax.ShapeDtypeStruct((n_rows, D), F32),
        compiler_params=_cparams(("arbitrary",)),
        name="moe_experts",
    )(tile_e1, tile_e2, n_used, xs, wg_bf, wu_bf, wd_bf, wg_bf, wu_bf, wd_bf)


def _combine_copies_start(dest_ref, src_ref, buf_ref, sem_ref, slot):
    def start(r, carry):
        pltpu.make_async_copy(src_ref.at[pl.ds(dest_ref[0, r], 1)], buf_ref.at[slot, pl.ds(r, 1)],
                              sem_ref.at[slot]).start()
        return carry

    lax.fori_loop(0, buf_ref.shape[1], start, 0, unroll=DMA_UNROLL)


def _combine_kernel(dest_ref, dest_next_ref, h_ref, gf_ref, moe_ref, y_ref, buf_ref, sem_ref):
    i = pl.program_id(0)
    n = buf_ref.shape[1]
    slot = i % 2

    @pl.when(i == 0)
    def _():
        _combine_copies_start(dest_ref, moe_ref, buf_ref, sem_ref, slot)

    @pl.when(i + 1 < pl.num_programs(0))
    def _():
        _combine_copies_start(dest_next_ref, moe_ref, buf_ref, sem_ref, 1 - slot)

    pltpu.make_async_copy(moe_ref.at[pl.ds(0, n)], buf_ref.at[slot], sem_ref.at[slot]).wait()
    y_ref[...] = _rms(h_ref[...] + buf_ref[slot], gf_ref[...])


def _combine(dest3d, h2d, g_final, moe_sorted):
    T, D = h2d.shape
    nt, _, tm = dest3d.shape
    return pl.pallas_call(
        _combine_kernel,
        grid=(nt,),
        in_specs=[pl.BlockSpec((None, 1, tm), lambda i: (i, 0, 0), memory_space=pltpu.SMEM),
                  pl.BlockSpec((None, 1, tm), lambda i: (jnp.minimum(i + 1, nt - 1), 0, 0),
                               memory_space=pltpu.SMEM),
                  pl.BlockSpec((tm, D), lambda i: (i, 0)),
                  pl.BlockSpec((1, D), lambda i: (0, 0)),
                  pl.BlockSpec(memory_space=pl.ANY)],
        out_specs=pl.BlockSpec((tm, D), lambda i: (i, 0)),
        out_shape=jax.ShapeDtypeStruct((T, D), F32),
        scratch_shapes=[pltpu.VMEM((2, tm, D), F32), pltpu.SemaphoreType.DMA((2,))],
        compiler_params=_cparams(("arbitrary",)),
        name="moe_combine",
    )(dest3d, dest3d, h2d, g_final, moe_sorted)


def _moe_layout(meta, counts):
    cls = meta[:, CLASS_LANE - META_ROW0, :].reshape(-1).astype(jnp.int32)
    rank = meta[:, RANK_LANE - META_ROW0, :].reshape(-1).astype(jnp.int32)
    cnt = counts[0, :N_CLASSES].astype(jnp.int32)
    tiles = (cnt + MOE_TILE - 1) // MOE_TILE
    tile_end = jnp.cumsum(tiles)
    row_start = (tile_end - tiles) * MOE_TILE
    dest = rank
    for c in range(N_CLASSES):
        dest = dest + jnp.where(cls == c, row_start[c], 0)
    max_tiles = cls.shape[0] // MOE_TILE + N_CLASSES
    tile_ids = jnp.arange(max_tiles, dtype=jnp.int32)
    tile_class = jnp.minimum(jnp.sum((tile_ids[:, None] >= tile_end[None, :]).astype(jnp.int32), axis=1),
                             N_CLASSES - 1)
    experts = jnp.asarray(CLASS_EXPERTS)[tile_class]
    return dest, experts[:, 0], experts[:, 1], tile_end[-1:].astype(jnp.int32), max_tiles * MOE_TILE


def _sample_inproj_kernel(x_ref, g_ref, w_ref, c_ref, slo_ref, shi_ref, sp_ref, wp_ref, ps_ref,
                          u_ref, k_ref, v_ref, po_ref, qt_ref, kt_ref, vt_ref):
    xn = _rms(x_ref[...], g_ref[...])

    def proj(i):
        return jnp.dot(xn, w_ref[:, i * POOL_WIDTH:(i + 1) * POOL_WIDTH],
                       preferred_element_type=F32, precision=HIGHEST)

    u, q, k, v = proj(0), proj(1), proj(2), proj(3)
    c, s_lo, s_hi = c_ref[...], slo_ref[...], shi_ref[...]
    q = _rotate(q, c, s_lo, s_hi)
    k = _rotate(k, c, s_lo, s_hi)
    u_ref[...] = u
    k_ref[...] = k
    v_ref[...] = v
    qt_ref[...] = jnp.transpose(q)
    kt_ref[...] = jnp.transpose(k)
    vt_ref[...] = jnp.transpose(v)
    db = sp_ref.shape[1]
    outs = []
    for g, w in enumerate(POOL_WINDOWS):
        sl = slice(g * POOL_GROUP, (g + 1) * POOL_GROUP)
        ug = u[0:db, sl]
        acc = ug
        for j in range(1, w):
            acc = acc + sp_ref[POOL_HIST - j, :, sl]
        d = acc / float(w) - ug
        outs.append(jnp.dot(d, wp_ref[g], preferred_element_type=F32, precision=HIGHEST))
    po_ref[...] = jnp.zeros_like(po_ref)
    po_ref[0:db, :] = jnp.concatenate(outs, axis=1) * ps_ref[...]


def _sample_inproj(x_pad, g_mix, w_in, pos, sp_t, w_pool, pool_scale):
    R, D = x_pad.shape
    c, s_lo, s_hi = _rope_tables(pos)
    row = jax.ShapeDtypeStruct((R, POOL_WIDTH), F32)
    col = jax.ShapeDtypeStruct((POOL_WIDTH, R), F32)
    return pl.pallas_call(
        _sample_inproj_kernel,
        out_shape=(row, row, row, row, col, col, col),
        compiler_params=pltpu.CompilerParams(vmem_limit_bytes=VMEM_LIMIT),
        name="sample_inproj",
    )(x_pad, g_mix, w_in, c, s_lo, s_hi, sp_t, w_pool, pool_scale)


def _column(mat, b):
    lane = lax.broadcasted_iota(jnp.int32, mat.shape, 1)
    col = jnp.sum(jnp.where(lane == b, mat, 0.0), axis=1, keepdims=True)
    return jnp.broadcast_to(col, mat.shape)


def _scan_copies(pt_ref, cache_ref, buf_ref, sem_ref, step, slot):
    n = buf_ref.shape[1]
    return [pltpu.make_async_copy(cache_ref.at[pt_ref[step * n + i]], buf_ref.at[slot, i], sem_ref.at[slot])
            for i in range(n)]


def _scan_step(step, nsteps, nch, pt_ref, qt_ref, cache_ref, top_ref, buf_ref, sem_ref, qb_sc, g_sc, nblk):
    b = step // nch
    ch = step - b * nch
    slot = step % 2

    @pl.when(step == 0)
    def _():
        for cp in _scan_copies(pt_ref, cache_ref, buf_ref, sem_ref, step, slot):
            cp.start()

    @pl.when(step + 1 < nsteps)
    def _():
        for cp in _scan_copies(pt_ref, cache_ref, buf_ref, sem_ref, step + 1, 1 - slot):
            cp.start()

    @pl.when(ch == 0)
    def _():
        qb_sc[...] = _column(qt_ref[...], b)
        g_sc[...] = jnp.zeros_like(g_sc)

    for cp in _scan_copies(pt_ref, cache_ref, buf_ref, sem_ref, step, slot):
        cp.wait()
    qb = qb_sc[...]
    lane = lax.broadcasted_iota(jnp.int32, (N_HEADS, LANES), 1)
    pages_per_blk = MOBA_BLOCK // buf_ref.shape[3]
    blks_per_step = buf_ref.shape[1] // pages_per_blk
    lane8 = lax.broadcasted_iota(jnp.int32, (SUBLANES, LANES), 1)
    g = [g_sc[h * SUBLANES:(h + 1) * SUBLANES, :] for h in range(N_HEADS)]
    for j in range(blks_per_step):
        tot = buf_ref[slot, j * pages_per_blk]
        for i in range(1, pages_per_blk):
            tot = tot + buf_ref[slot, j * pages_per_blk + i]
        pr = tot * qb
        for h in range(N_HEADS):
            a = pr[h * HEAD_DIM:h * HEAD_DIM + SUBLANES]
            for r in range(h * HEAD_DIM + SUBLANES, (h + 1) * HEAD_DIM, SUBLANES):
                a = a + pr[r:r + SUBLANES]
            g[h] = jnp.where(lane8 == ch * blks_per_step + j, jnp.sum(a, axis=1, keepdims=True), g[h])
    for h in range(N_HEADS):
        g_sc[h * SUBLANES:(h + 1) * SUBLANES, :] = g[h]

    @pl.when(ch == nch - 1)
    def _():
        full = jnp.concatenate([jnp.sum(gh, axis=0, keepdims=True) for gh in g], axis=0)
        gg = jnp.where(lane < nblk, full, NEG_INF)
        top = jnp.zeros((N_HEADS, LANES), jnp.int32)
        for t in range(MOBA_TOPK):
            m = jnp.max(gg, axis=1, keepdims=True)
            idx = jnp.min(jnp.where(gg == m, lane, BIG_IDX), axis=1, keepdims=True)
            top = jnp.where(lane == t, idx, top)
            gg = jnp.where(lane == idx, NEG_INF, gg)
        top_ref[...] = top


def _attn_copies(pt_ref, top_ref, ck_ref, cv_ref, kbuf, vbuf, sem_ref, b, slot, n_pages, pages_per_blk):
    cps = []
    for h in range(N_HEADS):
        rows = pl.ds(h * HEAD_DIM, HEAD_DIM)
        for t in range(MOBA_TOPK):
            blk = top_ref[(b * N_HEADS + h) * MOBA_TOPK + t]
            for i in range(pages_per_blk):
                phys = pt_ref[b * n_pages + blk * pages_per_blk + i]
                j = t * pages_per_blk + i
                cps.append(pltpu.make_async_copy(ck_ref.at[phys, rows, :], kbuf.at[slot, h, j], sem_ref.at[slot]))
                cps.append(pltpu.make_async_copy(cv_ref.at[phys, rows, :], vbuf.at[slot, h, j], sem_ref.at[slot]))
    return cps


def _sample_attn_kernel(pt_ref, top_ref, qt_ref, kt_ref, vt_ref, ck_ref, cv_ref, o_ref, kbuf, vbuf, sem_ref,
                        *, n_pages):
    b = pl.program_id(0)
    nb = pl.num_programs(0)
    slot = b % 2
    n_sel = kbuf.shape[2]
    copies = functools.partial(_attn_copies, pt_ref, top_ref, ck_ref, cv_ref, kbuf, vbuf, sem_ref,
                               n_pages=n_pages, pages_per_blk=n_sel // MOBA_TOPK)

    @pl.when(b == 0)
    def _():
        o_ref[...] = jnp.zeros_like(o_ref)
        for cp in copies(b=b, slot=slot):
            cp.start()

    @pl.when(b + 1 < nb)
    def _():
        for cp in copies(b=b + 1, slot=1 - slot):
            cp.start()

    for cp in copies(b=b, slot=slot):
        cp.wait()

    scale = HEAD_DIM ** -0.5
    lane = lax.broadcasted_iota(jnp.int32, (HEAD_DIM, LANES), 1)
    for h in range(N_HEADS):
        rows = slice(h * HEAD_DIM, (h + 1) * HEAD_DIM)
        qb = _column(qt_ref[rows, :], b)
        kb = _column(kt_ref[rows, :], b)
        vb = _column(vt_ref[rows, :], b)
        s = [jnp.sum(kbuf[slot, h, j] * qb, axis=0, keepdims=True) * scale for j in range(n_sel)]
        s_own = jnp.sum(kb * qb, axis=0, keepdims=True) * scale
        m = s_own
        for sj in s:
            m = jnp.maximum(m, jnp.max(sj, axis=1, keepdims=True))
        p = [jnp.exp(sj - m) for sj in s]
        p_own = jnp.exp(s_own - m)
        l = p_own
        acc = jnp.zeros((HEAD_DIM, LANES), F32)
        for j, pj in enumerate(p):
            l = l + jnp.sum(pj, axis=1, keepdims=True)
            acc = acc + vbuf[slot, h, j] * pj
        col = (jnp.sum(acc, axis=1, keepdims=True) + p_own * vb) / l
        o_ref[rows, :] = jnp.where(lane == b, col, o_ref[rows, :])


def _sample_attn(page_table_flat, top_flat, q_t, k_t, v_t, cache_kt, cache_vt, DB, n_pages):
    _, W, page = cache_kt.shape
    n_sel = MOBA_TOPK * MOBA_BLOCK // page
    vm = lambda: pl.BlockSpec(q_t.shape, lambda b, pt, tp: (0, 0))
    hbm = lambda: pl.BlockSpec(memory_space=pl.ANY)
    return pl.pallas_call(
        functools.partial(_sample_attn_kernel, n_pages=n_pages),
        grid_spec=pltpu.PrefetchScalarGridSpec(
            num_scalar_prefetch=2,
            grid=(DB,),
            in_specs=[vm(), vm(), vm(), hbm(), hbm()],
            out_specs=vm(),
            scratch_shapes=[pltpu.VMEM((2, N_HEADS, n_sel, HEAD_DIM, page), F32),
                            pltpu.VMEM((2, N_HEADS, n_sel, HEAD_DIM, page), F32),
                            pltpu.SemaphoreType.DMA((2,))],
        ),
        out_shape=jax.ShapeDtypeStruct(q_t.shape, F32),
        compiler_params=_cparams(("arbitrary",)),
        name="sample_attn",
    )(page_table_flat, top_flat, q_t, k_t, v_t, cache_kt, cache_vt)


def _sample_tail_kernel(x_ref, po_ref, aot_ref, w_ref, g_ref, wr_ref, wg_ref, wu_ref, wd_ref, gf_ref,
                        y_ref, h_sc, xn_sc, gate_sc, acc_sc):
    e = pl.program_id(0)

    @pl.when(e == 0)
    def _():
        ao = jnp.transpose(aot_ref[...])
        mix = (jnp.dot(po_ref[...], w_ref[0:POOL_WIDTH, :], preferred_element_type=F32, precision=HIGHEST)
               + jnp.dot(ao, w_ref[POOL_WIDTH:, :], preferred_element_type=F32, precision=HIGHEST))
        h = x_ref[...] + mix
        h_sc[...] = h
        xn = _rms(h, g_ref[...])
        xn_sc[...] = xn
        gate_sc[...] = _route(jnp.dot(xn, wr_ref[...], preferred_element_type=F32, precision=HIGHEST))[0]
        acc_sc[...] = jnp.zeros_like(acc_sc)

    o = _swiglu(xn_sc[...].astype(BF16), wg_ref[...], wu_ref[...], wd_ref[...])
    acc_sc[...] += _gate_column(gate_sc[...], e) * o

    @pl.when(e == pl.num_programs(0) - 1)
    def _():
        y_ref[...] = _rms(h_sc[...] + acc_sc[...], gf_ref[...])


def _sample_tail(x_pad, po, ao_t, w_out, g_ffn, w_router, wg_bf, wu_bf, wd_bf, g_final):
    R, D = x_pad.shape
    E, _, DE = wg_bf.shape
    full = lambda shape: pl.BlockSpec(shape, lambda e: (0,) * len(shape))
    return pl.pallas_call(
        _sample_tail_kernel,
        grid=(E,),
        in_specs=[full((R, D)), full((R, POOL_WIDTH)), full((ATTN_WIDTH, R)), full(w_out.shape),
                  full((1, D)), full(w_router.shape),
                  pl.BlockSpec((None, D, DE), lambda e: (e, 0, 0)),
                  pl.BlockSpec((None, D, DE), lambda e: (e, 0, 0)),
                  pl.BlockSpec((None, DE, D), lambda e: (e, 0, 0)),
                  full((1, D))],
        out_specs=full((R, D)),
        out_shape=jax.ShapeDtypeStruct((R, D), F32),
        scratch_shapes=[pltpu.VMEM((R, D), F32), pltpu.VMEM((R, D), F32), pltpu.VMEM((R, LANES), F32),
                        pltpu.VMEM((R, D), F32)],
        compiler_params=_cparams(("arbitrary",)),
        name="sample_tail",
    )(x_pad, po, ao_t, w_out, g_ffn, w_router, wg_bf, wu_bf, wd_bf, g_final)


def _tile(n, pref):
    while n % pref:
        pref //= 2
    return pref


def kernel(x_prompt, x_sample, cache_k, cache_v, state_pool, page_table, norm_mix, w_in, w_pool, pool_scale,
           w_out, norm_ffn, w_router_group, w_router_expert, w_gate, w_up, w_down, norm_final):
    B, S, D = x_prompt.shape
    DB, DS, _ = x_sample.shape
    depth, n_phys, page = cache_k.shape[:3]
    assert depth == 1 and DS == 1, "single layer, one new token per sequence"
    n_pages = page_table.shape[1]
    past_len = n_pages * page
    assert S % MOBA_BLOCK == 0 and past_len % MOBA_BLOCK == 0 and MOBA_BLOCK % page == 0
    assert past_len // MOBA_BLOCK >= MOBA_TOPK and DB <= LANES
    T = B * S

    g_mix = norm_mix[0][None, :]
    g_ffn = norm_ffn[0][None, :]
    g_final = norm_final[None, :]
    ps = pool_scale[0][None, :]
    w_router = jnp.concatenate(
        [w_router_expert[0], w_router_group[0],
         jnp.zeros((D, LANES - N_EXPERTS - N_EXPERT_GROUPS), F32)], axis=1)
    wg_bf, wu_bf, wd_bf = w_gate[0].astype(BF16), w_up[0].astype(BF16), w_down[0].astype(BF16)

    xs = x_sample.reshape(DB, D)
    x_pad = jnp.pad(xs, ((0, LANES - DB), (0, 0)))
    pos_s = jnp.full((1,), past_len, jnp.int32)
    sp_t = jnp.swapaxes(state_pool[0], 0, 1)
    u_s, k_s, v_s, po_s, q_t, k_t, v_t = _sample_inproj(x_pad, g_mix, w_in[0], pos_s, sp_t, w_pool[0], ps)
    cache_kt = jnp.transpose(cache_k[0], (0, 2, 3, 1)).reshape(n_phys, ATTN_WIDTH, page)
    cache_vt = jnp.transpose(cache_v[0], (0, 2, 3, 1)).reshape(n_phys, ATTN_WIDTH, page)
    pt_flat = page_table.reshape(-1).astype(jnp.int32)

    ts = _tile(S, 512)
    kt_f, vt_f, q_bf, kt_bf, vt_bf, po, ksum, u_last = _inproj(
        x_prompt, g_mix, w_in[0], w_pool[0].astype(BF16), ps, ts)
    ksum = ksum[:, :, :ts // MOBA_BLOCK].reshape(B, S // MOBA_BLOCK, ATTN_WIDTH)
    ao, top = _moba_prompt(q_bf, kt_bf, vt_bf, ksum, pt_flat, q_t, cache_kt, DB, n_pages)
    h2, xg, counts, meta = _outproj(x_prompt.reshape(T, D), po.reshape(T, POOL_WIDTH), ao.reshape(T, ATTN_WIDTH),
                                    w_out[0].astype(BF16), g_ffn, w_router.astype(BF16), _tile(T, 512))
    dest, tile_e1, tile_e2, n_used, n_rows = _moe_layout(meta, counts)
    dest3d = dest.reshape(T // ROW_TILE, 1, ROW_TILE)
    x_grouped = _dispatch(dest3d, xg, n_rows)
    moe_sorted = _moe(tile_e1, tile_e2, n_used, x_grouped, wg_bf, wu_bf, wd_bf)
    y_prompt = _combine(dest3d, h2, g_final, moe_sorted).reshape(B, S, D)
    k_prompt = jnp.transpose(kt_f.reshape(B, N_HEADS, HEAD_DIM, S), (0, 3, 1, 2))[None]
    v_prompt = jnp.transpose(vt_f.reshape(B, N_HEADS, HEAD_DIM, S), (0, 3, 1, 2))[None]
    pool_prompt = u_last[None, :, HALO - POOL_HIST:, :]

    top_flat = top[:, :, :MOBA_TOPK].reshape(-1)
    ao_t = _sample_attn(pt_flat, top_flat, q_t, k_t, v_t, cache_kt, cache_vt, DB, n_pages)
    y_pad = _sample_tail(x_pad, po_s, ao_t, w_out[0], g_ffn, w_router, wg_bf, wu_bf, wd_bf, g_final)
    y_sample = y_pad[:DB].reshape(DB, 1, D)
    k_sample = k_s[:DB].reshape(1, DB, 1, N_HEADS, HEAD_DIM)
    v_sample = v_s[:DB].reshape(1, DB, 1, N_HEADS, HEAD_DIM)
    pool_sample = jnp.concatenate([state_pool[0][:, 1:], u_s[:DB, None, :]], axis=1)[None]
    return (y_prompt, y_sample, k_prompt, v_prompt, pool_prompt, k_sample, v_sample, pool_sample)
```

```python
import functools

import jax
import jax.numpy as jnp
import numpy as np
from jax import lax
from jax.experimental import pallas as pl
from jax.experimental.pallas import tpu as pltpu

F32 = jnp.float32
BF16 = jnp.bfloat16
HIGHEST = lax.Precision.HIGHEST

POOL_WINDOWS = (2, 4, 8, 16)
POOL_GROUP = 128
POOL_WIDTH = POOL_GROUP * len(POOL_WINDOWS)
POOL_HIST = max(POOL_WINDOWS) - 1
HALO = POOL_HIST + 1
HEAD_DIM = 64
N_HEADS = 8
ATTN_WIDTH = N_HEADS * HEAD_DIM
ROT_DIM = HEAD_DIM // 4
ROT_HALF = ROT_DIM // 2
ROPE_THETA = 500000.0
MOBA_BLOCK = 256
MOBA_TOPK = 3
N_EXPERT_GROUPS = 4
EXPERTS_PER_GROUP = 4
N_EXPERTS = N_EXPERT_GROUPS * EXPERTS_PER_GROUP
RMS_EPS = 1e-6

LANES = 128
SUBLANES = 8
HEADS_PER_TILE = LANES // HEAD_DIM
VMEM_LIMIT = 52 * 1024 * 1024

NEG_INF = float("-inf")
NEG_BIG = -1e30
BIG_IDX = 1 << 20
NT = (((1,), (1,)), ((), ()))
Q_SCALE = HEAD_DIM ** -0.5 * float(np.log2(np.e))


def _cparams(sem):
    return pltpu.CompilerParams(dimension_semantics=sem, vmem_limit_bytes=VMEM_LIMIT)


def _rms(x, g):
    r = lax.rsqrt(jnp.mean(x * x, axis=-1, keepdims=True) + RMS_EPS)
    return x * r * g


def _rope_angles(pos):
    inv = ROPE_THETA ** (-(jnp.arange(0, ROT_DIM, 2, dtype=F32) / ROT_DIM))
    ang = pos.astype(F32)[:, None] * inv[None, :]
    return jnp.cos(ang), jnp.sin(ang)


def _rope_tables(pos):
    cos, sin = _rope_angles(pos)
    d = np.arange(LANES) % HEAD_DIM
    fi = d % ROT_HALF
    cos_l, sin_l = cos[:, fi], sin[:, fi]
    c = jnp.where(d < ROT_DIM, cos_l, 1.0)
    s_lo = jnp.where(d < ROT_HALF, -sin_l, 0.0)
    s_hi = jnp.where((d >= ROT_HALF) & (d < ROT_DIM), sin_l, 0.0)
    return c.astype(F32), s_lo.astype(F32), s_hi.astype(F32)


def _rotate(x, c, s_lo, s_hi):
    outs = []
    for j in range(x.shape[1] // LANES):
        t = x[:, j * LANES:(j + 1) * LANES]
        up = pltpu.roll(t, LANES - ROT_HALF, axis=1)
        dn = pltpu.roll(t, ROT_HALF, axis=1)
        outs.append(t * c + up * s_lo + dn * s_hi)
    return jnp.concatenate(outs, axis=1)


def _rotate_t(xt, cos_t, sin_t):
    pieces = []
    for h in range(N_HEADS):
        r0 = h * HEAD_DIM
        x1 = xt[r0:r0 + ROT_HALF]
        x2 = xt[r0 + ROT_HALF:r0 + ROT_DIM]
        pieces += [x1 * cos_t - x2 * sin_t, x2 * cos_t + x1 * sin_t, xt[r0 + ROT_DIM:r0 + HEAD_DIM]]
    return jnp.concatenate(pieces, axis=0)


def _inproj_kernel(x_ref, g_ref, wuq_ref, wkv_ref, c_ref, slo_ref, shi_ref, ct_ref, st_ref, ind_ref,
                   wp_ref, ps_ref,
                   kf_ref, vf_ref, qb_ref, kb_ref, vb_ref, po_ref, ksum_ref, ulast_ref, ext_ref):
    s = pl.program_id(1)
    ts = x_ref.shape[0]
    xn = _rms(x_ref[...], g_ref[...]).astype(BF16)
    u = jnp.dot(xn, wuq_ref[:, 0:POOL_WIDTH], preferred_element_type=F32)
    q = jnp.dot(xn, wuq_ref[:, POOL_WIDTH:], preferred_element_type=F32)
    kt = lax.dot_general(wkv_ref[0:ATTN_WIDTH, :], xn, NT, preferred_element_type=F32)
    vt = lax.dot_general(wkv_ref[ATTN_WIDTH:, :], xn, NT, preferred_element_type=F32)
    q = _rotate(q, c_ref[...], slo_ref[...], shi_ref[...])
    kt = _rotate_t(kt, ct_ref[...], st_ref[...])
    kf_ref[...] = kt
    vf_ref[...] = vt
    qb_ref[...] = (q * Q_SCALE).astype(BF16)
    kb_ref[...] = kt.astype(BF16)
    ones = jnp.ones((HEAD_DIM, ts), BF16)
    vtb = vt.astype(BF16)
    vb_ref[...] = jnp.concatenate(
        [piece for h in range(N_HEADS) for piece in (vtb[h * HEAD_DIM:(h + 1) * HEAD_DIM], ones)], axis=0)
    ksum_ref[...] = lax.dot_general(ind_ref[...], kt, NT, preferred_element_type=F32, precision=HIGHEST)

    @pl.when(s == 0)
    def _():
        ext_ref[0:HALO, :] = jnp.zeros((HALO, POOL_WIDTH), F32)

    ext_ref[HALO:HALO + ts, :] = u
    pos1 = s * ts + lax.broadcasted_iota(jnp.int32, (ts, POOL_GROUP), 0) + 1
    outs = []
    for g, w in enumerate(POOL_WINDOWS):
        sl = slice(g * POOL_GROUP, (g + 1) * POOL_GROUP)
        ug = u[:, sl]
        acc = ug
        for j in range(1, w):
            acc = acc + ext_ref[HALO - j:HALO - j + ts, sl]
        cnt = jnp.minimum(pos1, w).astype(F32)
        d = (acc / cnt - ug).astype(BF16)
        outs.append(jnp.dot(d, wp_ref[g], preferred_element_type=F32))
    y = jnp.concatenate(outs, axis=1) * ps_ref[...]
    po_ref[...] = y.astype(BF16)
    tail = u[ts - HALO:ts, :]
    ext_ref[0:HALO, :] = tail

    @pl.when(s == pl.num_programs(1) - 1)
    def _():
        ulast_ref[...] = tail


def _inproj(x, g_mix, w_in, w_pool_bf, pool_scale, ts):
    B, S, D = x.shape
    ns = S // ts
    pos = jnp.arange(S, dtype=jnp.int32)
    c, s_lo, s_hi = _rope_tables(pos)
    cos, sin = _rope_angles(pos)
    cos_t, sin_t = cos.T, sin.T
    w_uq = w_in[:, :POOL_WIDTH + ATTN_WIDTH].astype(BF16)
    w_kv_t = w_in[:, POOL_WIDTH + ATTN_WIDTH:].T.astype(BF16)
    ind = (np.arange(ts)[None, :] // MOBA_BLOCK == np.arange(SUBLANES)[:, None]).astype(np.float32)
    row = lambda b, s: (b, s, 0)
    col = lambda b, s: (b, 0, s)
    tab = pl.BlockSpec((ts, LANES), lambda b, s: (s, 0))
    tab_t = pl.BlockSpec((ROT_HALF, ts), lambda b, s: (0, s))
    full2 = lambda shape: pl.BlockSpec(shape, lambda b, s: (0, 0))
    act = lambda: pl.BlockSpec((None, ts, ATTN_WIDTH), row)
    act_t = lambda: pl.BlockSpec((None, ATTN_WIDTH, ts), col)
    out_shapes = (
        jax.ShapeDtypeStruct((B, ATTN_WIDTH, S), F32),
        jax.ShapeDtypeStruct((B, ATTN_WIDTH, S), F32),
        jax.ShapeDtypeStruct((B, S, ATTN_WIDTH), BF16),
        jax.ShapeDtypeStruct((B, ATTN_WIDTH, S), BF16),
        jax.ShapeDtypeStruct((B, 2 * ATTN_WIDTH, S), BF16),
        jax.ShapeDtypeStruct((B, S, POOL_WIDTH), BF16),
        jax.ShapeDtypeStruct((B, ns, SUBLANES, ATTN_WIDTH), F32),
        jax.ShapeDtypeStruct((B, HALO, POOL_WIDTH), F32),
    )
    return pl.pallas_call(
        _inproj_kernel,
        grid=(B, ns),
        in_specs=[
            pl.BlockSpec((None, ts, D), row),
            full2((1, D)),
            full2(w_uq.shape),
            full2(w_kv_t.shape),
            tab, tab, tab, tab_t, tab_t,
            full2(ind.shape),
            pl.BlockSpec(w_pool_bf.shape, lambda b, s: (0, 0, 0)),
            full2((1, POOL_WIDTH)),
        ],
        out_specs=(act_t(), act_t(), act(), act_t(), pl.BlockSpec((None, 2 * ATTN_WIDTH, ts), col), act(),
                   pl.BlockSpec((None, None, SUBLANES, ATTN_WIDTH), lambda b, s: (b, s, 0, 0)),
                   pl.BlockSpec((None, HALO, POOL_WIDTH), lambda b, s: (b, 0, 0))),
        out_shape=out_shapes,
        scratch_shapes=[pltpu.VMEM((HALO + ts, POOL_WIDTH), F32)],
        compiler_params=_cparams(("arbitrary", "arbitrary")),
        name="prompt_inproj",
    )(x, g_mix, w_uq, w_kv_t, c, s_lo, s_hi, cos_t, sin_t, jnp.asarray(ind), w_pool_bf, pool_scale)


def _moba_kernel(pt_ref, q_ref, k_ref, v_ref, e_ref, ksum_ref, qt_ref, cache_ref, o_ref, top_ref,
                 s_sc, buf_ref, sem_ref, qb_sc, g_sc, *, chunk, scan_nch, scan_nblk):
    n1, n2 = pl.num_programs(1), pl.num_programs(2)
    step = (pl.program_id(0) * n1 + pl.program_id(1)) * n2 + pl.program_id(2)
    _scan_step(step, pl.num_programs(0) * n1 * n2, scan_nch, pt_ref, qt_ref, cache_ref, top_ref,
               buf_ref, sem_ref, qb_sc, g_sc, scan_nblk)

    c = pl.program_id(2)
    tq = q_ref.shape[0]
    nb = ksum_ref.shape[0]
    blocks_per_chunk = chunk // MOBA_BLOCK
    n_tiles = chunk // LANES
    q = q_ref[...]
    kmean = (ksum_ref[...] * (1.0 / MOBA_BLOCK)).astype(BF16)
    lane = lax.broadcasted_iota(jnp.int32, (tq, LANES), 1)
    blk_iota = lax.broadcasted_iota(jnp.int32, (nb, tq), 0)

    qp = []
    for h in range(HEADS_PER_TILE):
        in_head = (lane >= h * HEAD_DIM) & (lane < (h + 1) * HEAD_DIM)
        qm = jnp.where(in_head, q, jnp.zeros_like(q))
        g = lax.dot_general(kmean, qm, NT, preferred_element_type=F32)
        g = jnp.where(blk_iota < c, g, NEG_INF)
        sel = blk_iota == c
        for t in range(MOBA_TOPK):
            m = jnp.max(g, axis=0, keepdims=True)
            idx = jnp.min(jnp.where(g == m, blk_iota, BIG_IDX), axis=0, keepdims=True)
            hit = blk_iota == idx
            sel = sel | (hit & (t < c))
            g = jnp.where(hit, NEG_INF, g)
        bias_t = jnp.where(sel, 0.0, NEG_BIG)
        bias_t = jnp.concatenate([bias_t, jnp.zeros((LANES - nb, tq), F32)], axis=0)
        bias = jnp.transpose(bias_t).astype(BF16)
        qp.append(jnp.concatenate([qm, bias], axis=1))

    def keys(off):
        return jnp.concatenate([k_ref[:, pl.ds(off, chunk)], e_ref[:, pl.ds(off, chunk)]], axis=0)

    def fold(fn, acc, x):
        for j in range(n_tiles):
            acc = fn(acc, x[:, j * LANES:(j + 1) * LANES])
        return acc

    n_past = lax.shift_right_logical(c, int(np.log2(blocks_per_chunk)))

    def pass1(i, mrun):
        off = pl.multiple_of(i * chunk, chunk)
        kk = keys(off)
        new = []
        for h in range(HEADS_PER_TILE):
            s = jnp.dot(qp[h], kk, preferred_element_type=F32)
            s_sc[h, :, pl.ds(off, chunk)] = s
            new.append(fold(jnp.maximum, mrun[h], s))
        return tuple(new)

    def pairwise(body, n, init):
        carry = lax.fori_loop(0, lax.shift_right_logical(n, 1),
                              lambda i, c_: body(2 * i + 1, body(2 * i, c_)), init)
        return lax.cond((n & 1) == 1, lambda c_: body(n - 1, c_), lambda c_: c_, carry)

    mrun = pairwise(pass1, n_past, tuple(jnp.full((tq, LANES), NEG_BIG, F32) for _ in range(HEADS_PER_TILE)))
    off = pl.multiple_of(n_past * chunk, chunk)
    kk = keys(off)
    kpos = off + lax.broadcasted_iota(jnp.int32, (tq, chunk), 1)
    qpos = c * MOBA_BLOCK + lax.broadcasted_iota(jnp.int32, (tq, chunk), 0)
    m_b = []
    for h in range(HEADS_PER_TILE):
        s = jnp.dot(qp[h], kk, preferred_element_type=F32)
        s = jnp.where(kpos <= qpos, s, NEG_BIG)
        s_sc[h, :, pl.ds(off, chunk)] = s
        m = jnp.max(fold(jnp.maximum, mrun[h], s), axis=1, keepdims=True)
        m_b.append(jnp.broadcast_to(m, (tq, LANES)))

    def pass2(i, accs):
        off = pl.multiple_of(i * chunk, chunk)
        new = []
        for h in range(HEADS_PER_TILE):
            s = s_sc[h, :, pl.ds(off, chunk)]
            pb = jnp.concatenate([jnp.exp2(s[:, j * LANES:(j + 1) * LANES] - m_b[h]).astype(BF16)
                                  for j in range(n_tiles)], axis=1)
            vv = v_ref[h * LANES:(h + 1) * LANES, pl.ds(off, chunk)]
            new.append(accs[h] + lax.dot_general(pb, vv, NT, preferred_element_type=F32))
        return tuple(new)

    acc0, acc1 = pairwise(pass2, n_past + 1, (jnp.zeros((tq, LANES), F32),) * HEADS_PER_TILE)
    out = jnp.where(lane < HEAD_DIM, acc0 / pltpu.roll(acc0, HEAD_DIM, axis=1),
                    pltpu.roll(acc1, HEAD_DIM, axis=1) / acc1)
    o_ref[...] = out.astype(o_ref.dtype)


def _moba_prompt(q_bf, kt_bf, vt_bf, ksum, page_table_flat, q_t, cache_kt, DB, n_pages):
    B, S, W = q_bf.shape
    nb = S // MOBA_BLOCK
    assert nb <= LANES and HEADS_PER_TILE == 2
    ntile = W // LANES
    chunk = min(S, 4 * MOBA_BLOCK)
    ind = (np.arange(S)[None, :] // MOBA_BLOCK == np.arange(LANES)[:, None]).astype(np.float32)
    _, cw, page = cache_kt.shape
    nsteps = B * ntile * nb
    pages_per_step, rem = divmod(DB * n_pages, nsteps)
    scan_nch, rem2 = divmod(n_pages, max(pages_per_step, 1))
    assert rem == 0 and rem2 == 0 and pages_per_step % (MOBA_BLOCK // page) == 0, "cache sweep does not tile"
    scan_nblk = n_pages * page // MOBA_BLOCK
    assert scan_nblk <= LANES

    blk = pl.BlockSpec((None, MOBA_BLOCK, LANES), lambda b, t, c, pt: (b, c, t))
    seq = pl.BlockSpec((None, LANES, S), lambda b, t, c, pt: (b, t, 0))
    seq_v = pl.BlockSpec((None, HEADS_PER_TILE * LANES, S), lambda b, t, c, pt: (b, t, 0))
    const = lambda shape: pl.BlockSpec(shape, lambda b, t, c, pt: (0, 0))
    sample_of_step = lambda b, t, c, pt: (((b * ntile + t) * nb + c) // scan_nch, 0, 0)
    return pl.pallas_call(
        functools.partial(_moba_kernel, chunk=chunk, scan_nch=scan_nch, scan_nblk=scan_nblk),
        grid_spec=pltpu.PrefetchScalarGridSpec(
            num_scalar_prefetch=1,
            grid=(B, ntile, nb),
            in_specs=[blk, seq, seq_v, const((LANES, S)),
                      pl.BlockSpec((None, nb, LANES), lambda b, t, c, pt: (b, 0, t)),
                      const(q_t.shape), pl.BlockSpec(memory_space=pl.ANY)],
            out_specs=(blk, pl.BlockSpec((None, N_HEADS, LANES), sample_of_step)),
            scratch_shapes=[pltpu.VMEM((HEADS_PER_TILE, MOBA_BLOCK, S), F32),
                            pltpu.VMEM((2, pages_per_step, cw, page), F32), pltpu.SemaphoreType.DMA((2,)),
                            pltpu.VMEM(q_t.shape, F32), pltpu.VMEM((N_HEADS * SUBLANES, LANES), F32)],
        ),
        out_shape=(jax.ShapeDtypeStruct((B, S, W), BF16), jax.ShapeDtypeStruct((DB, N_HEADS, LANES), jnp.int32)),
        compiler_params=_cparams(("arbitrary", "arbitrary", "arbitrary")),
        name="prompt_moba",
    )(page_table_flat, q_bf, kt_bf, vt_bf, jnp.asarray(ind, BF16), ksum, q_t, cache_kt)


def _route(logits):
    lane = lax.broadcasted_iota(jnp.int32, logits.shape, 1)
    is_grp = (lane >= N_EXPERTS) & (lane < N_EXPERTS + N_EXPERT_GROUPS)
    gl = jnp.where(is_grp, logits, NEG_INF)
    ge = jnp.exp(gl - jnp.max(gl, axis=1, keepdims=True))
    pg = ge / jnp.sum(ge, axis=1, keepdims=True)
    pg_sel = jnp.max(pg, axis=1, keepdims=True)
    g_sel = jnp.min(jnp.where(is_grp & (pg == pg_sel), lane - N_EXPERTS, BIG_IDX), axis=1, keepdims=True)
    in_grp = (lane < N_EXPERTS) & (jnp.right_shift(lane, 2) == g_sel)
    el = jnp.where(in_grp, logits, NEG_INF)
    ee = jnp.exp(el - jnp.max(el, axis=1, keepdims=True))
    pe = jnp.where(in_grp, ee / jnp.sum(ee, axis=1, keepdims=True), -1.0)
    p1 = jnp.max(pe, axis=1, keepdims=True)
    i1 = jnp.min(jnp.where(pe == p1, lane, BIG_IDX), axis=1, keepdims=True)
    pe2 = jnp.where(lane == i1, -1.0, pe)
    p2 = jnp.max(pe2, axis=1, keepdims=True)
    i2 = jnp.min(jnp.where(pe2 == p2, lane, BIG_IDX), axis=1, keepdims=True)
    tot = p1 + p2
    gate = (jnp.where(lane == i1, p1 / tot * pg_sel, 0.0)
            + jnp.where(lane == i2, p2 / tot * pg_sel, 0.0))
    return gate, (g_sel, i1, i2)


PAIRS_PER_GROUP = EXPERTS_PER_GROUP * (EXPERTS_PER_GROUP - 1) // 2
N_CLASSES = N_EXPERT_GROUPS * PAIRS_PER_GROUP
_PAIRS = [(a, b) for a in range(EXPERTS_PER_GROUP) for b in range(a + 1, EXPERTS_PER_GROUP)]
CLASS_EXPERTS = np.array([[g * EXPERTS_PER_GROUP + a, g * EXPERTS_PER_GROUP + b]
                          for g in range(N_EXPERT_GROUPS) for a, b in _PAIRS], np.int32)


def _route_class(g_sel, i1, i2):
    assert EXPERTS_PER_GROUP == 4
    base = g_sel * EXPERTS_PER_GROUP
    a = jnp.minimum(i1, i2) - base
    b = jnp.maximum(i1, i2) - base
    pair = jnp.right_shift(a * (5 - a), 1) + b - 1
    return g_sel * PAIRS_PER_GROUP + pair


def _gate_column(gate, e):
    lane = lax.broadcasted_iota(jnp.int32, gate.shape, 1)
    return jnp.sum(jnp.where(lane == e, gate, 0.0), axis=1, keepdims=True)


def _swiglu(x_bf, wg, wu, wd):
    hg = jnp.dot(x_bf, wg, preferred_element_type=F32)
    hu = jnp.dot(x_bf, wu, preferred_element_type=F32)
    h = hg * (1.0 / (1.0 + jnp.exp(-hg))) * hu
    return jnp.dot(h.astype(BF16), wd, preferred_element_type=F32)


CLASS_LANE = N_EXPERTS + N_EXPERT_GROUPS
RANK_LANE = CLASS_LANE + 1


def _outproj_kernel(x_ref, po_ref, ao_ref, w_ref, g_ref, wr_ref, tri_ref, h_ref, xg_ref, cnt_ref, meta_ref):
    i = pl.program_id(0)
    d = x_ref.shape[1]
    mix = (jnp.dot(po_ref[...], w_ref[0:POOL_WIDTH, :], preferred_element_type=F32)
           + jnp.dot(ao_ref[...], w_ref[POOL_WIDTH:, :], preferred_element_type=F32))
    h = x_ref[...] + mix
    h_ref[...] = h
    xn = _rms(h, g_ref[...])
    gate, picks = _route(jnp.dot(xn.astype(BF16), wr_ref[...], preferred_element_type=F32))
    cls = _route_class(*picks)

    @pl.when(i == 0)
    def _():
        cnt_ref[...] = jnp.zeros_like(cnt_ref)

    lane = lax.broadcasted_iota(jnp.int32, gate.shape, 1)
    onehot = jnp.where(lane == cls, 1.0, 0.0)
    before = jnp.dot(tri_ref[...], onehot.astype(BF16), preferred_element_type=F32) + cnt_ref[0:1, :]
    rank = jnp.sum(onehot * before, axis=1, keepdims=True)
    cnt_ref[...] = cnt_ref[...] + jnp.sum(onehot, axis=0, keepdims=True)
    info = jnp.where(lane == CLASS_LANE, cls.astype(F32), jnp.where(lane == RANK_LANE, rank, gate))
    xg_ref[:, 0:d] = xn
    xg_ref[:, d:] = info
    meta_ref[...] = jnp.transpose(info)[META_ROW0:META_ROW0 + SUBLANES, :]


META_ROW0 = CLASS_LANE // SUBLANES * SUBLANES


def _outproj(x2d, po2d, ao2d, w_out_bf, g_ffn, w_router_bf, tm):
    T, D = x2d.shape
    row = lambda i: (i, 0)
    full = lambda shape: pl.BlockSpec(shape, lambda i: (0, 0))
    tri = np.tril(np.ones((tm, tm), np.float32), -1)
    return pl.pallas_call(
        _outproj_kernel,
        grid=(T // tm,),
        in_specs=[pl.BlockSpec((tm, D), row), pl.BlockSpec((tm, POOL_WIDTH), row),
                  pl.BlockSpec((tm, ATTN_WIDTH), row), full(w_out_bf.shape), full((1, D)),
                  full(w_router_bf.shape), full((tm, tm))],
        out_specs=(pl.BlockSpec((tm, D), row), pl.BlockSpec((tm, D + LANES), row), full((SUBLANES, LANES)),
                   pl.BlockSpec((None, SUBLANES, tm), lambda i: (i, 0, 0))),
        out_shape=(jax.ShapeDtypeStruct((T, D), F32), jax.ShapeDtypeStruct((T, D + LANES), F32),
                   jax.ShapeDtypeStruct((SUBLANES, LANES), F32),
                   jax.ShapeDtypeStruct((T // tm, SUBLANES, tm), F32)),
        compiler_params=_cparams(("arbitrary",)),
        name="prompt_outproj",
    )(x2d, po2d, ao2d, w_out_bf, g_ffn, w_router_bf, jnp.asarray(tri, BF16))


MOE_TILE = 512
ROW_TILE = 512
DMA_UNROLL = 8


def _dispatch_kernel(dest_ref, xg_ref, zeros_ref, xs_ref, sem_ref):
    del zeros_ref
    n = xg_ref.shape[0]

    def start(r, carry):
        pltpu.make_async_copy(xg_ref.at[pl.ds(r, 1)], xs_ref.at[pl.ds(dest_ref[0, r], 1)], sem_ref.at[0]).start()
        return carry

    lax.fori_loop(0, n, start, 0, unroll=DMA_UNROLL)
    pltpu.make_async_copy(xg_ref, xs_ref.at[pl.ds(0, n)], sem_ref.at[0]).wait()


def _dispatch(dest3d, xg, n_rows):
    T, W = xg.shape
    tm = dest3d.shape[2]
    return pl.pallas_call(
        _dispatch_kernel,
        grid=(T // tm,),
        in_specs=[pl.BlockSpec((None, 1, tm), lambda i: (i, 0, 0), memory_space=pltpu.SMEM),
                  pl.BlockSpec((tm, W), lambda i: (i, 0)),
                  pl.BlockSpec(memory_space=pl.ANY)],
        out_specs=pl.BlockSpec(memory_space=pl.ANY),
        out_shape=jax.ShapeDtypeStruct((n_rows, W), F32),
        scratch_shapes=[pltpu.SemaphoreType.DMA((1,))],
        input_output_aliases={2: 0},
        compiler_params=_cparams(("arbitrary",)),
        name="moe_dispatch",
    )(dest3d, xg, jnp.zeros((n_rows, W), F32))


def _moe_kernel(e1_ref, e2_ref, nused_ref, xs_ref, wg1_ref, wu1_ref, wd1_ref, wg2_ref, wu2_ref, wd2_ref, o_ref):
    i = pl.program_id(0)
    d = o_ref.shape[1]

    @pl.when(i < nused_ref[0])
    def _():
        x = xs_ref[:, 0:d].astype(BF16)
        gate = xs_ref[:, d:]
        o_ref[...] = (_gate_column(gate, e1_ref[i]) * _swiglu(x, wg1_ref[...], wu1_ref[...], wd1_ref[...])
                      + _gate_column(gate, e2_ref[i]) * _swiglu(x, wg2_ref[...], wu2_ref[...], wd2_ref[...]))

    @pl.when(i >= nused_ref[0])
    def _():
        o_ref[...] = jnp.zeros_like(o_ref)


def _moe(tile_e1, tile_e2, n_used, xs, wg_bf, wu_bf, wd_bf):
    n_rows, W = xs.shape
    E, D, DE = wg_bf.shape
    tile = lambda i, e1, e2, nu: (jnp.maximum(jnp.minimum(i, nu[0] - 1), 0), 0)
    first = lambda shape: pl.BlockSpec((None,) + shape, lambda i, e1, e2, nu: (e1[i], 0, 0))
    second = lambda shape: pl.BlockSpec((None,) + shape, lambda i, e1, e2, nu: (e2[i], 0, 0))
    return pl.pallas_call(
        _moe_kernel,
        grid_spec=pltpu.PrefetchScalarGridSpec(
            num_scalar_prefetch=3,
            grid=(n_rows // MOE_TILE,),
            in_specs=[pl.BlockSpec((MOE_TILE, W), tile),
                      first((D, DE)), first((D, DE)), first((DE, D)),
                      second((D, DE)), second((D, DE)), second((DE, D))],
            out_specs=pl.BlockSpec((MOE_TILE, D), lambda i, e1, e2, nu: (i, 0)),
        ),
        out_shape=jax.ShapeDtypeStruct((n_rows, D), F32),
        compiler_params=_cparams(("arbitrary",)),
        name="moe_experts",
    )(tile_e1, tile_e2, n_used, xs, wg_bf, wu_bf, wd_bf, wg_bf, wu_bf, wd_bf)


def _combine_copies_start(dest_ref, src_ref, buf_ref, sem_ref, slot):
    def start(r, carry):
        pltpu.make_async_copy(src_ref.at[pl.ds(dest_ref[0, r], 1)], buf_ref.at[slot, pl.ds(r, 1)],
                              sem_ref.at[slot]).start()
        return carry

    lax.fori_loop(0, buf_ref.shape[1], start, 0, unroll=DMA_UNROLL)


def _combine_kernel(dest_ref, dest_next_ref, h_ref, gf_ref, moe_ref, y_ref, buf_ref, sem_ref):
    i = pl.program_id(0)
    n = buf_ref.shape[1]
    slot = i % 2

    @pl.when(i == 0)
    def _():
        _combine_copies_start(dest_ref, moe_ref, buf_ref, sem_ref, slot)

    @pl.when(i + 1 < pl.num_programs(0))
    def _():
        _combine_copies_start(dest_next_ref, moe_ref, buf_ref, sem_ref, 1 - slot)

    pltpu.make_async_copy(moe_ref.at[pl.ds(0, n)], buf_ref.at[slot], sem_ref.at[slot]).wait()
    y_ref[...] = _rms(h_ref[...] + buf_ref[slot], gf_ref[...])


def _combine(dest3d, h2d, g_final, moe_sorted):
    T, D = h2d.shape
    nt, _, tm = dest3d.shape
    return pl.pallas_call(
        _combine_kernel,
        grid=(nt,),
        in_specs=[pl.BlockSpec((None, 1, tm), lambda i: (i, 0, 0), memory_space=pltpu.SMEM),
                  pl.BlockSpec((None, 1, tm), lambda i: (jnp.minimum(i + 1, nt - 1), 0, 0),
                               memory_space=pltpu.SMEM),
                  pl.BlockSpec((tm, D), lambda i: (i, 0)),
                  pl.BlockSpec((1, D), lambda i: (0, 0)),
                  pl.BlockSpec(memory_space=pl.ANY)],
        out_specs=pl.BlockSpec((tm, D), lambda i: (i, 0)),
        out_shape=jax.ShapeDtypeStruct((T, D), F32),
        scratch_shapes=[pltpu.VMEM((2, tm, D), F32), pltpu.SemaphoreType.DMA((2,))],
        compiler_params=_cparams(("arbitrary",)),
        name="moe_combine",
    )(dest3d, dest3d, h2d, g_final, moe_sorted)


def _moe_layout(meta, counts):
    cls = meta[:, CLASS_LANE - META_ROW0, :].reshape(-1).astype(jnp.int32)
    rank = meta[:, RANK_LANE - META_ROW0, :].reshape(-1).astype(jnp.int32)
    cnt = counts[0, :N_CLASSES].astype(jnp.int32)
    tiles = (cnt + MOE_TILE - 1) // MOE_TILE
    tile_end = jnp.cumsum(tiles)
    row_start = (tile_end - tiles) * MOE_TILE
    dest = rank
    for c in range(N_CLASSES):
        dest = dest + jnp.where(cls == c, row_start[c], 0)
    max_tiles = cls.shape[0] // MOE_TILE + N_CLASSES
    tile_ids = jnp.arange(max_tiles, dtype=jnp.int32)
    tile_class = jnp.minimum(jnp.sum((tile_ids[:, None] >= tile_end[None, :]).astype(jnp.int32), axis=1),
                             N_CLASSES - 1)
    experts = jnp.asarray(CLASS_EXPERTS)[tile_class]
    return dest, experts[:, 0], experts[:, 1], tile_end[-1:].astype(jnp.int32), max_tiles * MOE_TILE


def _sample_inproj_kernel(x_ref, g_ref, w_ref, c_ref, slo_ref, shi_ref, sp_ref, wp_ref, ps_ref,
                          u_ref, k_ref, v_ref, po_ref, qt_ref, kt_ref, vt_ref):
    xn = _rms(x_ref[...], g_ref[...])

    def proj(i):
        return jnp.dot(xn, w_ref[:, i * POOL_WIDTH:(i + 1) * POOL_WIDTH],
                       preferred_element_type=F32, precision=HIGHEST)

    u, q, k, v = proj(0), proj(1), proj(2), proj(3)
    c, s_lo, s_hi = c_ref[...], slo_ref[...], shi_ref[...]
    q = _rotate(q, c, s_lo, s_hi)
    k = _rotate(k, c, s_lo, s_hi)
    u_ref[...] = u
    k_ref[...] = k
    v_ref[...] = v
    qt_ref[...] = jnp.transpose(q)
    kt_ref[...] = jnp.transpose(k)
    vt_ref[...] = jnp.transpose(v)
    db = sp_ref.shape[1]
    outs = []
    for g, w in enumerate(POOL_WINDOWS):
        sl = slice(g * POOL_GROUP, (g + 1) * POOL_GROUP)
        ug = u[0:db, sl]
        acc = ug
        for j in range(1, w):
            acc = acc + sp_ref[POOL_HIST - j, :, sl]
        d = acc / float(w) - ug
        outs.append(jnp.dot(d, wp_ref[g], preferred_element_type=F32, precision=HIGHEST))
    po_ref[...] = jnp.zeros_like(po_ref)
    po_ref[0:db, :] = jnp.concatenate(outs, axis=1) * ps_ref[...]


def _sample_inproj(x_pad, g_mix, w_in, pos, sp_t, w_pool, pool_scale):
    R, D = x_pad.shape
    c, s_lo, s_hi = _rope_tables(pos)
    row = jax.ShapeDtypeStruct((R, POOL_WIDTH), F32)
    col = jax.ShapeDtypeStruct((POOL_WIDTH, R), F32)
    return pl.pallas_call(
        _sample_inproj_kernel,
        out_shape=(row, row, row, row, col, col, col),
        compiler_params=pltpu.CompilerParams(vmem_limit_bytes=VMEM_LIMIT),
        name="sample_inproj",
    )(x_pad, g_mix, w_in, c, s_lo, s_hi, sp_t, w_pool, pool_scale)


def _column(mat, b):
    lane = lax.broadcasted_iota(jnp.int32, mat.shape, 1)
    col = jnp.sum(jnp.where(lane == b, mat, 0.0), axis=1, keepdims=True)
    return jnp.broadcast_to(col, mat.shape)


def _scan_copies(pt_ref, cache_ref, buf_ref, sem_ref, step, slot):
    n = buf_ref.shape[1]
    return [pltpu.make_async_copy(cache_ref.at[pt_ref[step * n + i]], buf_ref.at[slot, i], sem_ref.at[slot])
            for i in range(n)]


def _scan_step(step, nsteps, nch, pt_ref, qt_ref, cache_ref, top_ref, buf_ref, sem_ref, qb_sc, g_sc, nblk):
    b = step // nch
    ch = step - b * nch
    slot = step % 2

    @pl.when(step == 0)
    def _():
        for cp in _scan_copies(pt_ref, cache_ref, buf_ref, sem_ref, step, slot):
            cp.start()

    @pl.when(step + 1 < nsteps)
    def _():
        for cp in _scan_copies(pt_ref, cache_ref, buf_ref, sem_ref, step + 1, 1 - slot):
            cp.start()

    @pl.when(ch == 0)
    def _():
        qb_sc[...] = _column(qt_ref[...], b)
        g_sc[...] = jnp.zeros_like(g_sc)

    for cp in _scan_copies(pt_ref, cache_ref, buf_ref, sem_ref, step, slot):
        cp.wait()
    qb = qb_sc[...]
    lane = lax.broadcasted_iota(jnp.int32, (N_HEADS, LANES), 1)
    pages_per_blk = MOBA_BLOCK // buf_ref.shape[3]
    blks_per_step = buf_ref.shape[1] // pages_per_blk
    lane8 = lax.broadcasted_iota(jnp.int32, (SUBLANES, LANES), 1)
    g = [g_sc[h * SUBLANES:(h + 1) * SUBLANES, :] for h in range(N_HEADS)]
    for j in range(blks_per_step):
        tot = buf_ref[slot, j * pages_per_blk]
        for i in range(1, pages_per_blk):
            tot = tot + buf_ref[slot, j * pages_per_blk + i]
        pr = tot * qb
        for h in range(N_HEADS):
            a = pr[h * HEAD_DIM:h * HEAD_DIM + SUBLANES]
            for r in range(h * HEAD_DIM + SUBLANES, (h + 1) * HEAD_DIM, SUBLANES):
                a = a + pr[r:r + SUBLANES]
            g[h] = jnp.where(lane8 == ch * blks_per_step + j, jnp.sum(a, axis=1, keepdims=True), g[h])
    for h in range(N_HEADS):
        g_sc[h * SUBLANES:(h + 1) * SUBLANES, :] = g[h]

    @pl.when(ch == nch - 1)
    def _():
        full = jnp.concatenate([jnp.sum(gh, axis=0, keepdims=True) for gh in g], axis=0)
        gg = jnp.where(lane < nblk, full, NEG_INF)
        top = jnp.zeros((N_HEADS, LANES), jnp.int32)
        for t in range(MOBA_TOPK):
            m = jnp.max(gg, axis=1, keepdims=True)
            idx = jnp.min(jnp.where(gg == m, lane, BIG_IDX), axis=1, keepdims=True)
            top = jnp.where(lane == t, idx, top)
            gg = jnp.where(lane == idx, NEG_INF, gg)
        top_ref[...] = top


def _attn_copies(pt_ref, top_ref, ck_ref, cv_ref, kbuf, vbuf, sem_ref, b, slot, n_pages, pages_per_blk):
    cps = []
    for h in range(N_HEADS):
        rows = pl.ds(h * HEAD_DIM, HEAD_DIM)
        for t in range(MOBA_TOPK):
            blk = top_ref[(b * N_HEADS + h) * MOBA_TOPK + t]
            for i in range(pages_per_blk):
                phys = pt_ref[b * n_pages + blk * pages_per_blk + i]
                j = t * pages_per_blk + i
                cps.append(pltpu.make_async_copy(ck_ref.at[phys, rows, :], kbuf.at[slot, h, j], sem_ref.at[slot]))
                cps.append(pltpu.make_async_copy(cv_ref.at[phys, rows, :], vbuf.at[slot, h, j], sem_ref.at[slot]))
    return cps


def _sample_attn_kernel(pt_ref, top_ref, qt_ref, kt_ref, vt_ref, ck_ref, cv_ref, o_ref, kbuf, vbuf, sem_ref,
                        *, n_pages):
    b = pl.program_id(0)
    nb = pl.num_programs(0)
    slot = b % 2
    n_sel = kbuf.shape[2]
    copies = functools.partial(_attn_copies, pt_ref, top_ref, ck_ref, cv_ref, kbuf, vbuf, sem_ref,
                               n_pages=n_pages, pages_per_blk=n_sel // MOBA_TOPK)

    @pl.when(b == 0)
    def _():
        o_ref[...] = jnp.zeros_like(o_ref)
        for cp in copies(b=b, slot=slot):
            cp.start()

    @pl.when(b + 1 < nb)
    def _():
        for cp in copies(b=b + 1, slot=1 - slot):
            cp.start()

    for cp in copies(b=b, slot=slot):
        cp.wait()

    scale = HEAD_DIM ** -0.5
    lane = lax.broadcasted_iota(jnp.int32, (HEAD_DIM, LANES), 1)
    for h in range(N_HEADS):
        rows = slice(h * HEAD_DIM, (h + 1) * HEAD_DIM)
        qb = _column(qt_ref[rows, :], b)
        kb = _column(kt_ref[rows, :], b)
        vb = _column(vt_ref[rows, :], b)
        s = [jnp.sum(kbuf[slot, h, j] * qb, axis=0, keepdims=True) * scale for j in range(n_sel)]
        s_own = jnp.sum(kb * qb, axis=0, keepdims=True) * scale
        m = s_own
        for sj in s:
            m = jnp.maximum(m, jnp.max(sj, axis=1, keepdims=True))
        p = [jnp.exp(sj - m) for sj in s]
        p_own = jnp.exp(s_own - m)
        l = p_own
        acc = jnp.zeros((HEAD_DIM, LANES), F32)
        for j, pj in enumerate(p):
            l = l + jnp.sum(pj, axis=1, keepdims=True)
            acc = acc + vbuf[slot, h, j] * pj
        col = (jnp.sum(acc, axis=1, keepdims=True) + p_own * vb) / l
        o_ref[rows, :] = jnp.where(lane == b, col, o_ref[rows, :])


def _sample_attn(page_table_flat, top_flat, q_t, k_t, v_t, cache_kt, cache_vt, DB, n_pages):
    _, W, page = cache_kt.shape
    n_sel = MOBA_TOPK * MOBA_BLOCK // page
    vm = lambda: pl.BlockSpec(q_t.shape, lambda b, pt, tp: (0, 0))
    hbm = lambda: pl.BlockSpec(memory_space=pl.ANY)
    return pl.pallas_call(
        functools.partial(_sample_attn_kernel, n_pages=n_pages),
        grid_spec=pltpu.PrefetchScalarGridSpec(
            num_scalar_prefetch=2,
            grid=(DB,),
            in_specs=[vm(), vm(), vm(), hbm(), hbm()],
            out_specs=vm(),
            scratch_shapes=[pltpu.VMEM((2, N_HEADS, n_sel, HEAD_DIM, page), F32),
                            pltpu.VMEM((2, N_HEADS, n_sel, HEAD_DIM, page), F32),
                            pltpu.SemaphoreType.DMA((2,))],
        ),
        out_shape=jax.ShapeDtypeStruct(q_t.shape, F32),
        compiler_params=_cparams(("arbitrary",)),
        name="sample_attn",
    )(page_table_flat, top_flat, q_t, k_t, v_t, cache_kt, cache_vt)


def _sample_tail_kernel(x_ref, po_ref, aot_ref, w_ref, g_ref, wr_ref, wg_ref, wu_ref, wd_ref, gf_ref,
                        y_ref, h_sc, xn_sc, gate_sc, acc_sc):
    e = pl.program_id(0)

    @pl.when(e == 0)
    def _():
        ao = jnp.transpose(aot_ref[...])
        mix = (jnp.dot(po_ref[...], w_ref[0:POOL_WIDTH, :], preferred_element_type=F32, precision=HIGHEST)
               + jnp.dot(ao, w_ref[POOL_WIDTH:, :], preferred_element_type=F32, precision=HIGHEST))
        h = x_ref[...] + mix
        h_sc[...] = h
        xn = _rms(h, g_ref[...])
        xn_sc[...] = xn
        gate_sc[...] = _route(jnp.dot(xn, wr_ref[...], preferred_element_type=F32, precision=HIGHEST))[0]
        acc_sc[...] = jnp.zeros_like(acc_sc)

    o = _swiglu(xn_sc[...].astype(BF16), wg_ref[...], wu_ref[...], wd_ref[...])
    acc_sc[...] += _gate_column(gate_sc[...], e) * o

    @pl.when(e == pl.num_programs(0) - 1)
    def _():
        y_ref[...] = _rms(h_sc[...] + acc_sc[...], gf_ref[...])


def _sample_tail(x_pad, po, ao_t, w_out, g_ffn, w_router, wg_bf, wu_bf, wd_bf, g_final):
    R, D = x_pad.shape
    E, _, DE = wg_bf.shape
    full = lambda shape: pl.BlockSpec(shape, lambda e: (0,) * len(shape))
    return pl.pallas_call(
        _sample_tail_kernel,
        grid=(E,),
        in_specs=[full((R, D)), full((R, POOL_WIDTH)), full((ATTN_WIDTH, R)), full(w_out.shape),
                  full((1, D)), full(w_router.shape),
                  pl.BlockSpec((None, D, DE), lambda e: (e, 0, 0)),
                  pl.BlockSpec((None, D, DE), lambda e: (e, 0, 0)),
                  pl.BlockSpec((None, DE, D), lambda e: (e, 0, 0)),
                  full((1, D))],
        out_specs=full((R, D)),
        out_shape=jax.ShapeDtypeStruct((R, D), F32),
        scratch_shapes=[pltpu.VMEM((R, D), F32), pltpu.VMEM((R, D), F32), pltpu.VMEM((R, LANES), F32),
                        pltpu.VMEM((R, D), F32)],
        compiler_params=_cparams(("arbitrary",)),
        name="sample_tail",
    )(x_pad, po, ao_t, w_out, g_ffn, w_router, wg_bf, wu_bf, wd_bf, g_final)


def _tile(n, pref):
    while n % pref:
        pref //= 2
    return pref


def kernel(x_prompt, x_sample, cache_k, cache_v, state_pool, page_table, norm_mix, w_in, w_pool, pool_scale,
           w_out, norm_ffn, w_router_group, w_router_expert, w_gate, w_up, w_down, norm_final):
    B, S, D = x_prompt.shape
    DB, DS, _ = x_sample.shape
    depth, n_phys, page = cache_k.shape[:3]
    assert depth == 1 and DS == 1, "single layer, one new token per sequence"
    n_pages = page_table.shape[1]
    past_len = n_pages * page
    assert S % MOBA_BLOCK == 0 and past_len % MOBA_BLOCK == 0 and MOBA_BLOCK % page == 0
    assert past_len // MOBA_BLOCK >= MOBA_TOPK and DB <= LANES
    T = B * S

    g_mix = norm_mix[0][None, :]
    g_ffn = norm_ffn[0][None, :]
    g_final = norm_final[None, :]
    ps = pool_scale[0][None, :]
    w_router = jnp.concatenate(
        [w_router_expert[0], w_router_group[0],
         jnp.zeros((D, LANES - N_EXPERTS - N_EXPERT_GROUPS), F32)], axis=1)
    wg_bf, wu_bf, wd_bf = w_gate[0].astype(BF16), w_up[0].astype(BF16), w_down[0].astype(BF16)

    xs = x_sample.reshape(DB, D)
    x_pad = jnp.pad(xs, ((0, LANES - DB), (0, 0)))
    pos_s = jnp.full((1,), past_len, jnp.int32)
    sp_t = jnp.swapaxes(state_pool[0], 0, 1)
    u_s, k_s, v_s, po_s, q_t, k_t, v_t = _sample_inproj(x_pad, g_mix, w_in[0], pos_s, sp_t, w_pool[0], ps)
    cache_kt = jnp.transpose(cache_k[0], (0, 2, 3, 1)).reshape(n_phys, ATTN_WIDTH, page)
    cache_vt = jnp.transpose(cache_v[0], (0, 2, 3, 1)).reshape(n_phys, ATTN_WIDTH, page)
    pt_flat = page_table.reshape(-1).astype(jnp.int32)

    ts = _tile(S, 512)
    kt_f, vt_f, q_bf, kt_bf, vt_bf, po, ksum, u_last = _inproj(
        x_prompt, g_mix, w_in[0], w_pool[0].astype(BF16), ps, ts)
    ksum = ksum[:, :, :ts // MOBA_BLOCK].reshape(B, S // MOBA_BLOCK, ATTN_WIDTH)
    ao, top = _moba_prompt(q_bf, kt_bf, vt_bf, ksum, pt_flat, q_t, cache_kt, DB, n_pages)
    h2, xg, counts, meta = _outproj(x_prompt.reshape(T, D), po.reshape(T, POOL_WIDTH), ao.reshape(T, ATTN_WIDTH),
                                    w_out[0].astype(BF16), g_ffn, w_router.astype(BF16), _tile(T, 512))
    dest, tile_e1, tile_e2, n_used, n_rows = _moe_layout(meta, counts)
    dest3d = dest.reshape(T // ROW_TILE, 1, ROW_TILE)
    x_grouped = _dispatch(dest3d, xg, n_rows)
    moe_sorted = _moe(tile_e1, tile_e2, n_used, x_grouped, wg_bf, wu_bf, wd_bf)
    y_prompt = _combine(dest3d, h2, g_final, moe_sorted).reshape(B, S, D)
    k_prompt = jnp.transpose(kt_f.reshape(B, N_HEADS, HEAD_DIM, S), (0, 3, 1, 2))[None]
    v_prompt = jnp.transpose(vt_f.reshape(B, N_HEADS, HEAD_DIM, S), (0, 3, 1, 2))[None]
    pool_prompt = u_last[None, :, HALO - POOL_HIST:, :]

    top_flat = top[:, :, :MOBA_TOPK].reshape(-1)
    ao_t = _sample_attn(pt_flat, top_flat, q_t, k_t, v_t, cache_kt, cache_vt, DB, n_pages)
    y_pad = _sample_tail(x_pad, po_s, ao_t, w_out[0], g_ffn, w_router, wg_bf, wu_bf, wd_bf, g_final)
    y_sample = y_pad[:DB].reshape(DB, 1, D)
    k_sample = k_s[:DB].reshape(1, DB, 1, N_HEADS, HEAD_DIM)
    v_sample = v_s[:DB].reshape(1, DB, 1, N_HEADS, HEAD_DIM)
    pool_sample = jnp.concatenate([state_pool[0][:, 1:], u_s[:DB, None, :]], axis=1)[None]
    return (y_prompt, y_sample, k_prompt, v_prompt, pool_prompt, k_sample, v_sample, pool_sample)
```

```python
import functools

import jax
import jax.numpy as jnp
import numpy as np
from jax import lax
from jax.experimental import pallas as pl
from jax.experimental.pallas import tpu as pltpu

F32 = jnp.float32
BF16 = jnp.bfloat16
HIGHEST = lax.Precision.HIGHEST

POOL_WINDOWS = (2, 4, 8, 16)
POOL_GROUP = 128
POOL_WIDTH = POOL_GROUP * len(POOL_WINDOWS)
POOL_HIST = max(POOL_WINDOWS) - 1
HALO = POOL_HIST + 1
HEAD_DIM = 64
N_HEADS = 8
ATTN_WIDTH = N_HEADS * HEAD_DIM
ROT_DIM = HEAD_DIM // 4
ROT_HALF = ROT_DIM // 2
ROPE_THETA = 500000.0
MOBA_BLOCK = 256
MOBA_TOPK = 3
N_EXPERT_GROUPS = 4
EXPERTS_PER_GROUP = 4
N_EXPERTS = N_EXPERT_GROUPS * EXPERTS_PER_GROUP
RMS_EPS = 1e-6

LANES = 128
SUBLANES = 8
HEADS_PER_TILE = LANES // HEAD_DIM
VMEM_LIMIT = 52 * 1024 * 1024

NEG_INF = float("-inf")
NEG_BIG = -1e30
BIG_IDX = 1 << 20
NT = (((1,), (1,)), ((), ()))
Q_SCALE = HEAD_DIM ** -0.5 * float(np.log2(np.e))


def _cparams(sem):
    return pltpu.CompilerParams(dimension_semantics=sem, vmem_limit_bytes=VMEM_LIMIT)


def _rms(x, g):
    r = lax.rsqrt(jnp.mean(x * x, axis=-1, keepdims=True) + RMS_EPS)
    return x * r * g


def _rope_angles(pos):
    inv = ROPE_THETA ** (-(jnp.arange(0, ROT_DIM, 2, dtype=F32) / ROT_DIM))
    ang = pos.astype(F32)[:, None] * inv[None, :]
    return jnp.cos(ang), jnp.sin(ang)


def _rope_tables(pos):
    cos, sin = _rope_angles(pos)
    d = np.arange(LANES) % HEAD_DIM
    fi = d % ROT_HALF
    cos_l, sin_l = cos[:, fi], sin[:, fi]
    c = jnp.where(d < ROT_DIM, cos_l, 1.0)
    s_lo = jnp.where(d < ROT_HALF, -sin_l, 0.0)
    s_hi = jnp.where((d >= ROT_HALF) & (d < ROT_DIM), sin_l, 0.0)
    return c.astype(F32), s_lo.astype(F32), s_hi.astype(F32)


def _rotate(x, c, s_lo, s_hi):
    outs = []
    for j in range(x.shape[1] // LANES):
        t = x[:, j * LANES:(j + 1) * LANES]
        up = pltpu.roll(t, LANES - ROT_HALF, axis=1)
        dn = pltpu.roll(t, ROT_HALF, axis=1)
        outs.append(t * c + up * s_lo + dn * s_hi)
    return jnp.concatenate(outs, axis=1)


def _rotate_t(xt, cos_t, sin_t):
    pieces = []
    for h in range(N_HEADS):
        r0 = h * HEAD_DIM
        x1 = xt[r0:r0 + ROT_HALF]
        x2 = xt[r0 + ROT_HALF:r0 + ROT_DIM]
        pieces += [x1 * cos_t - x2 * sin_t, x2 * cos_t + x1 * sin_t, xt[r0 + ROT_DIM:r0 + HEAD_DIM]]
    return jnp.concatenate(pieces, axis=0)


def _inproj_kernel(x_ref, g_ref, wuq_ref, wkv_ref, c_ref, slo_ref, shi_ref, ct_ref, st_ref, ind_ref,
                   wp_ref, ps_ref,
                   kf_ref, vf_ref, qb_ref, kb_ref, vb_ref, po_ref, ksum_ref, ulast_ref, ext_ref):
    s = pl.program_id(1)
    ts = x_ref.shape[0]
    xn = _rms(x_ref[...], g_ref[...]).astype(BF16)
    u = jnp.dot(xn, wuq_ref[:, 0:POOL_WIDTH], preferred_element_type=F32)
    q = jnp.dot(xn, wuq_ref[:, POOL_WIDTH:], preferred_element_type=F32)
    kt = lax.dot_general(wkv_ref[0:ATTN_WIDTH, :], xn, NT, preferred_element_type=F32)
    vt = lax.dot_general(wkv_ref[ATTN_WIDTH:, :], xn, NT, preferred_element_type=F32)
    q = _rotate(q, c_ref[...], slo_ref[...], shi_ref[...])
    kt = _rotate_t(kt, ct_ref[...], st_ref[...])
    kf_ref[...] = kt
    vf_ref[...] = vt
    qb_ref[...] = (q * Q_SCALE).astype(BF16)
    kb_ref[...] = kt.astype(BF16)
    ones = jnp.ones((HEAD_DIM, ts), BF16)
    vtb = vt.astype(BF16)
    vb_ref[...] = jnp.concatenate(
        [piece for h in range(N_HEADS) for piece in (vtb[h * HEAD_DIM:(h + 1) * HEAD_DIM], ones)], axis=0)
    ksum_ref[...] = lax.dot_general(ind_ref[...], kt, NT, preferred_element_type=F32, precision=HIGHEST)

    @pl.when(s == 0)
    def _():
        ext_ref[0:HALO, :] = jnp.zeros((HALO, POOL_WIDTH), F32)

    ext_ref[HALO:HALO + ts, :] = u
    pos1 = s * ts + lax.broadcasted_iota(jnp.int32, (ts, POOL_GROUP), 0) + 1
    outs = []
    for g, w in enumerate(POOL_WINDOWS):
        sl = slice(g * POOL_GROUP, (g + 1) * POOL_GROUP)
        ug = u[:, sl]
        acc = ug
        for j in range(1, w):
            acc = acc + ext_ref[HALO - j:HALO - j + ts, sl]
        cnt = jnp.minimum(pos1, w).astype(F32)
        d = (acc / cnt - ug).astype(BF16)
        outs.append(jnp.dot(d, wp_ref[g], preferred_element_type=F32))
    y = jnp.concatenate(outs, axis=1) * ps_ref[...]
    po_ref[...] = y.astype(BF16)
    tail = u[ts - HALO:ts, :]
    ext_ref[0:HALO, :] = tail

    @pl.when(s == pl.num_programs(1) - 1)
    def _():
        ulast_ref[...] = tail


def _inproj(x, g_mix, w_in, w_pool_bf, pool_scale, ts):
    B, S, D = x.shape
    ns = S // ts
    pos = jnp.arange(S, dtype=jnp.int32)
    c, s_lo, s_hi = _rope_tables(pos)
    cos, sin = _rope_angles(pos)
    cos_t, sin_t = cos.T, sin.T
    w_uq = w_in[:, :POOL_WIDTH + ATTN_WIDTH].astype(BF16)
    w_kv_t = w_in[:, POOL_WIDTH + ATTN_WIDTH:].T.astype(BF16)
    ind = (np.arange(ts)[None, :] // MOBA_BLOCK == np.arange(SUBLANES)[:, None]).astype(np.float32)
    row = lambda b, s: (b, s, 0)
    col = lambda b, s: (b, 0, s)
    tab = pl.BlockSpec((ts, LANES), lambda b, s: (s, 0))
    tab_t = pl.BlockSpec((ROT_HALF, ts), lambda b, s: (0, s))
    full2 = lambda shape: pl.BlockSpec(shape, lambda b, s: (0, 0))
    act = lambda: pl.BlockSpec((None, ts, ATTN_WIDTH), row)
    act_t = lambda: pl.BlockSpec((None, ATTN_WIDTH, ts), col)
    out_shapes = (
        jax.ShapeDtypeStruct((B, ATTN_WIDTH, S), F32),
        jax.ShapeDtypeStruct((B, ATTN_WIDTH, S), F32),
        jax.ShapeDtypeStruct((B, S, ATTN_WIDTH), BF16),
        jax.ShapeDtypeStruct((B, ATTN_WIDTH, S), BF16),
        jax.ShapeDtypeStruct((B, 2 * ATTN_WIDTH, S), BF16),
        jax.ShapeDtypeStruct((B, S, POOL_WIDTH), BF16),
        jax.ShapeDtypeStruct((B, ns, SUBLANES, ATTN_WIDTH), F32),
        jax.ShapeDtypeStruct((B, HALO, POOL_WIDTH), F32),
    )
    return pl.pallas_call(
        _inproj_kernel,
        grid=(B, ns),
        in_specs=[
            pl.BlockSpec((None, ts, D), row),
            full2((1, D)),
            full2(w_uq.shape),
            full2(w_kv_t.shape),
            tab, tab, tab, tab_t, tab_t,
            full2(ind.shape),
            pl.BlockSpec(w_pool_bf.shape, lambda b, s: (0, 0, 0)),
            full2((1, POOL_WIDTH)),
        ],
        out_specs=(act_t(), act_t(), act(), act_t(), pl.BlockSpec((None, 2 * ATTN_WIDTH, ts), col), act(),
                   pl.BlockSpec((None, None, SUBLANES, ATTN_WIDTH), lambda b, s: (b, s, 0, 0)),
                   pl.BlockSpec((None, HALO, POOL_WIDTH), lambda b, s: (b, 0, 0))),
        out_shape=out_shapes,
        scratch_shapes=[pltpu.VMEM((HALO + ts, POOL_WIDTH), F32)],
        compiler_params=_cparams(("arbitrary", "arbitrary")),
        name="prompt_inproj",
    )(x, g_mix, w_uq, w_kv_t, c, s_lo, s_hi, cos_t, sin_t, jnp.asarray(ind), w_pool_bf, pool_scale)


def _moba_kernel(pt_ref, q_ref, k_ref, v_ref, e_ref, ksum_ref, qt_ref, cache_ref, o_ref, top_ref,
                 s_sc, buf_ref, sem_ref, qb_sc, g_sc, *, chunk, scan_nch, scan_nblk):
    n1, n2 = pl.num_programs(1), pl.num_programs(2)
    step = (pl.program_id(0) * n1 + pl.program_id(1)) * n2 + pl.program_id(2)
    _scan_step(step, pl.num_programs(0) * n1 * n2, scan_nch, pt_ref, qt_ref, cache_ref, top_ref,
               buf_ref, sem_ref, qb_sc, g_sc, scan_nblk)

    c = pl.program_id(2)
    tq = q_ref.shape[0]
    nb = ksum_ref.shape[0]
    blocks_per_chunk = chunk // MOBA_BLOCK
    n_tiles = chunk // LANES
    q = q_ref[...]
    kmean = (ksum_ref[...] * (1.0 / MOBA_BLOCK)).astype(BF16)
    lane = lax.broadcasted_iota(jnp.int32, (tq, LANES), 1)
    blk_iota = lax.broadcasted_iota(jnp.int32, (nb, tq), 0)

    qp = []
    for h in range(HEADS_PER_TILE):
        in_head = (lane >= h * HEAD_DIM) & (lane < (h + 1) * HEAD_DIM)
        qm = jnp.where(in_head, q, jnp.zeros_like(q))
        g = lax.dot_general(kmean, qm, NT, preferred_element_type=F32)
        g = jnp.where(blk_iota < c, g, NEG_INF)
        sel = blk_iota == c
        for t in range(MOBA_TOPK):
            m = jnp.max(g, axis=0, keepdims=True)
            idx = jnp.min(jnp.where(g == m, blk_iota, BIG_IDX), axis=0, keepdims=True)
            hit = blk_iota == idx
            sel = sel | (hit & (t < c))
            g = jnp.where(hit, NEG_INF, g)
        bias_t = jnp.where(sel, 0.0, NEG_BIG)
        bias_t = jnp.concatenate([bias_t, jnp.zeros((LANES - nb, tq), F32)], axis=0)
        bias = jnp.transpose(bias_t).astype(BF16)
        qp.append(jnp.concatenate([qm, bias], axis=1))

    def keys(off):
        return jnp.concatenate([k_ref[:, pl.ds(off, chunk)], e_ref[:, pl.ds(off, chunk)]], axis=0)

    def fold(fn, acc, x):
        for j in range(n_tiles):
            acc = fn(acc, x[:, j * LANES:(j + 1) * LANES])
        return acc

    n_past = lax.shift_right_logical(c, int(np.log2(blocks_per_chunk)))

    def pass1(i, mrun):
        off = pl.multiple_of(i * chunk, chunk)
        kk = keys(off)
        new = []
        for h in range(HEADS_PER_TILE):
            s = jnp.dot(qp[h], kk, preferred_element_type=F32)
            s_sc[h, :, pl.ds(off, chunk)] = s
            new.append(fold(jnp.maximum, mrun[h], s))
        return tuple(new)

    def pairwise(body, n, init):
        def run(base, count, c_):
            for k in range(count):
                c_ = body(base + k, c_)
            return c_

        carry = lax.fori_loop(0, lax.shift_right_logical(n, 2), lambda i, c_: run(4 * i, 4, c_), init)
        base = n & ~3
        carry = lax.cond((n & 2) != 0, lambda c_: run(base, 2, c_), lambda c_: c_, carry)
        return lax.cond((n & 1) != 0, lambda c_: run(n - 1, 1, c_), lambda c_: c_, carry)

    mrun = pairwise(pass1, n_past, tuple(jnp.full((tq, LANES), NEG_BIG, F32) for _ in range(HEADS_PER_TILE)))
    off = pl.multiple_of(n_past * chunk, chunk)
    kk = keys(off)
    kpos = off + lax.broadcasted_iota(jnp.int32, (tq, chunk), 1)
    qpos = c * MOBA_BLOCK + lax.broadcasted_iota(jnp.int32, (tq, chunk), 0)
    m_b = []
    for h in range(HEADS_PER_TILE):
        s = jnp.dot(qp[h], kk, preferred_element_type=F32)
        s = jnp.where(kpos <= qpos, s, NEG_BIG)
        s_sc[h, :, pl.ds(off, chunk)] = s
        m = jnp.max(fold(jnp.maximum, mrun[h], s), axis=1, keepdims=True)
        m_b.append(jnp.broadcast_to(m, (tq, LANES)))

    def pass2(i, accs):
        off = pl.multiple_of(i * chunk, chunk)
        new = []
        for h in range(HEADS_PER_TILE):
            s = s_sc[h, :, pl.ds(off, chunk)]
            pb = jnp.concatenate([jnp.exp2(s[:, j * LANES:(j + 1) * LANES] - m_b[h]).astype(BF16)
                                  for j in range(n_tiles)], axis=1)
            vv = v_ref[h * LANES:(h + 1) * LANES, pl.ds(off, chunk)]
            new.append(accs[h] + lax.dot_general(pb, vv, NT, preferred_element_type=F32))
        return tuple(new)

    acc0, acc1 = pairwise(pass2, n_past + 1, (jnp.zeros((tq, LANES), F32),) * HEADS_PER_TILE)
    out = jnp.where(lane < HEAD_DIM, acc0 / pltpu.roll(acc0, HEAD_DIM, axis=1),
                    pltpu.roll(acc1, HEAD_DIM, axis=1) / acc1)
    o_ref[...] = out.astype(o_ref.dtype)


def _moba_prompt(q_bf, kt_bf, vt_bf, ksum, page_table_flat, q_t, cache_kt, DB, n_pages):
    B, S, W = q_bf.shape
    nb = S // MOBA_BLOCK
    assert nb <= LANES and HEADS_PER_TILE == 2
    ntile = W // LANES
    chunk = min(S, 4 * MOBA_BLOCK)
    ind = (np.arange(S)[None, :] // MOBA_BLOCK == np.arange(LANES)[:, None]).astype(np.float32)
    _, cw, page = cache_kt.shape
    nsteps = B * ntile * nb
    pages_per_step, rem = divmod(DB * n_pages, nsteps)
    scan_nch, rem2 = divmod(n_pages, max(pages_per_step, 1))
    assert rem == 0 and rem2 == 0 and pages_per_step % (MOBA_BLOCK // page) == 0, "cache sweep does not tile"
    scan_nblk = n_pages * page // MOBA_BLOCK
    assert scan_nblk <= LANES

    blk = pl.BlockSpec((None, MOBA_BLOCK, LANES), lambda b, t, c, pt: (b, c, t))
    seq = pl.BlockSpec((None, LANES, S), lambda b, t, c, pt: (b, t, 0))
    seq_v = pl.BlockSpec((None, HEADS_PER_TILE * LANES, S), lambda b, t, c, pt: (b, t, 0))
    const = lambda shape: pl.BlockSpec(shape, lambda b, t, c, pt: (0, 0))
    sample_of_step = lambda b, t, c, pt: (((b * ntile + t) * nb + c) // scan_nch, 0, 0)
    return pl.pallas_call(
        functools.partial(_moba_kernel, chunk=chunk, scan_nch=scan_nch, scan_nblk=scan_nblk),
        grid_spec=pltpu.PrefetchScalarGridSpec(
            num_scalar_prefetch=1,
            grid=(B, ntile, nb),
            in_specs=[blk, seq, seq_v, const((LANES, S)),
                      pl.BlockSpec((None, nb, LANES), lambda b, t, c, pt: (b, 0, t)),
                      const(q_t.shape), pl.BlockSpec(memory_space=pl.ANY)],
            out_specs=(blk, pl.BlockSpec((None, N_HEADS, LANES), sample_of_step)),
            scratch_shapes=[pltpu.VMEM((HEADS_PER_TILE, MOBA_BLOCK, S), F32),
                            pltpu.VMEM((2, pages_per_step, cw, page), F32), pltpu.SemaphoreType.DMA((2,)),
                            pltpu.VMEM(q_t.shape, F32), pltpu.VMEM((N_HEADS * SUBLANES, LANES), F32)],
        ),
        out_shape=(jax.ShapeDtypeStruct((B, S, W), BF16), jax.ShapeDtypeStruct((DB, N_HEADS, LANES), jnp.int32)),
        compiler_params=_cparams(("arbitrary", "arbitrary", "arbitrary")),
        name="prompt_moba",
    )(page_table_flat, q_bf, kt_bf, vt_bf, jnp.asarray(ind, BF16), ksum, q_t, cache_kt)


def _route(logits):
    lane = lax.broadcasted_iota(jnp.int32, logits.shape, 1)
    is_grp = (lane >= N_EXPERTS) & (lane < N_EXPERTS + N_EXPERT_GROUPS)
    gl = jnp.where(is_grp, logits, NEG_INF)
    ge = jnp.exp(gl - jnp.max(gl, axis=1, keepdims=True))
    pg = ge / jnp.sum(ge, axis=1, keepdims=True)
    pg_sel = jnp.max(pg, axis=1, keepdims=True)
    g_sel = jnp.min(jnp.where(is_grp & (pg == pg_sel), lane - N_EXPERTS, BIG_IDX), axis=1, keepdims=True)
    in_grp = (lane < N_EXPERTS) & (jnp.right_shift(lane, 2) == g_sel)
    el = jnp.where(in_grp, logits, NEG_INF)
    ee = jnp.exp(el - jnp.max(el, axis=1, keepdims=True))
    pe = jnp.where(in_grp, ee / jnp.sum(ee, axis=1, keepdims=True), -1.0)
    p1 = jnp.max(pe, axis=1, keepdims=True)
    i1 = jnp.min(jnp.where(pe == p1, lane, BIG_IDX), axis=1, keepdims=True)
    pe2 = jnp.where(lane == i1, -1.0, pe)
    p2 = jnp.max(pe2, axis=1, keepdims=True)
    i2 = jnp.min(jnp.where(pe2 == p2, lane, BIG_IDX), axis=1, keepdims=True)
    tot = p1 + p2
    gate = (jnp.where(lane == i1, p1 / tot * pg_sel, 0.0)
            + jnp.where(lane == i2, p2 / tot * pg_sel, 0.0))
    return gate, (g_sel, i1, i2)


PAIRS_PER_GROUP = EXPERTS_PER_GROUP * (EXPERTS_PER_GROUP - 1) // 2
N_CLASSES = N_EXPERT_GROUPS * PAIRS_PER_GROUP
_PAIRS = [(a, b) for a in range(EXPERTS_PER_GROUP) for b in range(a + 1, EXPERTS_PER_GROUP)]
CLASS_EXPERTS = np.array([[g * EXPERTS_PER_GROUP + a, g * EXPERTS_PER_GROUP + b]
                          for g in range(N_EXPERT_GROUPS) for a, b in _PAIRS], np.int32)


def _route_class(g_sel, i1, i2):
    assert EXPERTS_PER_GROUP == 4
    base = g_sel * EXPERTS_PER_GROUP
    a = jnp.minimum(i1, i2) - base
    b = jnp.maximum(i1, i2) - base
    pair = jnp.right_shift(a * (5 - a), 1) + b - 1
    return g_sel * PAIRS_PER_GROUP + pair


def _gate_column(gate, e):
    lane = lax.broadcasted_iota(jnp.int32, gate.shape, 1)
    return jnp.sum(jnp.where(lane == e, gate, 0.0), axis=1, keepdims=True)


def _swiglu(x_bf, wg, wu, wd):
    hg = jnp.dot(x_bf, wg, preferred_element_type=F32)
    hu = jnp.dot(x_bf, wu, preferred_element_type=F32)
    h = hg * (1.0 / (1.0 + jnp.exp(-hg))) * hu
    return jnp.dot(h.astype(BF16), wd, preferred_element_type=F32)


CLASS_LANE = N_EXPERTS + N_EXPERT_GROUPS
RANK_LANE = CLASS_LANE + 1


def _outproj_kernel(x_ref, po_ref, ao_ref, w_ref, g_ref, wr_ref, tri_ref, h_ref, xg_ref, cnt_ref, meta_ref):
    i = pl.program_id(0)
    d = x_ref.shape[1]
    mix = (jnp.dot(po_ref[...], w_ref[0:POOL_WIDTH, :], preferred_element_type=F32)
           + jnp.dot(ao_ref[...], w_ref[POOL_WIDTH:, :], preferred_element_type=F32))
    h = x_ref[...] + mix
    h_ref[...] = h
    xn = _rms(h, g_ref[...])
    gate, picks = _route(jnp.dot(xn.astype(BF16), wr_ref[...], preferred_element_type=F32))
    cls = _route_class(*picks)

    @pl.when(i == 0)
    def _():
        cnt_ref[...] = jnp.zeros_like(cnt_ref)

    lane = lax.broadcasted_iota(jnp.int32, gate.shape, 1)
    onehot = jnp.where(lane == cls, 1.0, 0.0)
    before = jnp.dot(tri_ref[...], onehot.astype(BF16), preferred_element_type=F32) + cnt_ref[0:1, :]
    rank = jnp.sum(onehot * before, axis=1, keepdims=True)
    cnt_ref[...] = cnt_ref[...] + jnp.sum(onehot, axis=0, keepdims=True)
    info = jnp.where(lane == CLASS_LANE, cls.astype(F32), jnp.where(lane == RANK_LANE, rank, gate))
    xg_ref[:, 0:d] = xn
    xg_ref[:, d:] = info
    meta_ref[...] = jnp.transpose(info)[META_ROW0:META_ROW0 + SUBLANES, :]


META_ROW0 = CLASS_LANE // SUBLANES * SUBLANES


def _outproj(x2d, po2d, ao2d, w_out_bf, g_ffn, w_router_bf, tm):
    T, D = x2d.shape
    row = lambda i: (i, 0)
    full = lambda shape: pl.BlockSpec(shape, lambda i: (0, 0))
    tri = np.tril(np.ones((tm, tm), np.float32), -1)
    return pl.pallas_call(
        _outproj_kernel,
        grid=(T // tm,),
        in_specs=[pl.BlockSpec((tm, D), row), pl.BlockSpec((tm, POOL_WIDTH), row),
                  pl.BlockSpec((tm, ATTN_WIDTH), row), full(w_out_bf.shape), full((1, D)),
                  full(w_router_bf.shape), full((tm, tm))],
        out_specs=(pl.BlockSpec((tm, D), row), pl.BlockSpec((tm, D + LANES), row), full((SUBLANES, LANES)),
                   pl.BlockSpec((None, SUBLANES, tm), lambda i: (i, 0, 0))),
        out_shape=(jax.ShapeDtypeStruct((T, D), F32), jax.ShapeDtypeStruct((T, D + LANES), F32),
                   jax.ShapeDtypeStruct((SUBLANES, LANES), F32),
                   jax.ShapeDtypeStruct((T // tm, SUBLANES, tm), F32)),
        compiler_params=_cparams(("arbitrary",)),
        name="prompt_outproj",
    )(x2d, po2d, ao2d, w_out_bf, g_ffn, w_router_bf, jnp.asarray(tri, BF16))


MOE_TILE = 512
ROW_TILE = 1024
DMA_UNROLL = 8


def _dispatch_kernel(dest_ref, xg_ref, zeros_ref, xs_ref, sem_ref):
    del zeros_ref
    n = xg_ref.shape[0]

    def start(r, carry):
        pltpu.make_async_copy(xg_ref.at[pl.ds(r, 1)], xs_ref.at[pl.ds(dest_ref[0, r], 1)], sem_ref.at[0]).start()
        return carry

    lax.fori_loop(0, n, start, 0, unroll=DMA_UNROLL)
    pltpu.make_async_copy(xg_ref, xs_ref.at[pl.ds(0, n)], sem_ref.at[0]).wait()


def _dispatch(dest3d, xg, n_rows):
    T, W = xg.shape
    tm = dest3d.shape[2]
    return pl.pallas_call(
        _dispatch_kernel,
        grid=(T // tm,),
        in_specs=[pl.BlockSpec((None, 1, tm), lambda i: (i, 0, 0), memory_space=pltpu.SMEM),
                  pl.BlockSpec((tm, W), lambda i: (i, 0)),
                  pl.BlockSpec(memory_space=pl.ANY)],
        out_specs=pl.BlockSpec(memory_space=pl.ANY),
        out_shape=jax.ShapeDtypeStruct((n_rows, W), F32),
        scratch_shapes=[pltpu.SemaphoreType.DMA((1,))],
        input_output_aliases={2: 0},
        compiler_params=_cparams(("arbitrary",)),
        name="moe_dispatch",
    )(dest3d, xg, jnp.zeros((n_rows, W), F32))


def _moe_kernel(e1_ref, e2_ref, nused_ref, xs_ref, wg1_ref, wu1_ref, wd1_ref, wg2_ref, wu2_ref, wd2_ref, o_ref):
    i = pl.program_id(0)
    d = o_ref.shape[1]

    @pl.when(i < nused_ref[0])
    def _():
        x = xs_ref[:, 0:d].astype(BF16)
        gate = xs_ref[:, d:]
        o_ref[...] = (_gate_column(gate, e1_ref[i]) * _swiglu(x, wg1_ref[...], wu1_ref[...], wd1_ref[...])
                      + _gate_column(gate, e2_ref[i]) * _swiglu(x, wg2_ref[...], wu2_ref[...], wd2_ref[...]))

    @pl.when(i >= nused_ref[0])
    def _():
        o_ref[...] = jnp.zeros_like(o_ref)


def _moe(tile_e1, tile_e2, n_used, xs, wg_bf, wu_bf, wd_bf):
    n_rows, W = xs.shape
    E, D, DE = wg_bf.shape
    tile = lambda i, e1, e2, nu: (jnp.maximum(jnp.minimum(i, nu[0] - 1), 0), 0)
    first = lambda shape: pl.BlockSpec((None,) + shape, lambda i, e1, e2, nu: (e1[i], 0, 0))
    second = lambda shape: pl.BlockSpec((None,) + shape, lambda i, e1, e2, nu: (e2[i], 0, 0))
    return pl.pallas_call(
        _moe_kernel,
        grid_spec=pltpu.PrefetchScalarGridSpec(
            num_scalar_prefetch=3,
            grid=(n_rows // MOE_TILE,),
            in_specs=[pl.BlockSpec((MOE_TILE, W), tile),
                      first((D, DE)), first((D, DE)), first((DE, D)),
                      second((D, DE)), second((D, DE)), second((DE, D))],
            out_specs=pl.BlockSpec((MOE_TILE, D), lambda i, e1, e2, nu: (i, 0)),
        ),
        out_shape=jax.ShapeDtypeStruct((n_rows, D), F32),
        compiler_params=_cparams(("arbitrary",)),
        name="moe_experts",
    )(tile_e1, tile_e2, n_used, xs, wg_bf, wu_bf, wd_bf, wg_bf, wu_bf, wd_bf)


def _combine_copies_start(dest_ref, src_ref, buf_ref, sem_ref, slot):
    def start(r, carry):
        pltpu.make_async_copy(src_ref.at[pl.ds(dest_ref[0, r], 1)], buf_ref.at[slot, pl.ds(r, 1)],
                              sem_ref.at[slot]).start()
        return carry

    lax.fori_loop(0, buf_ref.shape[1], start, 0, unroll=DMA_UNROLL)


def _combine_kernel(dest_ref, dest_next_ref, h_ref, gf_ref, moe_ref, y_ref, buf_ref, sem_ref):
    i = pl.program_id(0)
    n = buf_ref.shape[1]
    slot = i % 2

    @pl.when(i == 0)
    def _():
        _combine_copies_start(dest_ref, moe_ref, buf_ref, sem_ref, slot)

    @pl.when(i + 1 < pl.num_programs(0))
    def _():
        _combine_copies_start(dest_next_ref, moe_ref, buf_ref, sem_ref, 1 - slot)

    pltpu.make_async_copy(moe_ref.at[pl.ds(0, n)], buf_ref.at[slot], sem_ref.at[slot]).wait()
    y_ref[...] = _rms(h_ref[...] + buf_ref[slot], gf_ref[...])


def _combine(dest3d, h2d, g_final, moe_sorted):
    T, D = h2d.shape
    nt, _, tm = dest3d.shape
    return pl.pallas_call(
        _combine_kernel,
        grid=(nt,),
        in_specs=[pl.BlockSpec((None, 1, tm), lambda i: (i, 0, 0), memory_space=pltpu.SMEM),
                  pl.BlockSpec((None, 1, tm), lambda i: (jnp.minimum(i + 1, nt - 1), 0, 0),
                               memory_space=pltpu.SMEM),
                  pl.BlockSpec((tm, D), lambda i: (i, 0)),
                  pl.BlockSpec((1, D), lambda i: (0, 0)),
                  pl.BlockSpec(memory_space=pl.ANY)],
        out_specs=pl.BlockSpec((tm, D), lambda i: (i, 0)),
        out_shape=jax.ShapeDtypeStruct((T, D), F32),
        scratch_shapes=[pltpu.VMEM((2, tm, D), F32), pltpu.SemaphoreType.DMA((2,))],
        compiler_params=_cparams(("arbitrary",)),
        name="moe_combine",
    )(dest3d, dest3d, h2d, g_final, moe_sorted)


def _moe_layout(meta, counts):
    cls = meta[:, CLASS_LANE - META_ROW0, :].reshape(-1).astype(jnp.int32)
    rank = meta[:, RANK_LANE - META_ROW0, :].reshape(-1).astype(jnp.int32)
    cnt = counts[0, :N_CLASSES].astype(jnp.int32)
    tiles = (cnt + MOE_TILE - 1) // MOE_TILE
    tile_end = jnp.cumsum(tiles)
    row_start = (tile_end - tiles) * MOE_TILE
    dest = rank
    for c in range(N_CLASSES):
        dest = dest + jnp.where(cls == c, row_start[c], 0)
    max_tiles = cls.shape[0] // MOE_TILE + N_CLASSES
    tile_ids = jnp.arange(max_tiles, dtype=jnp.int32)
    tile_class = jnp.minimum(jnp.sum((tile_ids[:, None] >= tile_end[None, :]).astype(jnp.int32), axis=1),
                             N_CLASSES - 1)
    experts = jnp.asarray(CLASS_EXPERTS)[tile_class]
    return dest, experts[:, 0], experts[:, 1], tile_end[-1:].astype(jnp.int32), max_tiles * MOE_TILE


def _sample_inproj_kernel(x_ref, g_ref, w_ref, c_ref, slo_ref, shi_ref, sp_ref, wp_ref, ps_ref,
                          u_ref, k_ref, v_ref, po_ref, qt_ref, kt_ref, vt_ref):
    xn = _rms(x_ref[...], g_ref[...])

    def proj(i):
        return jnp.dot(xn, w_ref[:, i * POOL_WIDTH:(i + 1) * POOL_WIDTH],
                       preferred_element_type=F32, precision=HIGHEST)

    u, q, k, v = proj(0), proj(1), proj(2), proj(3)
    c, s_lo, s_hi = c_ref[...], slo_ref[...], shi_ref[...]
    q = _rotate(q, c, s_lo, s_hi)
    k = _rotate(k, c, s_lo, s_hi)
    u_ref[...] = u
    k_ref[...] = k
    v_ref[...] = v
    qt_ref[...] = jnp.transpose(q)
    kt_ref[...] = jnp.transpose(k)
    vt_ref[...] = jnp.transpose(v)
    db = sp_ref.shape[1]
    outs = []
    for g, w in enumerate(POOL_WINDOWS):
        sl = slice(g * POOL_GROUP, (g + 1) * POOL_GROUP)
        ug = u[0:db, sl]
        acc = ug
        for j in range(1, w):
            acc = acc + sp_ref[POOL_HIST - j, :, sl]
        d = acc / float(w) - ug
        outs.append(jnp.dot(d, wp_ref[g], preferred_element_type=F32, precision=HIGHEST))
    po_ref[...] = jnp.zeros_like(po_ref)
    po_ref[0:db, :] = jnp.concatenate(outs, axis=1) * ps_ref[...]


def _sample_inproj(x_pad, g_mix, w_in, pos, sp_t, w_pool, pool_scale):
    R, D = x_pad.shape
    c, s_lo, s_hi = _rope_tables(pos)
    row = jax.ShapeDtypeStruct((R, POOL_WIDTH), F32)
    col = jax.ShapeDtypeStruct((POOL_WIDTH, R), F32)
    return pl.pallas_call(
        _sample_inproj_kernel,
        out_shape=(row, row, row, row, col, col, col),
        compiler_params=pltpu.CompilerParams(vmem_limit_bytes=VMEM_LIMIT),
        name="sample_inproj",
    )(x_pad, g_mix, w_in, c, s_lo, s_hi, sp_t, w_pool, pool_scale)


def _column(mat, b):
    lane = lax.broadcasted_iota(jnp.int32, mat.shape, 1)
    col = jnp.sum(jnp.where(lane == b, mat, 0.0), axis=1, keepdims=True)
    return jnp.broadcast_to(col, mat.shape)


def _scan_copies(pt_ref, cache_ref, buf_ref, sem_ref, step, slot):
    n = buf_ref.shape[1]
    return [pltpu.make_async_copy(cache_ref.at[pt_ref[step * n + i]], buf_ref.at[slot, i], sem_ref.at[slot])
            for i in range(n)]


def _scan_step(step, nsteps, nch, pt_ref, qt_ref, cache_ref, top_ref, buf_ref, sem_ref, qb_sc, g_sc, nblk):
    b = step // nch
    ch = step - b * nch
    slot = step % 2

    @pl.when(step == 0)
    def _():
        for cp in _scan_copies(pt_ref, cache_ref, buf_ref, sem_ref, step, slot):
            cp.start()

    @pl.when(step + 1 < nsteps)
    def _():
        for cp in _scan_copies(pt_ref, cache_ref, buf_ref, sem_ref, step + 1, 1 - slot):
            cp.start()

    @pl.when(ch == 0)
    def _():
        qb_sc[...] = _column(qt_ref[...], b)
        g_sc[...] = jnp.zeros_like(g_sc)

    for cp in _scan_copies(pt_ref, cache_ref, buf_ref, sem_ref, step, slot):
        cp.wait()
    qb = qb_sc[...]
    lane = lax.broadcasted_iota(jnp.int32, (N_HEADS, LANES), 1)
    pages_per_blk = MOBA_BLOCK // buf_ref.shape[3]
    blks_per_step = buf_ref.shape[1] // pages_per_blk
    lane8 = lax.broadcasted_iota(jnp.int32, (SUBLANES, LANES), 1)
    g = [g_sc[h * SUBLANES:(h + 1) * SUBLANES, :] for h in range(N_HEADS)]
    for j in range(blks_per_step):
        tot = buf_ref[slot, j * pages_per_blk]
        for i in range(1, pages_per_blk):
            tot = tot + buf_ref[slot, j * pages_per_blk + i]
        pr = tot * qb
        for h in range(N_HEADS):
            a = pr[h * HEAD_DIM:h * HEAD_DIM + SUBLANES]
            for r in range(h * HEAD_DIM + SUBLANES, (h + 1) * HEAD_DIM, SUBLANES):
                a = a + pr[r:r + SUBLANES]
            g[h] = jnp.where(lane8 == ch * blks_per_step + j, jnp.sum(a, axis=1, keepdims=True), g[h])
    for h in range(N_HEADS):
        g_sc[h * SUBLANES:(h + 1) * SUBLANES, :] = g[h]

    @pl.when(ch == nch - 1)
    def _():
        full = jnp.concatenate([jnp.sum(gh, axis=0, keepdims=True) for gh in g], axis=0)
        gg = jnp.where(lane < nblk, full, NEG_INF)
        top = jnp.zeros((N_HEADS, LANES), jnp.int32)
        for t in range(MOBA_TOPK):
            m = jnp.max(gg, axis=1, keepdims=True)
            idx = jnp.min(jnp.where(gg == m, lane, BIG_IDX), axis=1, keepdims=True)
            top = jnp.where(lane == t, idx, top)
            gg = jnp.where(lane == idx, NEG_INF, gg)
        top_ref[...] = top


def _attn_copies(pt_ref, top_ref, ck_ref, cv_ref, kbuf, vbuf, sem_ref, b, slot, n_pages, pages_per_blk):
    cps = []
    for h in range(N_HEADS):
        rows = pl.ds(h * HEAD_DIM, HEAD_DIM)
        for t in range(MOBA_TOPK):
            blk = top_ref[(b * N_HEADS + h) * MOBA_TOPK + t]
            for i in range(pages_per_blk):
                phys = pt_ref[b * n_pages + blk * pages_per_blk + i]
                j = t * pages_per_blk + i
                cps.append(pltpu.make_async_copy(ck_ref.at[phys, rows, :], kbuf.at[slot, h, j], sem_ref.at[slot]))
                cps.append(pltpu.make_async_copy(cv_ref.at[phys, rows, :], vbuf.at[slot, h, j], sem_ref.at[slot]))
    return cps


def _sample_attn_kernel(pt_ref, top_ref, qt_ref, kt_ref, vt_ref, ck_ref, cv_ref, o_ref, kbuf, vbuf, sem_ref,
                        *, n_pages):
    b = pl.program_id(0)
    nb = pl.num_programs(0)
    slot = b % 2
    n_sel = kbuf.shape[2]
    copies = functools.partial(_attn_copies, pt_ref, top_ref, ck_ref, cv_ref, kbuf, vbuf, sem_ref,
                               n_pages=n_pages, pages_per_blk=n_sel // MOBA_TOPK)

    @pl.when(b == 0)
    def _():
        o_ref[...] = jnp.zeros_like(o_ref)
        for cp in copies(b=b, slot=slot):
            cp.start()

    @pl.when(b + 1 < nb)
    def _():
        for cp in copies(b=b + 1, slot=1 - slot):
            cp.start()

    for cp in copies(b=b, slot=slot):
        cp.wait()

    scale = HEAD_DIM ** -0.5
    lane = lax.broadcasted_iota(jnp.int32, (HEAD_DIM, LANES), 1)
    for h in range(N_HEADS):
        rows = slice(h * HEAD_DIM, (h + 1) * HEAD_DIM)
        qb = _column(qt_ref[rows, :], b)
        kb = _column(kt_ref[rows, :], b)
        vb = _column(vt_ref[rows, :], b)
        s = [jnp.sum(kbuf[slot, h, j] * qb, axis=0, keepdims=True) * scale for j in range(n_sel)]
        s_own = jnp.sum(kb * qb, axis=0, keepdims=True) * scale
        m = s_own
        for sj in s:
            m = jnp.maximum(m, jnp.max(sj, axis=1, keepdims=True))
        p = [jnp.exp(sj - m) for sj in s]
        p_own = jnp.exp(s_own - m)
        l = p_own
        acc = jnp.zeros((HEAD_DIM, LANES), F32)
        for j, pj in enumerate(p):
            l = l + jnp.sum(pj, axis=1, keepdims=True)
            acc = acc + vbuf[slot, h, j] * pj
        col = (jnp.sum(acc, axis=1, keepdims=True) + p_own * vb) / l
        o_ref[rows, :] = jnp.where(lane == b, col, o_ref[rows, :])


def _sample_attn(page_table_flat, top_flat, q_t, k_t, v_t, cache_kt, cache_vt, DB, n_pages):
    _, W, page = cache_kt.shape
    n_sel = MOBA_TOPK * MOBA_BLOCK // page
    vm = lambda: pl.BlockSpec(q_t.shape, lambda b, pt, tp: (0, 0))
    hbm = lambda: pl.BlockSpec(memory_space=pl.ANY)
    return pl.pallas_call(
        functools.partial(_sample_attn_kernel, n_pages=n_pages),
        grid_spec=pltpu.PrefetchScalarGridSpec(
            num_scalar_prefetch=2,
            grid=(DB,),
            in_specs=[vm(), vm(), vm(), hbm(), hbm()],
            out_specs=vm(),
            scratch_shapes=[pltpu.VMEM((2, N_HEADS, n_sel, HEAD_DIM, page), F32),
                            pltpu.VMEM((2, N_HEADS, n_sel, HEAD_DIM, page), F32),
                            pltpu.SemaphoreType.DMA((2,))],
        ),
        out_shape=jax.ShapeDtypeStruct(q_t.shape, F32),
        compiler_params=_cparams(("arbitrary",)),
        name="sample_attn",
    )(page_table_flat, top_flat, q_t, k_t, v_t, cache_kt, cache_vt)


def _sample_tail_kernel(x_ref, po_ref, aot_ref, w_ref, g_ref, wr_ref, wg_ref, wu_ref, wd_ref, gf_ref,
                        y_ref, h_sc, xn_sc, gate_sc, acc_sc):
    e = pl.program_id(0)

    @pl.when(e == 0)
    def _():
        ao = jnp.transpose(aot_ref[...])
        mix = (jnp.dot(po_ref[...], w_ref[0:POOL_WIDTH, :], preferred_element_type=F32, precision=HIGHEST)
               + jnp.dot(ao, w_ref[POOL_WIDTH:, :], preferred_element_type=F32, precision=HIGHEST))
        h = x_ref[...] + mix
        h_sc[...] = h
        xn = _rms(h, g_ref[...])
        xn_sc[...] = xn
        gate_sc[...] = _route(jnp.dot(xn, wr_ref[...], preferred_element_type=F32, precision=HIGHEST))[0]
        acc_sc[...] = jnp.zeros_like(acc_sc)

    o = _swiglu(xn_sc[...].astype(BF16), wg_ref[...], wu_ref[...], wd_ref[...])
    acc_sc[...] += _gate_column(gate_sc[...], e) * o

    @pl.when(e == pl.num_programs(0) - 1)
    def _():
        y_ref[...] = _rms(h_sc[...] + acc_sc[...], gf_ref[...])


def _sample_tail(x_pad, po, ao_t, w_out, g_ffn, w_router, wg_bf, wu_bf, wd_bf, g_final):
    R, D = x_pad.shape
    E, _, DE = wg_bf.shape
    full = lambda shape: pl.BlockSpec(shape, lambda e: (0,) * len(shape))
    return pl.pallas_call(
        _sample_tail_kernel,
        grid=(E,),
        in_specs=[full((R, D)), full((R, POOL_WIDTH)), full((ATTN_WIDTH, R)), full(w_out.shape),
                  full((1, D)), full(w_router.shape),
                  pl.BlockSpec((None, D, DE), lambda e: (e, 0, 0)),
                  pl.BlockSpec((None, D, DE), lambda e: (e, 0, 0)),
                  pl.BlockSpec((None, DE, D), lambda e: (e, 0, 0)),
                  full((1, D))],
        out_specs=full((R, D)),
        out_shape=jax.ShapeDtypeStruct((R, D), F32),
        scratch_shapes=[pltpu.VMEM((R, D), F32), pltpu.VMEM((R, D), F32), pltpu.VMEM((R, LANES), F32),
                        pltpu.VMEM((R, D), F32)],
        compiler_params=_cparams(("arbitrary",)),
        name="sample_tail",
    )(x_pad, po, ao_t, w_out, g_ffn, w_router, wg_bf, wu_bf, wd_bf, g_final)


def _tile(n, pref):
    while n % pref:
        pref //= 2
    return pref


def kernel(x_prompt, x_sample, cache_k, cache_v, state_pool, page_table, norm_mix, w_in, w_pool, pool_scale,
           w_out, norm_ffn, w_router_group, w_router_expert, w_gate, w_up, w_down, norm_final):
    B, S, D = x_prompt.shape
    DB, DS, _ = x_sample.shape
    depth, n_phys, page = cache_k.shape[:3]
    assert depth == 1 and DS == 1, "single layer, one new token per sequence"
    n_pages = page_table.shape[1]
    past_len = n_pages * page
    assert S % MOBA_BLOCK == 0 and past_len % MOBA_BLOCK == 0 and MOBA_BLOCK % page == 0
    assert past_len // MOBA_BLOCK >= MOBA_TOPK and DB <= LANES
    T = B * S

    g_mix = norm_mix[0][None, :]
    g_ffn = norm_ffn[0][None, :]
    g_final = norm_final[None, :]
    ps = pool_scale[0][None, :]
    w_router = jnp.concatenate(
        [w_router_expert[0], w_router_group[0],
         jnp.zeros((D, LANES - N_EXPERTS - N_EXPERT_GROUPS), F32)], axis=1)
    wg_bf, wu_bf, wd_bf = w_gate[0].astype(BF16), w_up[0].astype(BF16), w_down[0].astype(BF16)

    xs = x_sample.reshape(DB, D)
    x_pad = jnp.pad(xs, ((0, LANES - DB), (0, 0)))
    pos_s = jnp.full((1,), past_len, jnp.int32)
    sp_t = jnp.swapaxes(state_pool[0], 0, 1)
    u_s, k_s, v_s, po_s, q_t, k_t, v_t = _sample_inproj(x_pad, g_mix, w_in[0], pos_s, sp_t, w_pool[0], ps)
    cache_kt = jnp.transpose(cache_k[0], (0, 2, 3, 1)).reshape(n_phys, ATTN_WIDTH, page)
    cache_vt = jnp.transpose(cache_v[0], (0, 2, 3, 1)).reshape(n_phys, ATTN_WIDTH, page)
    pt_flat = page_table.reshape(-1).astype(jnp.int32)

    ts = _tile(S, 512)
    kt_f, vt_f, q_bf, kt_bf, vt_bf, po, ksum, u_last = _inproj(
        x_prompt, g_mix, w_in[0], w_pool[0].astype(BF16), ps, ts)
    ksum = ksum[:, :, :ts // MOBA_BLOCK].reshape(B, S // MOBA_BLOCK, ATTN_WIDTH)
    ao, top = _moba_prompt(q_bf, kt_bf, vt_bf, ksum, pt_flat, q_t, cache_kt, DB, n_pages)
    h2, xg, counts, meta = _outproj(x_prompt.reshape(T, D), po.reshape(T, POOL_WIDTH), ao.reshape(T, ATTN_WIDTH),
                                    w_out[0].astype(BF16), g_ffn, w_router.astype(BF16), _tile(T, 512))
    dest, tile_e1, tile_e2, n_used, n_rows = _moe_layout(meta, counts)
    dest3d = dest.reshape(T // ROW_TILE, 1, ROW_TILE)
    x_grouped = _dispatch(dest3d, xg, n_rows)
    moe_sorted = _moe(tile_e1, tile_e2, n_used, x_grouped, wg_bf, wu_bf, wd_bf)
    y_prompt = _combine(dest3d, h2, g_final, moe_sorted).reshape(B, S, D)
    k_prompt = jnp.transpose(kt_f.reshape(B, N_HEADS, HEAD_DIM, S), (0, 3, 1, 2))[None]
    v_prompt = jnp.transpose(vt_f.reshape(B, N_HEADS, HEAD_DIM, S), (0, 3, 1, 2))[None]
    pool_prompt = u_last[None, :, HALO - POOL_HIST:, :]

    top_flat = top[:, :, :MOBA_TOPK].reshape(-1)
    ao_t = _sample_attn(pt_flat, top_flat, q_t, k_t, v_t, cache_kt, cache_vt, DB, n_pages)
    y_pad = _sample_tail(x_pad, po_s, ao_t, w_out[0], g_ffn, w_router, wg_bf, wu_bf, wd_bf, g_final)
    y_sample = y_pad[:DB].reshape(DB, 1, D)
    k_sample = k_s[:DB].reshape(1, DB, 1, N_HEADS, HEAD_DIM)
    v_sample = v_s[:DB].reshape(1, DB, 1, N_HEADS, HEAD_DIM)
    pool_sample = jnp.concatenate([state_pool[0][:, 1:], u_s[:DB, None, :]], axis=1)[None]
    return (y_prompt, y_sample, k_prompt, v_prompt, pool_prompt, k_sample, v_sample, pool_sample)
```

```python
import functools

import jax
import jax.numpy as jnp
import numpy as np
from jax import lax
from jax.experimental import pallas as pl
from jax.experimental.pallas import tpu as pltpu

F32 = jnp.float32
BF16 = jnp.bfloat16
HIGHEST = lax.Precision.HIGHEST

POOL_WINDOWS = (2, 4, 8, 16)
POOL_GROUP = 128
POOL_WIDTH = POOL_GROUP * len(POOL_WINDOWS)
POOL_HIST = max(POOL_WINDOWS) - 1
HALO = POOL_HIST + 1
HEAD_DIM = 64
N_HEADS = 8
ATTN_WIDTH = N_HEADS * HEAD_DIM
ROT_DIM = HEAD_DIM // 4
ROT_HALF = ROT_DIM // 2
ROPE_THETA = 500000.0
MOBA_BLOCK = 256
MOBA_TOPK = 3
N_EXPERT_GROUPS = 4
EXPERTS_PER_GROUP = 4
N_EXPERTS = N_EXPERT_GROUPS * EXPERTS_PER_GROUP
RMS_EPS = 1e-6

LANES = 128
SUBLANES = 8
HEADS_PER_TILE = LANES // HEAD_DIM
VMEM_LIMIT = 52 * 1024 * 1024

NEG_INF = float("-inf")
NEG_BIG = -1e30
BIG_IDX = 1 << 20
NT = (((1,), (1,)), ((), ()))
Q_SCALE = HEAD_DIM ** -0.5 * float(np.log2(np.e))


def _cparams(sem):
    return pltpu.CompilerParams(dimension_semantics=sem, vmem_limit_bytes=VMEM_LIMIT)


def _rms(x, g):
    r = lax.rsqrt(jnp.mean(x * x, axis=-1, keepdims=True) + RMS_EPS)
    return x * r * g


def _rope_angles(pos):
    inv = ROPE_THETA ** (-(jnp.arange(0, ROT_DIM, 2, dtype=F32) / ROT_DIM))
    ang = pos.astype(F32)[:, None] * inv[None, :]
    return jnp.cos(ang), jnp.sin(ang)


def _rope_tables(pos):
    cos, sin = _rope_angles(pos)
    d = np.arange(LANES) % HEAD_DIM
    fi = d % ROT_HALF
    cos_l, sin_l = cos[:, fi], sin[:, fi]
    c = jnp.where(d < ROT_DIM, cos_l, 1.0)
    s_lo = jnp.where(d < ROT_HALF, -sin_l, 0.0)
    s_hi = jnp.where((d >= ROT_HALF) & (d < ROT_DIM), sin_l, 0.0)
    return c.astype(F32), s_lo.astype(F32), s_hi.astype(F32)


def _rotate(x, c, s_lo, s_hi):
    outs = []
    for j in range(x.shape[1] // LANES):
        t = x[:, j * LANES:(j + 1) * LANES]
        up = pltpu.roll(t, LANES - ROT_HALF, axis=1)
        dn = pltpu.roll(t, ROT_HALF, axis=1)
        outs.append(t * c + up * s_lo + dn * s_hi)
    return jnp.concatenate(outs, axis=1)


def _rotate_t(xt, cos_t, sin_t):
    pieces = []
    for h in range(N_HEADS):
        r0 = h * HEAD_DIM
        x1 = xt[r0:r0 + ROT_HALF]
        x2 = xt[r0 + ROT_HALF:r0 + ROT_DIM]
        pieces += [x1 * cos_t - x2 * sin_t, x2 * cos_t + x1 * sin_t, xt[r0 + ROT_DIM:r0 + HEAD_DIM]]
    return jnp.concatenate(pieces, axis=0)


def _inproj_kernel(x_ref, g_ref, wuq_ref, wkv_ref, c_ref, slo_ref, shi_ref, ct_ref, st_ref, ind_ref,
                   wp_ref, ps_ref,
                   kf_ref, vf_ref, qb_ref, kb_ref, vb_ref, po_ref, ksum_ref, ulast_ref, ext_ref):
    s = pl.program_id(1)
    ts = x_ref.shape[0]
    xn = _rms(x_ref[...], g_ref[...]).astype(BF16)
    u = jnp.dot(xn, wuq_ref[:, 0:POOL_WIDTH], preferred_element_type=F32)
    q = jnp.dot(xn, wuq_ref[:, POOL_WIDTH:], preferred_element_type=F32)
    kt = lax.dot_general(wkv_ref[0:ATTN_WIDTH, :], xn, NT, preferred_element_type=F32)
    vt = lax.dot_general(wkv_ref[ATTN_WIDTH:, :], xn, NT, preferred_element_type=F32)
    q = _rotate(q, c_ref[...], slo_ref[...], shi_ref[...])
    kt = _rotate_t(kt, ct_ref[...], st_ref[...])
    kf_ref[...] = kt
    vf_ref[...] = vt
    qb_ref[...] = (q * Q_SCALE).astype(BF16)
    kb_ref[...] = kt.astype(BF16)
    ones = jnp.ones((HEAD_DIM, ts), BF16)
    vtb = vt.astype(BF16)
    vb_ref[...] = jnp.concatenate(
        [piece for h in range(N_HEADS) for piece in (vtb[h * HEAD_DIM:(h + 1) * HEAD_DIM], ones)], axis=0)
    ksum_ref[...] = lax.dot_general(ind_ref[...], kt, NT, preferred_element_type=F32, precision=HIGHEST)

    @pl.when(s == 0)
    def _():
        ext_ref[0:HALO, :] = jnp.zeros((HALO, POOL_WIDTH), F32)

    ext_ref[HALO:HALO + ts, :] = u
    pos1 = s * ts + lax.broadcasted_iota(jnp.int32, (ts, POOL_GROUP), 0) + 1
    outs = []
    for g, w in enumerate(POOL_WINDOWS):
        sl = slice(g * POOL_GROUP, (g + 1) * POOL_GROUP)
        ug = u[:, sl]
        acc = ug
        for j in range(1, w):
            acc = acc + ext_ref[HALO - j:HALO - j + ts, sl]
        cnt = jnp.minimum(pos1, w).astype(F32)
        d = (acc / cnt - ug).astype(BF16)
        outs.append(jnp.dot(d, wp_ref[g], preferred_element_type=F32))
    y = jnp.concatenate(outs, axis=1) * ps_ref[...]
    po_ref[...] = y.astype(BF16)
    tail = u[ts - HALO:ts, :]
    ext_ref[0:HALO, :] = tail

    @pl.when(s == pl.num_programs(1) - 1)
    def _():
        ulast_ref[...] = tail


def _inproj(x, g_mix, w_in, w_pool_bf, pool_scale, ts):
    B, S, D = x.shape
    ns = S // ts
    pos = jnp.arange(S, dtype=jnp.int32)
    c, s_lo, s_hi = _rope_tables(pos)
    cos, sin = _rope_angles(pos)
    cos_t, sin_t = cos.T, sin.T
    w_uq = w_in[:, :POOL_WIDTH + ATTN_WIDTH].astype(BF16)
    w_kv_t = w_in[:, POOL_WIDTH + ATTN_WIDTH:].T.astype(BF16)
    ind = (np.arange(ts)[None, :] // MOBA_BLOCK == np.arange(SUBLANES)[:, None]).astype(np.float32)
    row = lambda b, s: (b, s, 0)
    col = lambda b, s: (b, 0, s)
    tab = pl.BlockSpec((ts, LANES), lambda b, s: (s, 0))
    tab_t = pl.BlockSpec((ROT_HALF, ts), lambda b, s: (0, s))
    full2 = lambda shape: pl.BlockSpec(shape, lambda b, s: (0, 0))
    act = lambda: pl.BlockSpec((None, ts, ATTN_WIDTH), row)
    act_t = lambda: pl.BlockSpec((None, ATTN_WIDTH, ts), col)
    out_shapes = (
        jax.ShapeDtypeStruct((B, ATTN_WIDTH, S), F32),
        jax.ShapeDtypeStruct((B, ATTN_WIDTH, S), F32),
        jax.ShapeDtypeStruct((B, S, ATTN_WIDTH), BF16),
        jax.ShapeDtypeStruct((B, ATTN_WIDTH, S), BF16),
        jax.ShapeDtypeStruct((B, 2 * ATTN_WIDTH, S), BF16),
        jax.ShapeDtypeStruct((B, S, POOL_WIDTH), BF16),
        jax.ShapeDtypeStruct((B, ns, SUBLANES, ATTN_WIDTH), F32),
        jax.ShapeDtypeStruct((B, HALO, POOL_WIDTH), F32),
    )
    return pl.pallas_call(
        _inproj_kernel,
        grid=(B, ns),
        in_specs=[
            pl.BlockSpec((None, ts, D), row),
            full2((1, D)),
            full2(w_uq.shape),
            full2(w_kv_t.shape),
            tab, tab, tab, tab_t, tab_t,
            full2(ind.shape),
            pl.BlockSpec(w_pool_bf.shape, lambda b, s: (0, 0, 0)),
            full2((1, POOL_WIDTH)),
        ],
        out_specs=(act_t(), act_t(), act(), act_t(), pl.BlockSpec((None, 2 * ATTN_WIDTH, ts), col), act(),
                   pl.BlockSpec((None, None, SUBLANES, ATTN_WIDTH), lambda b, s: (b, s, 0, 0)),
                   pl.BlockSpec((None, HALO, POOL_WIDTH), lambda b, s: (b, 0, 0))),
        out_shape=out_shapes,
        scratch_shapes=[pltpu.VMEM((HALO + ts, POOL_WIDTH), F32)],
        compiler_params=_cparams(("arbitrary", "arbitrary")),
        name="prompt_inproj",
    )(x, g_mix, w_uq, w_kv_t, c, s_lo, s_hi, cos_t, sin_t, jnp.asarray(ind), w_pool_bf, pool_scale)


def _moba_kernel(pt_ref, q_ref, k_ref, v_ref, e_ref, ksum_ref, qt_ref, cache_ref, o_ref, top_ref,
                 s_sc, buf_ref, sem_ref, qb_sc, g_sc, *, chunk, scan_nch, scan_nblk):
    n1, n2 = pl.num_programs(1), pl.num_programs(2)
    step = (pl.program_id(0) * n1 + pl.program_id(1)) * n2 + pl.program_id(2)
    _scan_step(step, pl.num_programs(0) * n1 * n2, scan_nch, pt_ref, qt_ref, cache_ref, top_ref,
               buf_ref, sem_ref, qb_sc, g_sc, scan_nblk)

    c = pl.program_id(2)
    tq = q_ref.shape[0]
    nb = ksum_ref.shape[0]
    blocks_per_chunk = chunk // MOBA_BLOCK
    n_tiles = chunk // LANES
    q = q_ref[...]
    kmean = (ksum_ref[...] * (1.0 / MOBA_BLOCK)).astype(BF16)
    lane = lax.broadcasted_iota(jnp.int32, (tq, LANES), 1)
    blk_iota = lax.broadcasted_iota(jnp.int32, (nb, tq), 0)

    qp = []
    for h in range(HEADS_PER_TILE):
        in_head = (lane >= h * HEAD_DIM) & (lane < (h + 1) * HEAD_DIM)
        qm = jnp.where(in_head, q, jnp.zeros_like(q))
        g = lax.dot_general(kmean, qm, NT, preferred_element_type=F32)
        g = jnp.where(blk_iota < c, g, NEG_INF)
        sel = blk_iota == c
        for t in range(MOBA_TOPK):
            m = jnp.max(g, axis=0, keepdims=True)
            idx = jnp.min(jnp.where(g == m, blk_iota, BIG_IDX), axis=0, keepdims=True)
            hit = blk_iota == idx
            sel = sel | (hit & (t < c))
            g = jnp.where(hit, NEG_INF, g)
        bias_t = jnp.where(sel, 0.0, NEG_BIG)
        bias_t = jnp.concatenate([bias_t, jnp.zeros((LANES - nb, tq), F32)], axis=0)
        bias = jnp.transpose(bias_t).astype(BF16)
        qp.append(jnp.concatenate([qm, bias], axis=1))

    def keys(off):
        return jnp.concatenate([k_ref[:, pl.ds(off, chunk)], e_ref[:, pl.ds(off, chunk)]], axis=0)

    def fold(fn, acc, x):
        for j in range(n_tiles):
            acc = fn(acc, x[:, j * LANES:(j + 1) * LANES])
        return acc

    n_past = lax.shift_right_logical(c, int(np.log2(blocks_per_chunk)))

    def pass1(i, mrun):
        off = pl.multiple_of(i * chunk, chunk)
        kk = keys(off)
        new = []
        for h in range(HEADS_PER_TILE):
            s = jnp.dot(qp[h], kk, preferred_element_type=F32)
            s_sc[h, :, pl.ds(off, chunk)] = s
            new.append(fold(jnp.maximum, mrun[h], s))
        return tuple(new)

    def pairwise(body, n, init):
        def run(base, count, c_):
            for k in range(count):
                c_ = body(base + k, c_)
            return c_

        carry = lax.fori_loop(0, lax.shift_right_logical(n, 2), lambda i, c_: run(4 * i, 4, c_), init)
        base = n & ~3
        carry = lax.cond((n & 2) != 0, lambda c_: run(base, 2, c_), lambda c_: c_, carry)
        return lax.cond((n & 1) != 0, lambda c_: run(n - 1, 1, c_), lambda c_: c_, carry)

    off = pl.multiple_of(n_past * chunk, chunk)
    kk = keys(off)
    kpos = off + lax.broadcasted_iota(jnp.int32, (tq, chunk), 1)
    qpos = c * MOBA_BLOCK + lax.broadcasted_iota(jnp.int32, (tq, chunk), 0)
    mrun = []
    for h in range(HEADS_PER_TILE):
        s = jnp.dot(qp[h], kk, preferred_element_type=F32)
        s = jnp.where(kpos <= qpos, s, NEG_BIG)
        s_sc[h, :, pl.ds(off, chunk)] = s
        mrun.append(fold(jnp.maximum, jnp.full((tq, LANES), NEG_BIG, F32), s))
    mrun = pairwise(pass1, n_past, tuple(mrun))
    m_b = [jnp.broadcast_to(jnp.max(mrun[h], axis=1, keepdims=True), (tq, LANES)) for h in range(HEADS_PER_TILE)]

    def pass2(i, accs):
        off = pl.multiple_of(i * chunk, chunk)
        new = []
        for h in range(HEADS_PER_TILE):
            s = s_sc[h, :, pl.ds(off, chunk)]
            pb = jnp.concatenate([jnp.exp2(s[:, j * LANES:(j + 1) * LANES] - m_b[h]).astype(BF16)
                                  for j in range(n_tiles)], axis=1)
            vv = v_ref[h * LANES:(h + 1) * LANES, pl.ds(off, chunk)]
            new.append(accs[h] + lax.dot_general(pb, vv, NT, preferred_element_type=F32))
        return tuple(new)

    acc0, acc1 = pairwise(pass2, n_past + 1, (jnp.zeros((tq, LANES), F32),) * HEADS_PER_TILE)
    out = jnp.where(lane < HEAD_DIM, acc0 / pltpu.roll(acc0, HEAD_DIM, axis=1),
                    pltpu.roll(acc1, HEAD_DIM, axis=1) / acc1)
    o_ref[...] = out.astype(o_ref.dtype)


def _moba_prompt(q_bf, kt_bf, vt_bf, ksum, page_table_flat, q_t, cache_kt, DB, n_pages):
    B, S, W = q_bf.shape
    nb = S // MOBA_BLOCK
    assert nb <= LANES and HEADS_PER_TILE == 2
    ntile = W // LANES
    chunk = min(S, 4 * MOBA_BLOCK)
    ind = (np.arange(S)[None, :] // MOBA_BLOCK == np.arange(LANES)[:, None]).astype(np.float32)
    _, cw, page = cache_kt.shape
    nsteps = B * ntile * nb
    pages_per_step, rem = divmod(DB * n_pages, nsteps)
    scan_nch, rem2 = divmod(n_pages, max(pages_per_step, 1))
    assert rem == 0 and rem2 == 0 and pages_per_step % (MOBA_BLOCK // page) == 0, "cache sweep does not tile"
    scan_nblk = n_pages * page // MOBA_BLOCK
    assert scan_nblk <= LANES

    blk = pl.BlockSpec((None, MOBA_BLOCK, LANES), lambda b, t, c, pt: (b, c, t))
    seq = pl.BlockSpec((None, LANES, S), lambda b, t, c, pt: (b, t, 0))
    seq_v = pl.BlockSpec((None, HEADS_PER_TILE * LANES, S), lambda b, t, c, pt: (b, t, 0))
    const = lambda shape: pl.BlockSpec(shape, lambda b, t, c, pt: (0, 0))
    sample_of_step = lambda b, t, c, pt: (((b * ntile + t) * nb + c) // scan_nch, 0, 0)
    return pl.pallas_call(
        functools.partial(_moba_kernel, chunk=chunk, scan_nch=scan_nch, scan_nblk=scan_nblk),
        grid_spec=pltpu.PrefetchScalarGridSpec(
            num_scalar_prefetch=1,
            grid=(B, ntile, nb),
            in_specs=[blk, seq, seq_v, const((LANES, S)),
                      pl.BlockSpec((None, nb, LANES), lambda b, t, c, pt: (b, 0, t)),
                      const(q_t.shape), pl.BlockSpec(memory_space=pl.ANY)],
            out_specs=(blk, pl.BlockSpec((None, N_HEADS, LANES), sample_of_step)),
            scratch_shapes=[pltpu.VMEM((HEADS_PER_TILE, MOBA_BLOCK, S), F32),
                            pltpu.VMEM((2, pages_per_step, cw, page), F32), pltpu.SemaphoreType.DMA((2,)),
                            pltpu.VMEM(q_t.shape, F32), pltpu.VMEM((N_HEADS * SUBLANES, LANES), F32)],
        ),
        out_shape=(jax.ShapeDtypeStruct((B, S, W), BF16), jax.ShapeDtypeStruct((DB, N_HEADS, LANES), jnp.int32)),
        compiler_params=_cparams(("arbitrary", "arbitrary", "arbitrary")),
        name="prompt_moba",
    )(page_table_flat, q_bf, kt_bf, vt_bf, jnp.asarray(ind, BF16), ksum, q_t, cache_kt)


def _route(logits):
    lane = lax.broadcasted_iota(jnp.int32, logits.shape, 1)
    is_grp = (lane >= N_EXPERTS) & (lane < N_EXPERTS + N_EXPERT_GROUPS)
    gl = jnp.where(is_grp, logits, NEG_INF)
    ge = jnp.exp(gl - jnp.max(gl, axis=1, keepdims=True))
    pg = ge / jnp.sum(ge, axis=1, keepdims=True)
    pg_sel = jnp.max(pg, axis=1, keepdims=True)
    g_sel = jnp.min(jnp.where(is_grp & (pg == pg_sel), lane - N_EXPERTS, BIG_IDX), axis=1, keepdims=True)
    in_grp = (lane < N_EXPERTS) & (jnp.right_shift(lane, 2) == g_sel)
    el = jnp.where(in_grp, logits, NEG_INF)
    ee = jnp.exp(el - jnp.max(el, axis=1, keepdims=True))
    pe = jnp.where(in_grp, ee / jnp.sum(ee, axis=1, keepdims=True), -1.0)
    p1 = jnp.max(pe, axis=1, keepdims=True)
    i1 = jnp.min(jnp.where(pe == p1, lane, BIG_IDX), axis=1, keepdims=True)
    pe2 = jnp.where(lane == i1, -1.0, pe)
    p2 = jnp.max(pe2, axis=1, keepdims=True)
    i2 = jnp.min(jnp.where(pe2 == p2, lane, BIG_IDX), axis=1, keepdims=True)
    tot = p1 + p2
    gate = (jnp.where(lane == i1, p1 / tot * pg_sel, 0.0)
            + jnp.where(lane == i2, p2 / tot * pg_sel, 0.0))
    return gate, (g_sel, i1, i2)


PAIRS_PER_GROUP = EXPERTS_PER_GROUP * (EXPERTS_PER_GROUP - 1) // 2
N_CLASSES = N_EXPERT_GROUPS * PAIRS_PER_GROUP
_PAIRS = [(a, b) for a in range(EXPERTS_PER_GROUP) for b in range(a + 1, EXPERTS_PER_GROUP)]
CLASS_EXPERTS = np.array([[g * EXPERTS_PER_GROUP + a, g * EXPERTS_PER_GROUP + b]
                          for g in range(N_EXPERT_GROUPS) for a, b in _PAIRS], np.int32)


def _route_class(g_sel, i1, i2):
    assert EXPERTS_PER_GROUP == 4
    base = g_sel * EXPERTS_PER_GROUP
    a = jnp.minimum(i1, i2) - base
    b = jnp.maximum(i1, i2) - base
    pair = jnp.right_shift(a * (5 - a), 1) + b - 1
    return g_sel * PAIRS_PER_GROUP + pair


def _gate_column(gate, e):
    lane = lax.broadcasted_iota(jnp.int32, gate.shape, 1)
    return jnp.sum(jnp.where(lane == e, gate, 0.0), axis=1, keepdims=True)


def _swiglu(x_bf, wg, wu, wd):
    hg = jnp.dot(x_bf, wg, preferred_element_type=F32)
    hu = jnp.dot(x_bf, wu, preferred_element_type=F32)
    h = hg * (1.0 / (1.0 + jnp.exp(-hg))) * hu
    return jnp.dot(h.astype(BF16), wd, preferred_element_type=F32)


CLASS_LANE = N_EXPERTS + N_EXPERT_GROUPS
RANK_LANE = CLASS_LANE + 1


def _outproj_kernel(x_ref, po_ref, ao_ref, w_ref, g_ref, wr_ref, tri_ref, h_ref, xg_ref, cnt_ref, meta_ref):
    i = pl.program_id(0)
    d = x_ref.shape[1]
    mix = (jnp.dot(po_ref[...], w_ref[0:POOL_WIDTH, :], preferred_element_type=F32)
           + jnp.dot(ao_ref[...], w_ref[POOL_WIDTH:, :], preferred_element_type=F32))
    h = x_ref[...] + mix
    h_ref[...] = h
    xn = _rms(h, g_ref[...])
    gate, picks = _route(jnp.dot(xn.astype(BF16), wr_ref[...], preferred_element_type=F32))
    cls = _route_class(*picks)

    @pl.when(i == 0)
    def _():
        cnt_ref[...] = jnp.zeros_like(cnt_ref)

    lane = lax.broadcasted_iota(jnp.int32, gate.shape, 1)
    onehot = jnp.where(lane == cls, 1.0, 0.0)
    before = jnp.dot(tri_ref[...], onehot.astype(BF16), preferred_element_type=F32) + cnt_ref[0:1, :]
    rank = jnp.sum(onehot * before, axis=1, keepdims=True)
    cnt_ref[...] = cnt_ref[...] + jnp.sum(onehot, axis=0, keepdims=True)
    info = jnp.where(lane == CLASS_LANE, cls.astype(F32), jnp.where(lane == RANK_LANE, rank, gate))
    xg_ref[:, 0:d] = xn
    xg_ref[:, d:] = info
    meta_ref[...] = jnp.transpose(info)[META_ROW0:META_ROW0 + SUBLANES, :]


META_ROW0 = CLASS_LANE // SUBLANES * SUBLANES


def _outproj(x2d, po2d, ao2d, w_out_bf, g_ffn, w_router_bf, tm):
    T, D = x2d.shape
    row = lambda i: (i, 0)
    full = lambda shape: pl.BlockSpec(shape, lambda i: (0, 0))
    tri = np.tril(np.ones((tm, tm), np.float32), -1)
    return pl.pallas_call(
        _outproj_kernel,
        grid=(T // tm,),
        in_specs=[pl.BlockSpec((tm, D), row), pl.BlockSpec((tm, POOL_WIDTH), row),
                  pl.BlockSpec((tm, ATTN_WIDTH), row), full(w_out_bf.shape), full((1, D)),
                  full(w_router_bf.shape), full((tm, tm))],
        out_specs=(pl.BlockSpec((tm, D), row), pl.BlockSpec((tm, D + LANES), row), full((SUBLANES, LANES)),
                   pl.BlockSpec((None, SUBLANES, tm), lambda i: (i, 0, 0))),
        out_shape=(jax.ShapeDtypeStruct((T, D), F32), jax.ShapeDtypeStruct((T, D + LANES), F32),
                   jax.ShapeDtypeStruct((SUBLANES, LANES), F32),
                   jax.ShapeDtypeStruct((T // tm, SUBLANES, tm), F32)),
        compiler_params=_cparams(("arbitrary",)),
        name="prompt_outproj",
    )(x2d, po2d, ao2d, w_out_bf, g_ffn, w_router_bf, jnp.asarray(tri, BF16))


MOE_TILE = 512
ROW_TILE = 1024
DMA_UNROLL = 8


def _dispatch_kernel(dest_ref, xg_ref, zeros_ref, xs_ref, sem_ref):
    del zeros_ref
    n = xg_ref.shape[0]

    def start(g, carry):
        base = pl.multiple_of(g * DMA_UNROLL, DMA_UNROLL)
        rows = xg_ref.at[pl.ds(base, DMA_UNROLL)]
        for k in range(DMA_UNROLL):
            pltpu.make_async_copy(rows.at[pl.ds(k, 1)], xs_ref.at[pl.ds(dest_ref[0, base + k], 1)],
                                  sem_ref.at[0]).start()
        return carry

    lax.fori_loop(0, n // DMA_UNROLL, start, 0)
    pltpu.make_async_copy(xg_ref, xs_ref.at[pl.ds(0, n)], sem_ref.at[0]).wait()


def _dispatch(dest3d, xg, n_rows):
    T, W = xg.shape
    tm = dest3d.shape[2]
    return pl.pallas_call(
        _dispatch_kernel,
        grid=(T // tm,),
        in_specs=[pl.BlockSpec((None, 1, tm), lambda i: (i, 0, 0), memory_space=pltpu.SMEM),
                  pl.BlockSpec((tm, W), lambda i: (i, 0)),
                  pl.BlockSpec(memory_space=pl.ANY)],
        out_specs=pl.BlockSpec(memory_space=pl.ANY),
        out_shape=jax.ShapeDtypeStruct((n_rows, W), F32),
        scratch_shapes=[pltpu.SemaphoreType.DMA((1,))],
        input_output_aliases={2: 0},
        compiler_params=_cparams(("arbitrary",)),
        name="moe_dispatch",
    )(dest3d, xg, jnp.zeros((n_rows, W), F32))


def _moe_kernel(e1_ref, e2_ref, nused_ref, xs_ref, wg1_ref, wu1_ref, wd1_ref, wg2_ref, wu2_ref, wd2_ref, o_ref):
    i = pl.program_id(0)
    d = o_ref.shape[1]

    @pl.when(i < nused_ref[0])
    def _():
        x = xs_ref[:, 0:d].astype(BF16)
        gate = xs_ref[:, d:]
        o_ref[...] = (_gate_column(gate, e1_ref[i]) * _swiglu(x, wg1_ref[...], wu1_ref[...], wd1_ref[...])
                      + _gate_column(gate, e2_ref[i]) * _swiglu(x, wg2_ref[...], wu2_ref[...], wd2_ref[...]))

    @pl.when(i >= nused_ref[0])
    def _():
        o_ref[...] = jnp.zeros_like(o_ref)


def _moe(tile_e1, tile_e2, n_used, xs, wg_bf, wu_bf, wd_bf):
    n_rows, W = xs.shape
    E, D, DE = wg_bf.shape
    tile = lambda i, e1, e2, nu: (jnp.maximum(jnp.minimum(i, nu[0] - 1), 0), 0)
    first = lambda shape: pl.BlockSpec((None,) + shape, lambda i, e1, e2, nu: (e1[i], 0, 0))
    second = lambda shape: pl.BlockSpec((None,) + shape, lambda i, e1, e2, nu: (e2[i], 0, 0))
    return pl.pallas_call(
        _moe_kernel,
        grid_spec=pltpu.PrefetchScalarGridSpec(
            num_scalar_prefetch=3,
            grid=(n_rows // MOE_TILE,),
            in_specs=[pl.BlockSpec((MOE_TILE, W), tile),
                      first((D, DE)), first((D, DE)), first((DE, D)),
                      second((D, DE)), second((D, DE)), second((DE, D))],
            out_specs=pl.BlockSpec((MOE_TILE, D), lambda i, e1, e2, nu: (i, 0)),
        ),
        out_shape=jax.ShapeDtypeStruct((n_rows, D), F32),
        compiler_params=_cparams(("arbitrary",)),
        name="moe_experts",
    )(tile_e1, tile_e2, n_used, xs, wg_bf, wu_bf, wd_bf, wg_bf, wu_bf, wd_bf)


def _combine_copies_start(dest_ref, src_ref, buf_ref, sem_ref, slot):
    def start(g, carry):
        base = pl.multiple_of(g * DMA_UNROLL, DMA_UNROLL)
        rows = buf_ref.at[slot, pl.ds(base, DMA_UNROLL)]
        for k in range(DMA_UNROLL):
            pltpu.make_async_copy(src_ref.at[pl.ds(dest_ref[0, base + k], 1)], rows.at[pl.ds(k, 1)],
                                  sem_ref.at[slot]).start()
        return carry

    lax.fori_loop(0, buf_ref.shape[1] // DMA_UNROLL, start, 0)


def _combine_kernel(dest_ref, dest_next_ref, h_ref, gf_ref, moe_ref, y_ref, buf_ref, sem_ref):
    i = pl.program_id(0)
    n = buf_ref.shape[1]
    slot = i % 2

    @pl.when(i == 0)
    def _():
        _combine_copies_start(dest_ref, moe_ref, buf_ref, sem_ref, slot)

    @pl.when(i + 1 < pl.num_programs(0))
    def _():
        _combine_copies_start(dest_next_ref, moe_ref, buf_ref, sem_ref, 1 - slot)

    pltpu.make_async_copy(moe_ref.at[pl.ds(0, n)], buf_ref.at[slot], sem_ref.at[slot]).wait()
    y_ref[...] = _rms(h_ref[...] + buf_ref[slot], gf_ref[...])


def _combine(dest3d, h2d, g_final, moe_sorted):
    T, D = h2d.shape
    nt, _, tm = dest3d.shape
    return pl.pallas_call(
        _combine_kernel,
        grid=(nt,),
        in_specs=[pl.BlockSpec((None, 1, tm), lambda i: (i, 0, 0), memory_space=pltpu.SMEM),
                  pl.BlockSpec((None, 1, tm), lambda i: (jnp.minimum(i + 1, nt - 1), 0, 0),
                               memory_space=pltpu.SMEM),
                  pl.BlockSpec((tm, D), lambda i: (i, 0)),
                  pl.BlockSpec((1, D), lambda i: (0, 0)),
                  pl.BlockSpec(memory_space=pl.ANY)],
        out_specs=pl.BlockSpec((tm, D), lambda i: (i, 0)),
        out_shape=jax.ShapeDtypeStruct((T, D), F32),
        scratch_shapes=[pltpu.VMEM((2, tm, D), F32), pltpu.SemaphoreType.DMA((2,))],
        compiler_params=_cparams(("arbitrary",)),
        name="moe_combine",
    )(dest3d, dest3d, h2d, g_final, moe_sorted)


def _moe_layout(meta, counts):
    cls = meta[:, CLASS_LANE - META_ROW0, :].reshape(-1).astype(jnp.int32)
    rank = meta[:, RANK_LANE - META_ROW0, :].reshape(-1).astype(jnp.int32)
    cnt = counts[0, :N_CLASSES].astype(jnp.int32)
    tiles = (cnt + MOE_TILE - 1) // MOE_TILE
    tile_end = jnp.cumsum(tiles)
    row_start = (tile_end - tiles) * MOE_TILE
    dest = rank
    for c in range(N_CLASSES):
        dest = dest + jnp.where(cls == c, row_start[c], 0)
    max_tiles = cls.shape[0] // MOE_TILE + N_CLASSES
    tile_ids = jnp.arange(max_tiles, dtype=jnp.int32)
    tile_class = jnp.minimum(jnp.sum((tile_ids[:, None] >= tile_end[None, :]).astype(jnp.int32), axis=1),
                             N_CLASSES - 1)
    experts = jnp.asarray(CLASS_EXPERTS)[tile_class]
    return dest, experts[:, 0], experts[:, 1], tile_end[-1:].astype(jnp.int32), max_tiles * MOE_TILE


def _sample_inproj_kernel(x_ref, g_ref, w_ref, c_ref, slo_ref, shi_ref, sp_ref, wp_ref, ps_ref,
                          u_ref, k_ref, v_ref, po_ref, qt_ref, kt_ref, vt_ref):
    xn = _rms(x_ref[...], g_ref[...])

    def proj(i):
        return jnp.dot(xn, w_ref[:, i * POOL_WIDTH:(i + 1) * POOL_WIDTH],
                       preferred_element_type=F32, precision=HIGHEST)

    u, q, k, v = proj(0), proj(1), proj(2), proj(3)
    c, s_lo, s_hi = c_ref[...], slo_ref[...], shi_ref[...]
    q = _rotate(q, c, s_lo, s_hi)
    k = _rotate(k, c, s_lo, s_hi)
    u_ref[...] = u
    k_ref[...] = k
    v_ref[...] = v
    qt_ref[...] = jnp.transpose(q)
    kt_ref[...] = jnp.transpose(k)
    vt_ref[...] = jnp.transpose(v)
    db = sp_ref.shape[1]
    outs = []
    for g, w in enumerate(POOL_WINDOWS):
        sl = slice(g * POOL_GROUP, (g + 1) * POOL_GROUP)
        ug = u[0:db, sl]
        acc = ug
        for j in range(1, w):
            acc = acc + sp_ref[POOL_HIST - j, :, sl]
        d = acc / float(w) - ug
        outs.append(jnp.dot(d, wp_ref[g], preferred_element_type=F32, precision=HIGHEST))
    po_ref[...] = jnp.zeros_like(po_ref)
    po_ref[0:db, :] = jnp.concatenate(outs, axis=1) * ps_ref[...]


def _sample_inproj(x_pad, g_mix, w_in, pos, sp_t, w_pool, pool_scale):
    R, D = x_pad.shape
    c, s_lo, s_hi = _rope_tables(pos)
    row = jax.ShapeDtypeStruct((R, POOL_WIDTH), F32)
    col = jax.ShapeDtypeStruct((POOL_WIDTH, R), F32)
    return pl.pallas_call(
        _sample_inproj_kernel,
        out_shape=(row, row, row, row, col, col, col),
        compiler_params=pltpu.CompilerParams(vmem_limit_bytes=VMEM_LIMIT),
        name="sample_inproj",
    )(x_pad, g_mix, w_in, c, s_lo, s_hi, sp_t, w_pool, pool_scale)


def _column(mat, b):
    lane = lax.broadcasted_iota(jnp.int32, mat.shape, 1)
    col = jnp.sum(jnp.where(lane == b, mat, 0.0), axis=1, keepdims=True)
    return jnp.broadcast_to(col, mat.shape)


def _scan_copies(pt_ref, cache_ref, buf_ref, sem_ref, step, slot):
    n = buf_ref.shape[1]
    return [pltpu.make_async_copy(cache_ref.at[pt_ref[step * n + i]], buf_ref.at[slot, i], sem_ref.at[slot])
            for i in range(n)]


def _scan_step(step, nsteps, nch, pt_ref, qt_ref, cache_ref, top_ref, buf_ref, sem_ref, qb_sc, g_sc, nblk):
    b = step // nch
    ch = step - b * nch
    slot = step % 2

    @pl.when(step == 0)
    def _():
        for cp in _scan_copies(pt_ref, cache_ref, buf_ref, sem_ref, step, slot):
            cp.start()

    @pl.when(step + 1 < nsteps)
    def _():
        for cp in _scan_copies(pt_ref, cache_ref, buf_ref, sem_ref, step + 1, 1 - slot):
            cp.start()

    @pl.when(ch == 0)
    def _():
        qb_sc[...] = _column(qt_ref[...], b)
        g_sc[...] = jnp.zeros_like(g_sc)

    for cp in _scan_copies(pt_ref, cache_ref, buf_ref, sem_ref, step, slot):
        cp.wait()
    qb = qb_sc[...]
    lane = lax.broadcasted_iota(jnp.int32, (N_HEADS, LANES), 1)
    pages_per_blk = MOBA_BLOCK // buf_ref.shape[3]
    blks_per_step = buf_ref.shape[1] // pages_per_blk
    lane8 = lax.broadcasted_iota(jnp.int32, (SUBLANES, LANES), 1)
    g = [g_sc[h * SUBLANES:(h + 1) * SUBLANES, :] for h in range(N_HEADS)]
    for j in range(blks_per_step):
        tot = buf_ref[slot, j * pages_per_blk]
        for i in range(1, pages_per_blk):
            tot = tot + buf_ref[slot, j * pages_per_blk + i]
        pr = tot * qb
        for h in range(N_HEADS):
            a = pr[h * HEAD_DIM:h * HEAD_DIM + SUBLANES]
            for r in range(h * HEAD_DIM + SUBLANES, (h + 1) * HEAD_DIM, SUBLANES):
                a = a + pr[r:r + SUBLANES]
            g[h] = jnp.where(lane8 == ch * blks_per_step + j, jnp.sum(a, axis=1, keepdims=True), g[h])
    for h in range(N_HEADS):
        g_sc[h * SUBLANES:(h + 1) * SUBLANES, :] = g[h]

    @pl.when(ch == nch - 1)
    def _():
        full = jnp.concatenate([jnp.sum(gh, axis=0, keepdims=True) for gh in g], axis=0)
        gg = jnp.where(lane < nblk, full, NEG_INF)
        top = jnp.zeros((N_HEADS, LANES), jnp.int32)
        for t in range(MOBA_TOPK):
            m = jnp.max(gg, axis=1, keepdims=True)
            idx = jnp.min(jnp.where(gg == m, lane, BIG_IDX), axis=1, keepdims=True)
            top = jnp.where(lane == t, idx, top)
            gg = jnp.where(lane == idx, NEG_INF, gg)
        top_ref[...] = top


def _attn_copies(pt_ref, top_ref, ck_ref, cv_ref, kbuf, vbuf, sem_ref, b, slot, n_pages, pages_per_blk):
    cps = []
    for h in range(N_HEADS):
        rows = pl.ds(h * HEAD_DIM, HEAD_DIM)
        for t in range(MOBA_TOPK):
            blk = top_ref[(b * N_HEADS + h) * MOBA_TOPK + t]
            for i in range(pages_per_blk):
                phys = pt_ref[b * n_pages + blk * pages_per_blk + i]
                j = t * pages_per_blk + i
                cps.append(pltpu.make_async_copy(ck_ref.at[phys, rows, :], kbuf.at[slot, h, j], sem_ref.at[slot]))
                cps.append(pltpu.make_async_copy(cv_ref.at[phys, rows, :], vbuf.at[slot, h, j], sem_ref.at[slot]))
    return cps


def _sample_attn_kernel(pt_ref, top_ref, qt_ref, kt_ref, vt_ref, ck_ref, cv_ref, o_ref, kbuf, vbuf, sem_ref,
                        *, n_pages):
    b = pl.program_id(0)
    nb = pl.num_programs(0)
    slot = b % 2
    n_sel = kbuf.shape[2]
    copies = functools.partial(_attn_copies, pt_ref, top_ref, ck_ref, cv_ref, kbuf, vbuf, sem_ref,
                               n_pages=n_pages, pages_per_blk=n_sel // MOBA_TOPK)

    @pl.when(b == 0)
    def _():
        o_ref[...] = jnp.zeros_like(o_ref)
        for cp in copies(b=b, slot=slot):
            cp.start()

    @pl.when(b + 1 < nb)
    def _():
        for cp in copies(b=b + 1, slot=1 - slot):
            cp.start()

    for cp in copies(b=b, slot=slot):
        cp.wait()

    scale = HEAD_DIM ** -0.5
    lane = lax.broadcasted_iota(jnp.int32, (HEAD_DIM, LANES), 1)
    for h in range(N_HEADS):
        rows = slice(h * HEAD_DIM, (h + 1) * HEAD_DIM)
        qb = _column(qt_ref[rows, :], b)
        kb = _column(kt_ref[rows, :], b)
        vb = _column(vt_ref[rows, :], b)
        s = [jnp.sum(kbuf[slot, h, j] * qb, axis=0, keepdims=True) * scale for j in range(n_sel)]
        s_own = jnp.sum(kb * qb, axis=0, keepdims=True) * scale
        m = s_own
        for sj in s:
            m = jnp.maximum(m, jnp.max(sj, axis=1, keepdims=True))
        p = [jnp.exp(sj - m) for sj in s]
        p_own = jnp.exp(s_own - m)
        l = p_own
        acc = jnp.zeros((HEAD_DIM, LANES), F32)
        for j, pj in enumerate(p):
            l = l + jnp.sum(pj, axis=1, keepdims=True)
            acc = acc + vbuf[slot, h, j] * pj
        col = (jnp.sum(acc, axis=1, keepdims=True) + p_own * vb) / l
        o_ref[rows, :] = jnp.where(lane == b, col, o_ref[rows, :])


def _sample_attn(page_table_flat, top_flat, q_t, k_t, v_t, cache_kt, cache_vt, DB, n_pages):
    _, W, page = cache_kt.shape
    n_sel = MOBA_TOPK * MOBA_BLOCK // page
    vm = lambda: pl.BlockSpec(q_t.shape, lambda b, pt, tp: (0, 0))
    hbm = lambda: pl.BlockSpec(memory_space=pl.ANY)
    return pl.pallas_call(
        functools.partial(_sample_attn_kernel, n_pages=n_pages),
        grid_spec=pltpu.PrefetchScalarGridSpec(
            num_scalar_prefetch=2,
            grid=(DB,),
            in_specs=[vm(), vm(), vm(), hbm(), hbm()],
            out_specs=vm(),
            scratch_shapes=[pltpu.VMEM((2, N_HEADS, n_sel, HEAD_DIM, page), F32),
                            pltpu.VMEM((2, N_HEADS, n_sel, HEAD_DIM, page), F32),
                            pltpu.SemaphoreType.DMA((2,))],
        ),
        out_shape=jax.ShapeDtypeStruct(q_t.shape, F32),
        compiler_params=_cparams(("arbitrary",)),
        name="sample_attn",
    )(page_table_flat, top_flat, q_t, k_t, v_t, cache_kt, cache_vt)


def _sample_tail_kernel(x_ref, po_ref, aot_ref, w_ref, g_ref, wr_ref, wg_ref, wu_ref, wd_ref, gf_ref,
                        y_ref, h_sc, xn_sc, gate_sc, acc_sc):
    e = pl.program_id(0)

    @pl.when(e == 0)
    def _():
        ao = jnp.transpose(aot_ref[...])
        mix = (jnp.dot(po_ref[...], w_ref[0:POOL_WIDTH, :], preferred_element_type=F32, precision=HIGHEST)
               + jnp.dot(ao, w_ref[POOL_WIDTH:, :], preferred_element_type=F32, precision=HIGHEST))
        h = x_ref[...] + mix
        h_sc[...] = h
        xn = _rms(h, g_ref[...])
        xn_sc[...] = xn
        gate_sc[...] = _route(jnp.dot(xn, wr_ref[...], preferred_element_type=F32, precision=HIGHEST))[0]
        acc_sc[...] = jnp.zeros_like(acc_sc)

    o = _swiglu(xn_sc[...].astype(BF16), wg_ref[...], wu_ref[...], wd_ref[...])
    acc_sc[...] += _gate_column(gate_sc[...], e) * o

    @pl.when(e == pl.num_programs(0) - 1)
    def _():
        y_ref[...] = _rms(h_sc[...] + acc_sc[...], gf_ref[...])


def _sample_tail(x_pad, po, ao_t, w_out, g_ffn, w_router, wg_bf, wu_bf, wd_bf, g_final):
    R, D = x_pad.shape
    E, _, DE = wg_bf.shape
    full = lambda shape: pl.BlockSpec(shape, lambda e: (0,) * len(shape))
    return pl.pallas_call(
        _sample_tail_kernel,
        grid=(E,),
        in_specs=[full((R, D)), full((R, POOL_WIDTH)), full((ATTN_WIDTH, R)), full(w_out.shape),
                  full((1, D)), full(w_router.shape),
                  pl.BlockSpec((None, D, DE), lambda e: (e, 0, 0)),
                  pl.BlockSpec((None, D, DE), lambda e: (e, 0, 0)),
                  pl.BlockSpec((None, DE, D), lambda e: (e, 0, 0)),
                  full((1, D))],
        out_specs=full((R, D)),
        out_shape=jax.ShapeDtypeStruct((R, D), F32),
        scratch_shapes=[pltpu.VMEM((R, D), F32), pltpu.VMEM((R, D), F32), pltpu.VMEM((R, LANES), F32),
                        pltpu.VMEM((R, D), F32)],
        compiler_params=_cparams(("arbitrary",)),
        name="sample_tail",
    )(x_pad, po, ao_t, w_out, g_ffn, w_router, wg_bf, wu_bf, wd_bf, g_final)


def _tile(n, pref):
    while n % pref:
        pref //= 2
    return pref


def kernel(x_prompt, x_sample, cache_k, cache_v, state_pool, page_table, norm_mix, w_in, w_pool, pool_scale,
           w_out, norm_ffn, w_router_group, w_router_expert, w_gate, w_up, w_down, norm_final):
    B, S, D = x_prompt.shape
    DB, DS, _ = x_sample.shape
    depth, n_phys, page = cache_k.shape[:3]
    assert depth == 1 and DS == 1, "single layer, one new token per sequence"
    n_pages = page_table.shape[1]
    past_len = n_pages * page
    assert S % MOBA_BLOCK == 0 and past_len % MOBA_BLOCK == 0 and MOBA_BLOCK % page == 0
    assert past_len // MOBA_BLOCK >= MOBA_TOPK and DB <= LANES
    T = B * S

    g_mix = norm_mix[0][None, :]
    g_ffn = norm_ffn[0][None, :]
    g_final = norm_final[None, :]
    ps = pool_scale[0][None, :]
    w_router = jnp.concatenate(
        [w_router_expert[0], w_router_group[0],
         jnp.zeros((D, LANES - N_EXPERTS - N_EXPERT_GROUPS), F32)], axis=1)
    wg_bf, wu_bf, wd_bf = w_gate[0].astype(BF16), w_up[0].astype(BF16), w_down[0].astype(BF16)

    xs = x_sample.reshape(DB, D)
    x_pad = jnp.pad(xs, ((0, LANES - DB), (0, 0)))
    pos_s = jnp.full((1,), past_len, jnp.int32)
    sp_t = jnp.swapaxes(state_pool[0], 0, 1)
    u_s, k_s, v_s, po_s, q_t, k_t, v_t = _sample_inproj(x_pad, g_mix, w_in[0], pos_s, sp_t, w_pool[0], ps)
    cache_kt = jnp.transpose(cache_k[0], (0, 2, 3, 1)).reshape(n_phys, ATTN_WIDTH, page)
    cache_vt = jnp.transpose(cache_v[0], (0, 2, 3, 1)).reshape(n_phys, ATTN_WIDTH, page)
    pt_flat = page_table.reshape(-1).astype(jnp.int32)

    ts = _tile(S, 512)
    kt_f, vt_f, q_bf, kt_bf, vt_bf, po, ksum, u_last = _inproj(
        x_prompt, g_mix, w_in[0], w_pool[0].astype(BF16), ps, ts)
    ksum = ksum[:, :, :ts // MOBA_BLOCK].reshape(B, S // MOBA_BLOCK, ATTN_WIDTH)
    ao, top = _moba_prompt(q_bf, kt_bf, vt_bf, ksum, pt_flat, q_t, cache_kt, DB, n_pages)
    h2, xg, counts, meta = _outproj(x_prompt.reshape(T, D), po.reshape(T, POOL_WIDTH), ao.reshape(T, ATTN_WIDTH),
                                    w_out[0].astype(BF16), g_ffn, w_router.astype(BF16), _tile(T, 512))
    dest, tile_e1, tile_e2, n_used, n_rows = _moe_layout(meta, counts)
    dest3d = dest.reshape(T // ROW_TILE, 1, ROW_TILE)
    x_grouped = _dispatch(dest3d, xg, n_rows)
    moe_sorted = _moe(tile_e1, tile_e2, n_used, x_grouped, wg_bf, wu_bf, wd_bf)
    y_prompt = _combine(dest3d, h2, g_final, moe_sorted).reshape(B, S, D)
    k_prompt = jnp.transpose(kt_f.reshape(B, N_HEADS, HEAD_DIM, S), (0, 3, 1, 2))[None]
    v_prompt = jnp.transpose(vt_f.reshape(B, N_HEADS, HEAD_DIM, S), (0, 3, 1, 2))[None]
    pool_prompt = u_last[None, :, HALO - POOL_HIST:, :]

    top_flat = top[:, :, :MOBA_TOPK].reshape(-1)
    ao_t = _sample_attn(pt_flat, top_flat, q_t, k_t, v_t, cache_kt, cache_vt, DB, n_pages)
    y_pad = _sample_tail(x_pad, po_s, ao_t, w_out[0], g_ffn, w_router, wg_bf, wu_bf, wd_bf, g_final)
    y_sample = y_pad[:DB].reshape(DB, 1, D)
    k_sample = k_s[:DB].reshape(1, DB, 1, N_HEADS, HEAD_DIM)
    v_sample = v_s[:DB].reshape(1, DB, 1, N_HEADS, HEAD_DIM)
    pool_sample = jnp.concatenate([state_pool[0][:, 1:], u_s[:DB, None, :]], axis=1)[None]
    return (y_prompt, y_sample, k_prompt, v_prompt, pool_prompt, k_sample, v_sample, pool_sample)
```

```python
import functools

import jax
import jax.numpy as jnp
import numpy as np
from jax import lax
from jax.experimental import pallas as pl
from jax.experimental.pallas import tpu as pltpu

F32 = jnp.float32
BF16 = jnp.bfloat16
HIGHEST = lax.Precision.HIGHEST

POOL_WINDOWS = (2, 4, 8, 16)
POOL_GROUP = 128
POOL_WIDTH = POOL_GROUP * len(POOL_WINDOWS)
POOL_HIST = max(POOL_WINDOWS) - 1
HALO = POOL_HIST + 1
HEAD_DIM = 64
N_HEADS = 8
ATTN_WIDTH = N_HEADS * HEAD_DIM
ROT_DIM = HEAD_DIM // 4
ROT_HALF = ROT_DIM // 2
ROPE_THETA = 500000.0
MOBA_BLOCK = 256
MOBA_TOPK = 3
N_EXPERT_GROUPS = 4
EXPERTS_PER_GROUP = 4
N_EXPERTS = N_EXPERT_GROUPS * EXPERTS_PER_GROUP
RMS_EPS = 1e-6

LANES = 128
SUBLANES = 8
HEADS_PER_TILE = LANES // HEAD_DIM
VMEM_LIMIT = 52 * 1024 * 1024

NEG_INF = float("-inf")
NEG_BIG = -1e30
BIG_IDX = 1 << 20
NT = (((1,), (1,)), ((), ()))
Q_SCALE = HEAD_DIM ** -0.5 * float(np.log2(np.e))


def _cparams(sem):
    return pltpu.CompilerParams(dimension_semantics=sem, vmem_limit_bytes=VMEM_LIMIT)


def _rms(x, g):
    r = lax.rsqrt(jnp.mean(x * x, axis=-1, keepdims=True) + RMS_EPS)
    return x * r * g


def _rope_angles(pos):
    inv = ROPE_THETA ** (-(jnp.arange(0, ROT_DIM, 2, dtype=F32) / ROT_DIM))
    ang = pos.astype(F32)[:, None] * inv[None, :]
    return jnp.cos(ang), jnp.sin(ang)


def _rope_tables(pos):
    cos, sin = _rope_angles(pos)
    d = np.arange(LANES) % HEAD_DIM
    fi = d % ROT_HALF
    cos_l, sin_l = cos[:, fi], sin[:, fi]
    c = jnp.where(d < ROT_DIM, cos_l, 1.0)
    s_lo = jnp.where(d < ROT_HALF, -sin_l, 0.0)
    s_hi = jnp.where((d >= ROT_HALF) & (d < ROT_DIM), sin_l, 0.0)
    return c.astype(F32), s_lo.astype(F32), s_hi.astype(F32)


def _rotate(x, c, s_lo, s_hi):
    outs = []
    for j in range(x.shape[1] // LANES):
        t = x[:, j * LANES:(j + 1) * LANES]
        up = pltpu.roll(t, LANES - ROT_HALF, axis=1)
        dn = pltpu.roll(t, ROT_HALF, axis=1)
        outs.append(t * c + up * s_lo + dn * s_hi)
    return jnp.concatenate(outs, axis=1)


def _rotate_t(xt, cos_t, sin_t):
    pieces = []
    for h in range(N_HEADS):
        r0 = h * HEAD_DIM
        x1 = xt[r0:r0 + ROT_HALF]
        x2 = xt[r0 + ROT_HALF:r0 + ROT_DIM]
        pieces += [x1 * cos_t - x2 * sin_t, x2 * cos_t + x1 * sin_t, xt[r0 + ROT_DIM:r0 + HEAD_DIM]]
    return jnp.concatenate(pieces, axis=0)


def _inproj_kernel(x_ref, g_ref, wuq_ref, wkv_ref, c_ref, slo_ref, shi_ref, ct_ref, st_ref, ind_ref,
                   wp_ref, ps_ref,
                   kf_ref, vf_ref, qb_ref, kb_ref, vb_ref, po_ref, ksum_ref, ulast_ref, ext_ref):
    s = pl.program_id(1)
    ts = x_ref.shape[0]
    xn = _rms(x_ref[...], g_ref[...]).astype(BF16)
    u = jnp.dot(xn, wuq_ref[:, 0:POOL_WIDTH], preferred_element_type=F32)
    q = jnp.dot(xn, wuq_ref[:, POOL_WIDTH:], preferred_element_type=F32)
    kt = lax.dot_general(wkv_ref[0:ATTN_WIDTH, :], xn, NT, preferred_element_type=F32)
    vt = lax.dot_general(wkv_ref[ATTN_WIDTH:, :], xn, NT, preferred_element_type=F32)
    q = _rotate(q, c_ref[...], slo_ref[...], shi_ref[...])
    kt = _rotate_t(kt, ct_ref[...], st_ref[...])
    kf_ref[...] = kt
    vf_ref[...] = vt
    qb_ref[...] = (q * Q_SCALE).astype(BF16)
    kb_ref[...] = kt.astype(BF16)
    ones = jnp.ones((HEAD_DIM, ts), BF16)
    vtb = vt.astype(BF16)
    vb_ref[...] = jnp.concatenate(
        [piece for h in range(N_HEADS) for piece in (vtb[h * HEAD_DIM:(h + 1) * HEAD_DIM], ones)], axis=0)
    ksum_ref[...] = lax.dot_general(ind_ref[...], kt, NT, preferred_element_type=F32, precision=HIGHEST)

    @pl.when(s == 0)
    def _():
        ext_ref[0:HALO, :] = jnp.zeros((HALO, POOL_WIDTH), F32)

    ext_ref[HALO:HALO + ts, :] = u
    pos1 = s * ts + lax.broadcasted_iota(jnp.int32, (ts, POOL_GROUP), 0) + 1
    outs = []
    for g, w in enumerate(POOL_WINDOWS):
        sl = slice(g * POOL_GROUP, (g + 1) * POOL_GROUP)
        ug = u[:, sl]
        acc = ug
        for j in range(1, w):
            acc = acc + ext_ref[HALO - j:HALO - j + ts, sl]
        cnt = jnp.minimum(pos1, w).astype(F32)
        d = (acc / cnt - ug).astype(BF16)
        outs.append(jnp.dot(d, wp_ref[g], preferred_element_type=F32))
    y = jnp.concatenate(outs, axis=1) * ps_ref[...]
    po_ref[...] = y.astype(BF16)
    tail = u[ts - HALO:ts, :]
    ext_ref[0:HALO, :] = tail

    @pl.when(s == pl.num_programs(1) - 1)
    def _():
        ulast_ref[...] = tail


def _inproj(x, g_mix, w_in, w_pool_bf, pool_scale, ts):
    B, S, D = x.shape
    ns = S // ts
    pos = jnp.arange(S, dtype=jnp.int32)
    c, s_lo, s_hi = _rope_tables(pos)
    cos, sin = _rope_angles(pos)
    cos_t, sin_t = cos.T, sin.T
    w_uq = w_in[:, :POOL_WIDTH + ATTN_WIDTH].astype(BF16)
    w_kv_t = w_in[:, POOL_WIDTH + ATTN_WIDTH:].T.astype(BF16)
    ind = (np.arange(ts)[None, :] // MOBA_BLOCK == np.arange(SUBLANES)[:, None]).astype(np.float32)
    row = lambda b, s: (b, s, 0)
    col = lambda b, s: (b, 0, s)
    tab = pl.BlockSpec((ts, LANES), lambda b, s: (s, 0))
    tab_t = pl.BlockSpec((ROT_HALF, ts), lambda b, s: (0, s))
    full2 = lambda shape: pl.BlockSpec(shape, lambda b, s: (0, 0))
    act = lambda: pl.BlockSpec((None, ts, ATTN_WIDTH), row)
    act_t = lambda: pl.BlockSpec((None, ATTN_WIDTH, ts), col)
    out_shapes = (
        jax.ShapeDtypeStruct((B, ATTN_WIDTH, S), F32),
        jax.ShapeDtypeStruct((B, ATTN_WIDTH, S), F32),
        jax.ShapeDtypeStruct((B, S, ATTN_WIDTH), BF16),
        jax.ShapeDtypeStruct((B, ATTN_WIDTH, S), BF16),
        jax.ShapeDtypeStruct((B, 2 * ATTN_WIDTH, S), BF16),
        jax.ShapeDtypeStruct((B, S, POOL_WIDTH), BF16),
        jax.ShapeDtypeStruct((B, ns, SUBLANES, ATTN_WIDTH), F32),
        jax.ShapeDtypeStruct((B, HALO, POOL_WIDTH), F32),
    )
    return pl.pallas_call(
        _inproj_kernel,
        grid=(B, ns),
        in_specs=[
            pl.BlockSpec((None, ts, D), row),
            full2((1, D)),
            full2(w_uq.shape),
            full2(w_kv_t.shape),
            tab, tab, tab, tab_t, tab_t,
            full2(ind.shape),
            pl.BlockSpec(w_pool_bf.shape, lambda b, s: (0, 0, 0)),
            full2((1, POOL_WIDTH)),
        ],
        out_specs=(act_t(), act_t(), act(), act_t(), pl.BlockSpec((None, 2 * ATTN_WIDTH, ts), col), act(),
                   pl.BlockSpec((None, None, SUBLANES, ATTN_WIDTH), lambda b, s: (b, s, 0, 0)),
                   pl.BlockSpec((None, HALO, POOL_WIDTH), lambda b, s: (b, 0, 0))),
        out_shape=out_shapes,
        scratch_shapes=[pltpu.VMEM((HALO + ts, POOL_WIDTH), F32)],
        compiler_params=_cparams(("arbitrary", "arbitrary")),
        name="prompt_inproj",
    )(x, g_mix, w_uq, w_kv_t, c, s_lo, s_hi, cos_t, sin_t, jnp.asarray(ind), w_pool_bf, pool_scale)


def _moba_kernel(pt_ref, q_ref, k_ref, v_ref, e_ref, ksum_ref, qt_ref, cache_ref, o_ref, top_ref,
                 s_sc, buf_ref, sem_ref, qb_sc, g_sc, *, chunk, scan_nch, scan_nblk):
    n1, n2 = pl.num_programs(1), pl.num_programs(2)
    step = (pl.program_id(0) * n1 + pl.program_id(1)) * n2 + pl.program_id(2)
    _scan_step(step, pl.num_programs(0) * n1 * n2, scan_nch, pt_ref, qt_ref, cache_ref, top_ref,
               buf_ref, sem_ref, qb_sc, g_sc, scan_nblk)

    c = pl.program_id(2)
    tq = q_ref.shape[0]
    nb = ksum_ref.shape[0]
    blocks_per_chunk = chunk // MOBA_BLOCK
    n_tiles = chunk // LANES
    q = q_ref[...]
    kmean = (ksum_ref[...] * (1.0 / MOBA_BLOCK)).astype(BF16)
    lane = lax.broadcasted_iota(jnp.int32, (tq, LANES), 1)
    blk_iota = lax.broadcasted_iota(jnp.int32, (nb, tq), 0)

    qp = []
    for h in range(HEADS_PER_TILE):
        in_head = (lane >= h * HEAD_DIM) & (lane < (h + 1) * HEAD_DIM)
        qm = jnp.where(in_head, q, jnp.zeros_like(q))
        g = lax.dot_general(kmean, qm, NT, preferred_element_type=F32)
        g = jnp.where(blk_iota < c, g, NEG_INF)
        sel = blk_iota == c
        for t in range(MOBA_TOPK):
            m = jnp.max(g, axis=0, keepdims=True)
            idx = jnp.min(jnp.where(g == m, blk_iota, BIG_IDX), axis=0, keepdims=True)
            hit = blk_iota == idx
            sel = sel | (hit & (t < c))
            g = jnp.where(hit, NEG_INF, g)
        bias_t = jnp.where(sel, 0.0, NEG_BIG)
        bias_t = jnp.concatenate([bias_t, jnp.zeros((LANES - nb, tq), F32)], axis=0)
        bias = jnp.transpose(bias_t).astype(BF16)
        qp.append(jnp.concatenate([qm, bias], axis=1))

    def keys(off):
        return jnp.concatenate([k_ref[:, pl.ds(off, chunk)], e_ref[:, pl.ds(off, chunk)]], axis=0)

    def fold(fn, acc, x):
        for j in range(n_tiles):
            acc = fn(acc, x[:, j * LANES:(j + 1) * LANES])
        return acc

    n_past = lax.shift_right_logical(c, int(np.log2(blocks_per_chunk)))

    def pass1(i, mrun):
        off = pl.multiple_of(i * chunk, chunk)
        kk = keys(off)
        new = []
        for h in range(HEADS_PER_TILE):
            s = jnp.dot(qp[h], kk, preferred_element_type=F32)
            s_sc[h, :, pl.ds(off, chunk)] = s
            new.append(fold(jnp.maximum, mrun[h], s))
        return tuple(new)

    def pairwise(body, n, init):
        def run(base, count, c_):
            for k in range(count):
                c_ = body(base + k, c_)
            return c_

        carry = lax.fori_loop(0, lax.shift_right_logical(n, 2), lambda i, c_: run(4 * i, 4, c_), init)
        base = n & ~3
        carry = lax.cond((n & 2) != 0, lambda c_: run(base, 2, c_), lambda c_: c_, carry)
        return lax.cond((n & 1) != 0, lambda c_: run(n - 1, 1, c_), lambda c_: c_, carry)

    off = pl.multiple_of(n_past * chunk, chunk)
    kk = keys(off)
    kpos = off + lax.broadcasted_iota(jnp.int32, (tq, chunk), 1)
    qpos = c * MOBA_BLOCK + lax.broadcasted_iota(jnp.int32, (tq, chunk), 0)
    mrun = []
    for h in range(HEADS_PER_TILE):
        s = jnp.dot(qp[h], kk, preferred_element_type=F32)
        s = jnp.where(kpos <= qpos, s, NEG_BIG)
        s_sc[h, :, pl.ds(off, chunk)] = s
        mrun.append(fold(jnp.maximum, jnp.full((tq, LANES), NEG_BIG, F32), s))
    mrun = pairwise(pass1, n_past, tuple(mrun))
    m_b = [jnp.broadcast_to(jnp.max(mrun[h], axis=1, keepdims=True), (tq, LANES)) for h in range(HEADS_PER_TILE)]

    def pass2(i, accs):
        off = pl.multiple_of(i * chunk, chunk)
        new = []
        for h in range(HEADS_PER_TILE):
            s = s_sc[h, :, pl.ds(off, chunk)]
            pb = jnp.concatenate([jnp.exp2(s[:, j * LANES:(j + 1) * LANES] - m_b[h]).astype(BF16)
                                  for j in range(n_tiles)], axis=1)
            vv = v_ref[h * LANES:(h + 1) * LANES, pl.ds(off, chunk)]
            new.append(accs[h] + lax.dot_general(pb, vv, NT, preferred_element_type=F32))
        return tuple(new)

    acc0, acc1 = pairwise(pass2, n_past + 1, (jnp.zeros((tq, LANES), F32),) * HEADS_PER_TILE)
    out = jnp.where(lane < HEAD_DIM, acc0 / pltpu.roll(acc0, HEAD_DIM, axis=1),
                    pltpu.roll(acc1, HEAD_DIM, axis=1) / acc1)
    o_ref[...] = out.astype(o_ref.dtype)


def _moba_prompt(q_bf, kt_bf, vt_bf, ksum, page_table_flat, q_t, cache_kt, DB, n_pages):
    B, S, W = q_bf.shape
    nb = S // MOBA_BLOCK
    assert nb <= LANES and HEADS_PER_TILE == 2
    ntile = W // LANES
    chunk = min(S, 4 * MOBA_BLOCK)
    ind = (np.arange(S)[None, :] // MOBA_BLOCK == np.arange(LANES)[:, None]).astype(np.float32)
    _, cw, page = cache_kt.shape
    nsteps = B * ntile * nb
    pages_per_step, rem = divmod(DB * n_pages, nsteps)
    scan_nch, rem2 = divmod(n_pages, max(pages_per_step, 1))
    assert rem == 0 and rem2 == 0 and pages_per_step % (MOBA_BLOCK // page) == 0, "cache sweep does not tile"
    scan_nblk = n_pages * page // MOBA_BLOCK
    assert scan_nblk <= LANES

    blk = pl.BlockSpec((None, MOBA_BLOCK, LANES), lambda b, t, c, pt: (b, c, t))
    seq = pl.BlockSpec((None, LANES, S), lambda b, t, c, pt: (b, t, 0))
    seq_v = pl.BlockSpec((None, HEADS_PER_TILE * LANES, S), lambda b, t, c, pt: (b, t, 0))
    const = lambda shape: pl.BlockSpec(shape, lambda b, t, c, pt: (0, 0))
    sample_of_step = lambda b, t, c, pt: (((b * ntile + t) * nb + c) // scan_nch, 0, 0)
    return pl.pallas_call(
        functools.partial(_moba_kernel, chunk=chunk, scan_nch=scan_nch, scan_nblk=scan_nblk),
        grid_spec=pltpu.PrefetchScalarGridSpec(
            num_scalar_prefetch=1,
            grid=(B, ntile, nb),
            in_specs=[blk, seq, seq_v, const((LANES, S)),
                      pl.BlockSpec((None, nb, LANES), lambda b, t, c, pt: (b, 0, t)),
                      const(q_t.shape), pl.BlockSpec(memory_space=pl.ANY)],
            out_specs=(blk, pl.BlockSpec((None, N_HEADS, LANES), sample_of_step)),
            scratch_shapes=[pltpu.VMEM((HEADS_PER_TILE, MOBA_BLOCK, S), F32),
                            pltpu.VMEM((2, pages_per_step, cw, page), F32), pltpu.SemaphoreType.DMA((2,)),
                            pltpu.VMEM(q_t.shape, F32), pltpu.VMEM((N_HEADS * SUBLANES, LANES), F32)],
        ),
        out_shape=(jax.ShapeDtypeStruct((B, S, W), BF16), jax.ShapeDtypeStruct((DB, N_HEADS, LANES), jnp.int32)),
        compiler_params=_cparams(("arbitrary", "arbitrary", "arbitrary")),
        name="prompt_moba",
    )(page_table_flat, q_bf, kt_bf, vt_bf, jnp.asarray(ind, BF16), ksum, q_t, cache_kt)


def _route(logits):
    lane = lax.broadcasted_iota(jnp.int32, logits.shape, 1)
    is_grp = (lane >= N_EXPERTS) & (lane < N_EXPERTS + N_EXPERT_GROUPS)
    gl = jnp.where(is_grp, logits, NEG_INF)
    ge = jnp.exp(gl - jnp.max(gl, axis=1, keepdims=True))
    pg = ge / jnp.sum(ge, axis=1, keepdims=True)
    pg_sel = jnp.max(pg, axis=1, keepdims=True)
    g_sel = jnp.min(jnp.where(is_grp & (pg == pg_sel), lane - N_EXPERTS, BIG_IDX), axis=1, keepdims=True)
    in_grp = (lane < N_EXPERTS) & (jnp.right_shift(lane, 2) == g_sel)
    el = jnp.where(in_grp, logits, NEG_INF)
    ee = jnp.exp(el - jnp.max(el, axis=1, keepdims=True))
    pe = jnp.where(in_grp, ee / jnp.sum(ee, axis=1, keepdims=True), -1.0)
    p1 = jnp.max(pe, axis=1, keepdims=True)
    i1 = jnp.min(jnp.where(pe == p1, lane, BIG_IDX), axis=1, keepdims=True)
    pe2 = jnp.where(lane == i1, -1.0, pe)
    p2 = jnp.max(pe2, axis=1, keepdims=True)
    i2 = jnp.min(jnp.where(pe2 == p2, lane, BIG_IDX), axis=1, keepdims=True)
    tot = p1 + p2
    gate = (jnp.where(lane == i1, p1 / tot * pg_sel, 0.0)
            + jnp.where(lane == i2, p2 / tot * pg_sel, 0.0))
    return gate, (g_sel, i1, i2)


PAIRS_PER_GROUP = EXPERTS_PER_GROUP * (EXPERTS_PER_GROUP - 1) // 2
N_CLASSES = N_EXPERT_GROUPS * PAIRS_PER_GROUP
_PAIRS = [(a, b) for a in range(EXPERTS_PER_GROUP) for b in range(a + 1, EXPERTS_PER_GROUP)]
CLASS_EXPERTS = np.array([[g * EXPERTS_PER_GROUP + a, g * EXPERTS_PER_GROUP + b]
                          for g in range(N_EXPERT_GROUPS) for a, b in _PAIRS], np.int32)


def _route_class(g_sel, i1, i2):
    assert EXPERTS_PER_GROUP == 4
    base = g_sel * EXPERTS_PER_GROUP
    a = jnp.minimum(i1, i2) - base
    b = jnp.maximum(i1, i2) - base
    pair = jnp.right_shift(a * (5 - a), 1) + b - 1
    return g_sel * PAIRS_PER_GROUP + pair


def _gate_column(gate, e):
    lane = lax.broadcasted_iota(jnp.int32, gate.shape, 1)
    return jnp.sum(jnp.where(lane == e, gate, 0.0), axis=1, keepdims=True)


def _swiglu(x_bf, wg, wu, wd):
    hg = jnp.dot(x_bf, wg, preferred_element_type=F32)
    hu = jnp.dot(x_bf, wu, preferred_element_type=F32)
    h = hg * (1.0 / (1.0 + jnp.exp(-hg))) * hu
    return jnp.dot(h.astype(BF16), wd, preferred_element_type=F32)


CLASS_LANE = N_EXPERTS + N_EXPERT_GROUPS
RANK_LANE = CLASS_LANE + 1


def _outproj_kernel(x_ref, po_ref, ao_ref, w_ref, g_ref, wr_ref, tri_ref, h_ref, xg_ref, cnt_ref, meta_ref):
    i = pl.program_id(0)
    d = x_ref.shape[1]
    mix = (jnp.dot(po_ref[...], w_ref[0:POOL_WIDTH, :], preferred_element_type=F32)
           + jnp.dot(ao_ref[...], w_ref[POOL_WIDTH:, :], preferred_element_type=F32))
    h = x_ref[...] + mix
    h_ref[...] = h
    xn = _rms(h, g_ref[...])
    gate, picks = _route(jnp.dot(xn.astype(BF16), wr_ref[...], preferred_element_type=F32))
    cls = _route_class(*picks)

    @pl.when(i == 0)
    def _():
        cnt_ref[...] = jnp.zeros_like(cnt_ref)

    lane = lax.broadcasted_iota(jnp.int32, gate.shape, 1)
    onehot = jnp.where(lane == cls, 1.0, 0.0)
    before = jnp.dot(tri_ref[...], onehot.astype(BF16), preferred_element_type=F32) + cnt_ref[0:1, :]
    rank = jnp.sum(onehot * before, axis=1, keepdims=True)
    cnt_ref[...] = cnt_ref[...] + jnp.sum(onehot, axis=0, keepdims=True)
    info = jnp.where(lane == CLASS_LANE, cls.astype(F32), jnp.where(lane == RANK_LANE, rank, gate))
    xg_ref[:, 0:d] = xn
    xg_ref[:, d:] = info
    meta_ref[...] = jnp.transpose(info)[META_ROW0:META_ROW0 + SUBLANES, :]


META_ROW0 = CLASS_LANE // SUBLANES * SUBLANES


def _outproj(x2d, po2d, ao2d, w_out_bf, g_ffn, w_router_bf, tm):
    T, D = x2d.shape
    row = lambda i: (i, 0)
    full = lambda shape: pl.BlockSpec(shape, lambda i: (0, 0))
    tri = np.tril(np.ones((tm, tm), np.float32), -1)
    return pl.pallas_call(
        _outproj_kernel,
        grid=(T // tm,),
        in_specs=[pl.BlockSpec((tm, D), row), pl.BlockSpec((tm, POOL_WIDTH), row),
                  pl.BlockSpec((tm, ATTN_WIDTH), row), full(w_out_bf.shape), full((1, D)),
                  full(w_router_bf.shape), full((tm, tm))],
        out_specs=(pl.BlockSpec((tm, D), row), pl.BlockSpec((tm, D + LANES), row), full((SUBLANES, LANES)),
                   pl.BlockSpec((None, SUBLANES, tm), lambda i: (i, 0, 0))),
        out_shape=(jax.ShapeDtypeStruct((T, D), F32), jax.ShapeDtypeStruct((T, D + LANES), F32),
                   jax.ShapeDtypeStruct((SUBLANES, LANES), F32),
                   jax.ShapeDtypeStruct((T // tm, SUBLANES, tm), F32)),
        compiler_params=_cparams(("arbitrary",)),
        name="prompt_outproj",
    )(x2d, po2d, ao2d, w_out_bf, g_ffn, w_router_bf, jnp.asarray(tri, BF16))


MOE_TILE = 512
ROW_TILE = 1024
DMA_UNROLL = 8


def _dispatch_kernel(dest_ref, xg_ref, zeros_ref, xs_ref, sem_ref):
    del zeros_ref
    n = xg_ref.shape[0]

    def start(g, carry):
        base = pl.multiple_of(g * DMA_UNROLL, DMA_UNROLL)
        rows = xg_ref.at[pl.ds(base, DMA_UNROLL)]
        for k in range(DMA_UNROLL):
            pltpu.make_async_copy(rows.at[pl.ds(k, 1)], xs_ref.at[pl.ds(dest_ref[0, base + k], 1)],
                                  sem_ref.at[0]).start(priority=k % 2)
        return carry

    lax.fori_loop(0, n // DMA_UNROLL, start, 0)
    pltpu.make_async_copy(xg_ref, xs_ref.at[pl.ds(0, n)], sem_ref.at[0]).wait()


def _dispatch(dest3d, xg, n_rows):
    T, W = xg.shape
    tm = dest3d.shape[2]
    return pl.pallas_call(
        _dispatch_kernel,
        grid=(T // tm,),
        in_specs=[pl.BlockSpec((None, 1, tm), lambda i: (i, 0, 0), memory_space=pltpu.SMEM),
                  pl.BlockSpec((tm, W), lambda i: (i, 0)),
                  pl.BlockSpec(memory_space=pl.ANY)],
        out_specs=pl.BlockSpec(memory_space=pl.ANY),
        out_shape=jax.ShapeDtypeStruct((n_rows, W), F32),
        scratch_shapes=[pltpu.SemaphoreType.DMA((1,))],
        input_output_aliases={2: 0},
        compiler_params=_cparams(("arbitrary",)),
        name="moe_dispatch",
    )(dest3d, xg, jnp.zeros((n_rows, W), F32))


def _moe_kernel(e1_ref, e2_ref, nused_ref, xs_ref, wg1_ref, wu1_ref, wd1_ref, wg2_ref, wu2_ref, wd2_ref, o_ref):
    i = pl.program_id(0)
    d = o_ref.shape[1]

    @pl.when(i < nused_ref[0])
    def _():
        x = xs_ref[:, 0:d].astype(BF16)
        gate = xs_ref[:, d:]
        o_ref[...] = (_gate_column(gate, e1_ref[i]) * _swiglu(x, wg1_ref[...], wu1_ref[...], wd1_ref[...])
                      + _gate_column(gate, e2_ref[i]) * _swiglu(x, wg2_ref[...], wu2_ref[...], wd2_ref[...]))

    @pl.when(i >= nused_ref[0])
    def _():
        o_ref[...] = jnp.zeros_like(o_ref)


def _moe(tile_e1, tile_e2, n_used, xs, wg_bf, wu_bf, wd_bf):
    n_rows, W = xs.shape
    E, D, DE = wg_bf.shape
    tile = lambda i, e1, e2, nu: (jnp.maximum(jnp.minimum(i, nu[0] - 1), 0), 0)
    first = lambda shape: pl.BlockSpec((None,) + shape, lambda i, e1, e2, nu: (e1[i], 0, 0))
    second = lambda shape: pl.BlockSpec((None,) + shape, lambda i, e1, e2, nu: (e2[i], 0, 0))
    return pl.pallas_call(
        _moe_kernel,
        grid_spec=pltpu.PrefetchScalarGridSpec(
            num_scalar_prefetch=3,
            grid=(n_rows // MOE_TILE,),
            in_specs=[pl.BlockSpec((MOE_TILE, W), tile),
                      first((D, DE)), first((D, DE)), first((DE, D)),
                      second((D, DE)), second((D, DE)), second((DE, D))],
            out_specs=pl.BlockSpec((MOE_TILE, D), lambda i, e1, e2, nu: (i, 0)),
        ),
        out_shape=jax.ShapeDtypeStruct((n_rows, D), F32),
        compiler_params=_cparams(("arbitrary",)),
        name="moe_experts",
    )(tile_e1, tile_e2, n_used, xs, wg_bf, wu_bf, wd_bf, wg_bf, wu_bf, wd_bf)


def _combine_copies_start(dest_ref, src_ref, buf_ref, sem_ref, slot):
    def start(g, carry):
        base = pl.multiple_of(g * DMA_UNROLL, DMA_UNROLL)
        rows = buf_ref.at[slot, pl.ds(base, DMA_UNROLL)]
        for k in range(DMA_UNROLL):
            pltpu.make_async_copy(src_ref.at[pl.ds(dest_ref[0, base + k], 1)], rows.at[pl.ds(k, 1)],
                                  sem_ref.at[slot]).start(priority=k % 2)
        return carry

    lax.fori_loop(0, buf_ref.shape[1] // DMA_UNROLL, start, 0)


def _combine_kernel(dest_ref, dest_next_ref, h_ref, gf_ref, moe_ref, y_ref, buf_ref, sem_ref):
    i = pl.program_id(0)
    n = buf_ref.shape[1]
    slot = i % 2

    @pl.when(i == 0)
    def _():
        _combine_copies_start(dest_ref, moe_ref, buf_ref, sem_ref, slot)

    @pl.when(i + 1 < pl.num_programs(0))
    def _():
        _combine_copies_start(dest_next_ref, moe_ref, buf_ref, sem_ref, 1 - slot)

    pltpu.make_async_copy(moe_ref.at[pl.ds(0, n)], buf_ref.at[slot], sem_ref.at[slot]).wait()
    y_ref[...] = _rms(h_ref[...] + buf_ref[slot], gf_ref[...])


def _combine(dest3d, h2d, g_final, moe_sorted):
    T, D = h2d.shape
    nt, _, tm = dest3d.shape
    return pl.pallas_call(
        _combine_kernel,
        grid=(nt,),
        in_specs=[pl.BlockSpec((None, 1, tm), lambda i: (i, 0, 0), memory_space=pltpu.SMEM),
                  pl.BlockSpec((None, 1, tm), lambda i: (jnp.minimum(i + 1, nt - 1), 0, 0),
                               memory_space=pltpu.SMEM),
                  pl.BlockSpec((tm, D), lambda i: (i, 0)),
                  pl.BlockSpec((1, D), lambda i: (0, 0)),
                  pl.BlockSpec(memory_space=pl.ANY)],
        out_specs=pl.BlockSpec((tm, D), lambda i: (i, 0)),
        out_shape=jax.ShapeDtypeStruct((T, D), F32),
        scratch_shapes=[pltpu.VMEM((2, tm, D), F32), pltpu.SemaphoreType.DMA((2,))],
        compiler_params=_cparams(("arbitrary",)),
        name="moe_combine",
    )(dest3d, dest3d, h2d, g_final, moe_sorted)


def _moe_layout(meta, counts):
    cls = meta[:, CLASS_LANE - META_ROW0, :].reshape(-1).astype(jnp.int32)
    rank = meta[:, RANK_LANE - META_ROW0, :].reshape(-1).astype(jnp.int32)
    cnt = counts[0, :N_CLASSES].astype(jnp.int32)
    tiles = (cnt + MOE_TILE - 1) // MOE_TILE
    tile_end = jnp.cumsum(tiles)
    row_start = (tile_end - tiles) * MOE_TILE
    dest = rank
    for c in range(N_CLASSES):
        dest = dest + jnp.where(cls == c, row_start[c], 0)
    max_tiles = cls.shape[0] // MOE_TILE + N_CLASSES
    tile_ids = jnp.arange(max_tiles, dtype=jnp.int32)
    tile_class = jnp.minimum(jnp.sum((tile_ids[:, None] >= tile_end[None, :]).astype(jnp.int32), axis=1),
                             N_CLASSES - 1)
    experts = jnp.asarray(CLASS_EXPERTS)[tile_class]
    return dest, experts[:, 0], experts[:, 1], tile_end[-1:].astype(jnp.int32), max_tiles * MOE_TILE


def _sample_inproj_kernel(x_ref, g_ref, w_ref, c_ref, slo_ref, shi_ref, sp_ref, wp_ref, ps_ref,
                          u_ref, k_ref, v_ref, po_ref, qt_ref, kt_ref, vt_ref):
    xn = _rms(x_ref[...], g_ref[...])

    def proj(i):
        return jnp.dot(xn, w_ref[:, i * POOL_WIDTH:(i + 1) * POOL_WIDTH],
                       preferred_element_type=F32, precision=HIGHEST)

    u, q, k, v = proj(0), proj(1), proj(2), proj(3)
    c, s_lo, s_hi = c_ref[...], slo_ref[...], shi_ref[...]
    q = _rotate(q, c, s_lo, s_hi)
    k = _rotate(k, c, s_lo, s_hi)
    u_ref[...] = u
    k_ref[...] = k
    v_ref[...] = v
    qt_ref[...] = jnp.transpose(q)
    kt_ref[...] = jnp.transpose(k)
    vt_ref[...] = jnp.transpose(v)
    db = sp_ref.shape[1]
    outs = []
    for g, w in enumerate(POOL_WINDOWS):
        sl = slice(g * POOL_GROUP, (g + 1) * POOL_GROUP)
        ug = u[0:db, sl]
        acc = ug
        for j in range(1, w):
            acc = acc + sp_ref[POOL_HIST - j, :, sl]
        d = acc / float(w) - ug
        outs.append(jnp.dot(d, wp_ref[g], preferred_element_type=F32, precision=HIGHEST))
    po_ref[...] = jnp.zeros_like(po_ref)
    po_ref[0:db, :] = jnp.concatenate(outs, axis=1) * ps_ref[...]


def _sample_inproj(x_pad, g_mix, w_in, pos, sp_t, w_pool, pool_scale):
    R, D = x_pad.shape
    c, s_lo, s_hi = _rope_tables(pos)
    row = jax.ShapeDtypeStruct((R, POOL_WIDTH), F32)
    col = jax.ShapeDtypeStruct((POOL_WIDTH, R), F32)
    return pl.pallas_call(
        _sample_inproj_kernel,
        out_shape=(row, row, row, row, col, col, col),
        compiler_params=pltpu.CompilerParams(vmem_limit_bytes=VMEM_LIMIT),
        name="sample_inproj",
    )(x_pad, g_mix, w_in, c, s_lo, s_hi, sp_t, w_pool, pool_scale)


def _column(mat, b):
    lane = lax.broadcasted_iota(jnp.int32, mat.shape, 1)
    col = jnp.sum(jnp.where(lane == b, mat, 0.0), axis=1, keepdims=True)
    return jnp.broadcast_to(col, mat.shape)


def _scan_copies(pt_ref, cache_ref, buf_ref, sem_ref, step, slot):
    n = buf_ref.shape[1]
    return [pltpu.make_async_copy(cache_ref.at[pt_ref[step * n + i]], buf_ref.at[slot, i], sem_ref.at[slot])
            for i in range(n)]


def _scan_step(step, nsteps, nch, pt_ref, qt_ref, cache_ref, top_ref, buf_ref, sem_ref, qb_sc, g_sc, nblk):
    b = step // nch
    ch = step - b * nch
    slot = step % 2

    @pl.when(step == 0)
    def _():
        for cp in _scan_copies(pt_ref, cache_ref, buf_ref, sem_ref, step, slot):
            cp.start()

    @pl.when(step + 1 < nsteps)
    def _():
        for cp in _scan_copies(pt_ref, cache_ref, buf_ref, sem_ref, step + 1, 1 - slot):
            cp.start()

    @pl.when(ch == 0)
    def _():
        qb_sc[...] = _column(qt_ref[...], b)
        g_sc[...] = jnp.zeros_like(g_sc)

    for cp in _scan_copies(pt_ref, cache_ref, buf_ref, sem_ref, step, slot):
        cp.wait()
    qb = qb_sc[...]
    lane = lax.broadcasted_iota(jnp.int32, (N_HEADS, LANES), 1)
    pages_per_blk = MOBA_BLOCK // buf_ref.shape[3]
    blks_per_step = buf_ref.shape[1] // pages_per_blk
    lane8 = lax.broadcasted_iota(jnp.int32, (SUBLANES, LANES), 1)
    g = [g_sc[h * SUBLANES:(h + 1) * SUBLANES, :] for h in range(N_HEADS)]
    for j in range(blks_per_step):
        tot = buf_ref[slot, j * pages_per_blk]
        for i in range(1, pages_per_blk):
            tot = tot + buf_ref[slot, j * pages_per_blk + i]
        pr = tot * qb
        for h in range(N_HEADS):
            a = pr[h * HEAD_DIM:h * HEAD_DIM + SUBLANES]
            for r in range(h * HEAD_DIM + SUBLANES, (h + 1) * HEAD_DIM, SUBLANES):
                a = a + pr[r:r + SUBLANES]
            g[h] = jnp.where(lane8 == ch * blks_per_step + j, jnp.sum(a, axis=1, keepdims=True), g[h])
    for h in range(N_HEADS):
        g_sc[h * SUBLANES:(h + 1) * SUBLANES, :] = g[h]

    @pl.when(ch == nch - 1)
    def _():
        full = jnp.concatenate([jnp.sum(gh, axis=0, keepdims=True) for gh in g], axis=0)
        gg = jnp.where(lane < nblk, full, NEG_INF)
        top = jnp.zeros((N_HEADS, LANES), jnp.int32)
        for t in range(MOBA_TOPK):
            m = jnp.max(gg, axis=1, keepdims=True)
            idx = jnp.min(jnp.where(gg == m, lane, BIG_IDX), axis=1, keepdims=True)
            top = jnp.where(lane == t, idx, top)
            gg = jnp.where(lane == idx, NEG_INF, gg)
        top_ref[...] = top


def _attn_copies(pt_ref, top_ref, ck_ref, cv_ref, kbuf, vbuf, sem_ref, b, slot, n_pages, pages_per_blk):
    cps = []
    for h in range(N_HEADS):
        rows = pl.ds(h * HEAD_DIM, HEAD_DIM)
        for t in range(MOBA_TOPK):
            blk = top_ref[(b * N_HEADS + h) * MOBA_TOPK + t]
            for i in range(pages_per_blk):
                phys = pt_ref[b * n_pages + blk * pages_per_blk + i]
                j = t * pages_per_blk + i
                cps.append(pltpu.make_async_copy(ck_ref.at[phys, rows, :], kbuf.at[slot, h, j], sem_ref.at[slot]))
                cps.append(pltpu.make_async_copy(cv_ref.at[phys, rows, :], vbuf.at[slot, h, j], sem_ref.at[slot]))
    return cps


def _sample_attn_kernel(pt_ref, top_ref, qt_ref, kt_ref, vt_ref, ck_ref, cv_ref, o_ref, kbuf, vbuf, sem_ref,
                        *, n_pages):
    b = pl.program_id(0)
    nb = pl.num_programs(0)
    slot = b % 2
    n_sel = kbuf.shape[2]
    copies = functools.partial(_attn_copies, pt_ref, top_ref, ck_ref, cv_ref, kbuf, vbuf, sem_ref,
                               n_pages=n_pages, pages_per_blk=n_sel // MOBA_TOPK)

    @pl.when(b == 0)
    def _():
        o_ref[...] = jnp.zeros_like(o_ref)
        for cp in copies(b=b, slot=slot):
            cp.start()

    @pl.when(b + 1 < nb)
    def _():
        for cp in copies(b=b + 1, slot=1 - slot):
            cp.start()

    for cp in copies(b=b, slot=slot):
        cp.wait()

    scale = HEAD_DIM ** -0.5
    lane = lax.broadcasted_iota(jnp.int32, (HEAD_DIM, LANES), 1)
    for h in range(N_HEADS):
        rows = slice(h * HEAD_DIM, (h + 1) * HEAD_DIM)
        qb = _column(qt_ref[rows, :], b)
        kb = _column(kt_ref[rows, :], b)
        vb = _column(vt_ref[rows, :], b)
        s = [jnp.sum(kbuf[slot, h, j] * qb, axis=0, keepdims=True) * scale for j in range(n_sel)]
        s_own = jnp.sum(kb * qb, axis=0, keepdims=True) * scale
        m = s_own
        for sj in s:
            m = jnp.maximum(m, jnp.max(sj, axis=1, keepdims=True))
        p = [jnp.exp(sj - m) for sj in s]
        p_own = jnp.exp(s_own - m)
        l = p_own
        acc = jnp.zeros((HEAD_DIM, LANES), F32)
        for j, pj in enumerate(p):
            l = l + jnp.sum(pj, axis=1, keepdims=True)
            acc = acc + vbuf[slot, h, j] * pj
        col = (jnp.sum(acc, axis=1, keepdims=True) + p_own * vb) / l
        o_ref[rows, :] = jnp.where(lane == b, col, o_ref[rows, :])


def _sample_attn(page_table_flat, top_flat, q_t, k_t, v_t, cache_kt, cache_vt, DB, n_pages):
    _, W, page = cache_kt.shape
    n_sel = MOBA_TOPK * MOBA_BLOCK // page
    vm = lambda: pl.BlockSpec(q_t.shape, lambda b, pt, tp: (0, 0))
    hbm = lambda: pl.BlockSpec(memory_space=pl.ANY)
    return pl.pallas_call(
        functools.partial(_sample_attn_kernel, n_pages=n_pages),
        grid_spec=pltpu.PrefetchScalarGridSpec(
            num_scalar_prefetch=2,
            grid=(DB,),
            in_specs=[vm(), vm(), vm(), hbm(), hbm()],
            out_specs=vm(),
            scratch_shapes=[pltpu.VMEM((2, N_HEADS, n_sel, HEAD_DIM, page), F32),
                            pltpu.VMEM((2, N_HEADS, n_sel, HEAD_DIM, page), F32),
                            pltpu.SemaphoreType.DMA((2,))],
        ),
        out_shape=jax.ShapeDtypeStruct(q_t.shape, F32),
        compiler_params=_cparams(("arbitrary",)),
        name="sample_attn",
    )(page_table_flat, top_flat, q_t, k_t, v_t, cache_kt, cache_vt)


def _sample_tail_kernel(x_ref, po_ref, aot_ref, w_ref, g_ref, wr_ref, wg_ref, wu_ref, wd_ref, gf_ref,
                        y_ref, h_sc, xn_sc, gate_sc, acc_sc):
    e = pl.program_id(0)

    @pl.when(e == 0)
    def _():
        ao = jnp.transpose(aot_ref[...])
        mix = (jnp.dot(po_ref[...], w_ref[0:POOL_WIDTH, :], preferred_element_type=F32, precision=HIGHEST)
               + jnp.dot(ao, w_ref[POOL_WIDTH:, :], preferred_element_type=F32, precision=HIGHEST))
        h = x_ref[...] + mix
        h_sc[...] = h
        xn = _rms(h, g_ref[...])
        xn_sc[...] = xn
        gate_sc[...] = _route(jnp.dot(xn, wr_ref[...], preferred_element_type=F32, precision=HIGHEST))[0]
        acc_sc[...] = jnp.zeros_like(acc_sc)

    o = _swiglu(xn_sc[...].astype(BF16), wg_ref[...], wu_ref[...], wd_ref[...])
    acc_sc[...] += _gate_column(gate_sc[...], e) * o

    @pl.when(e == pl.num_programs(0) - 1)
    def _():
        y_ref[...] = _rms(h_sc[...] + acc_sc[...], gf_ref[...])


def _sample_tail(x_pad, po, ao_t, w_out, g_ffn, w_router, wg_bf, wu_bf, wd_bf, g_final):
    R, D = x_pad.shape
    E, _, DE = wg_bf.shape
    full = lambda shape: pl.BlockSpec(shape, lambda e: (0,) * len(shape))
    return pl.pallas_call(
        _sample_tail_kernel,
        grid=(E,),
        in_specs=[full((R, D)), full((R, POOL_WIDTH)), full((ATTN_WIDTH, R)), full(w_out.shape),
                  full((1, D)), full(w_router.shape),
                  pl.BlockSpec((None, D, DE), lambda e: (e, 0, 0)),
                  pl.BlockSpec((None, D, DE), lambda e: (e, 0, 0)),
                  pl.BlockSpec((None, DE, D), lambda e: (e, 0, 0)),
                  full((1, D))],
        out_specs=full((R, D)),
        out_shape=jax.ShapeDtypeStruct((R, D), F32),
        scratch_shapes=[pltpu.VMEM((R, D), F32), pltpu.VMEM((R, D), F32), pltpu.VMEM((R, LANES), F32),
                        pltpu.VMEM((R, D), F32)],
        compiler_params=_cparams(("arbitrary",)),
        name="sample_tail",
    )(x_pad, po, ao_t, w_out, g_ffn, w_router, wg_bf, wu_bf, wd_bf, g_final)


def _tile(n, pref):
    while n % pref:
        pref //= 2
    return pref


def kernel(x_prompt, x_sample, cache_k, cache_v, state_pool, page_table, norm_mix, w_in, w_pool, pool_scale,
           w_out, norm_ffn, w_router_group, w_router_expert, w_gate, w_up, w_down, norm_final):
    B, S, D = x_prompt.shape
    DB, DS, _ = x_sample.shape
    depth, n_phys, page = cache_k.shape[:3]
    assert depth == 1 and DS == 1, "single layer, one new token per sequence"
    n_pages = page_table.shape[1]
    past_len = n_pages * page
    assert S % MOBA_BLOCK == 0 and past_len % MOBA_BLOCK == 0 and MOBA_BLOCK % page == 0
    assert past_len // MOBA_BLOCK >= MOBA_TOPK and DB <= LANES
    T = B * S

    g_mix = norm_mix[0][None, :]
    g_ffn = norm_ffn[0][None, :]
    g_final = norm_final[None, :]
    ps = pool_scale[0][None, :]
    w_router = jnp.concatenate(
        [w_router_expert[0], w_router_group[0],
         jnp.zeros((D, LANES - N_EXPERTS - N_EXPERT_GROUPS), F32)], axis=1)
    wg_bf, wu_bf, wd_bf = w_gate[0].astype(BF16), w_up[0].astype(BF16), w_down[0].astype(BF16)

    xs = x_sample.reshape(DB, D)
    x_pad = jnp.pad(xs, ((0, LANES - DB), (0, 0)))
    pos_s = jnp.full((1,), past_len, jnp.int32)
    sp_t = jnp.swapaxes(state_pool[0], 0, 1)
    u_s, k_s, v_s, po_s, q_t, k_t, v_t = _sample_inproj(x_pad, g_mix, w_in[0], pos_s, sp_t, w_pool[0], ps)
    cache_kt = jnp.transpose(cache_k[0], (0, 2, 3, 1)).reshape(n_phys, ATTN_WIDTH, page)
    cache_vt = jnp.transpose(cache_v[0], (0, 2, 3, 1)).reshape(n_phys, ATTN_WIDTH, page)
    pt_flat = page_table.reshape(-1).astype(jnp.int32)

    ts = _tile(S, 512)
    kt_f, vt_f, q_bf, kt_bf, vt_bf, po, ksum, u_last = _inproj(
        x_prompt, g_mix, w_in[0], w_pool[0].astype(BF16), ps, ts)
    ksum = ksum[:, :, :ts // MOBA_BLOCK].reshape(B, S // MOBA_BLOCK, ATTN_WIDTH)
    ao, top = _moba_prompt(q_bf, kt_bf, vt_bf, ksum, pt_flat, q_t, cache_kt, DB, n_pages)
    h2, xg, counts, meta = _outproj(x_prompt.reshape(T, D), po.reshape(T, POOL_WIDTH), ao.reshape(T, ATTN_WIDTH),
                                    w_out[0].astype(BF16), g_ffn, w_router.astype(BF16), _tile(T, 512))
    dest, tile_e1, tile_e2, n_used, n_rows = _moe_layout(meta, counts)
    dest3d = dest.reshape(T // ROW_TILE, 1, ROW_TILE)
    x_grouped = _dispatch(dest3d, xg, n_rows)
    moe_sorted = _moe(tile_e1, tile_e2, n_used, x_grouped, wg_bf, wu_bf, wd_bf)
    y_prompt = _combine(dest3d, h2, g_final, moe_sorted).reshape(B, S, D)
    k_prompt = jnp.transpose(kt_f.reshape(B, N_HEADS, HEAD_DIM, S), (0, 3, 1, 2))[None]
    v_prompt = jnp.transpose(vt_f.reshape(B, N_HEADS, HEAD_DIM, S), (0, 3, 1, 2))[None]
    pool_prompt = u_last[None, :, HALO - POOL_HIST:, :]

    top_flat = top[:, :, :MOBA_TOPK].reshape(-1)
    ao_t = _sample_attn(pt_flat, top_flat, q_t, k_t, v_t, cache_kt, cache_vt, DB, n_pages)
    y_pad = _sample_tail(x_pad, po_s, ao_t, w_out[0], g_ffn, w_router, wg_bf, wu_bf, wd_bf, g_final)
    y_sample = y_pad[:DB].reshape(DB, 1, D)
    k_sample = k_s[:DB].reshape(1, DB, 1, N_HEADS, HEAD_DIM)
    v_sample = v_s[:DB].reshape(1, DB, 1, N_HEADS, HEAD_DIM)
    pool_sample = jnp.concatenate([state_pool[0][:, 1:], u_s[:DB, None, :]], axis=1)[None]
    return (y_prompt, y_sample, k_prompt, v_prompt, pool_prompt, k_sample, v_sample, pool_sample)
```
